```python
import math
import jax, jax.numpy as jnp
from jax import lax
import numpy as np

D_MODEL = 1024
BATCH = 2
SEQ = 16384
DEPTH = 2

GRID_W = 64
N_HEADS_A = 8
HEAD_DIM = 64
W_ATTN = N_HEADS_A * HEAD_DIM
WIN_H = 8
WIN_W = 16
N_BLOCKS_B = 8
BLOCK_W = 64
W_LRU = N_BLOCKS_B * BLOCK_W
CONV_W = 4
C_LRU = 8.0
SPLITS = (W_ATTN, W_ATTN, W_ATTN, W_ATTN, W_LRU, W_LRU, D_MODEL, D_MODEL)
D_IN = sum(SPLITS)
ALPHA = (2 * DEPTH) ** 0.25
BETA = (8 * DEPTH) ** -0.25
LN_EPS = 1e-5

kernel_name = "hybrid_natten_rglru_deepnorm_encoder"


def layer_norm(x, g, b):
    xf = x.astype(jnp.float32)
    mu = jnp.mean(xf, axis=-1, keepdims=True)
    var = jnp.mean(jnp.square(xf - mu), axis=-1, keepdims=True)
    y = (xf - mu) * lax.rsqrt(var + LN_EPS)
    return (y * g.astype(jnp.float32) + b.astype(jnp.float32)).astype(x.dtype)


def neighbourhood_attention(q, k, v, rpb):
    B, S, H, Dh = q.shape
    rows = S // GRID_W
    kh = min(WIN_H, rows)
    r = jnp.arange(rows)
    c = jnp.arange(GRID_W)
    rs = jnp.clip(r - kh // 2, 0, rows - kh)
    cs = jnp.clip(c - WIN_W // 2, 0, GRID_W - WIN_W)
    key_r = rs[:, None] + jnp.arange(kh)[None, :]
    key_c = cs[:, None] + jnp.arange(WIN_W)[None, :]
    dr = key_r - r[:, None] + (WIN_H - 1)
    dc = key_c - c[:, None] + (WIN_W - 1)
    n_keys = kh * WIN_W
    scale = Dh ** -0.5
    q_rows = (q * scale).reshape(B, rows, GRID_W, H, Dh).transpose(1, 0, 2, 3, 4)

    def row_fn(args):
        q_row, kr, drr = args
        idx = (kr[None, :, None] * GRID_W + key_c[:, None, :]).reshape(GRID_W, n_keys)
        k_g = k[:, idx]
        v_g = v[:, idx]
        bias = rpb[:, drr[:, None, None], dc[None, :, :]]
        bias = bias.transpose(0, 2, 1, 3).reshape(H, GRID_W, n_keys).astype(jnp.float32)
        s = jnp.einsum('bqhd,bqnhd->bhqn', q_row, k_g).astype(jnp.float32) + bias[None]
        p = jax.nn.softmax(s, axis=-1).astype(v.dtype)
        return jnp.einsum('bhqn,bqnhd->bqhd', p, v_g)

    o = lax.map(row_fn, (q_rows, key_r, dr))
    return o.transpose(1, 0, 2, 3, 4).reshape(B, S, H * Dh)


def centred_depthwise_conv(u, w, b):
    S = u.shape[1]
    left = CONV_W // 2
    up = jnp.pad(u, ((0, 0), (left, CONV_W - 1 - left), (0, 0)))
    out = sum(up[:, j:j + S] * w[j] for j in range(CONV_W))
    return out + b


def _lin_combine(e1, e2):
    a1, b1 = e1
    a2, b2 = e2
    return a1 * a2, a2 * b1 + b2


def rg_lru(x, w_gate, b_gate, lam, reverse):
    B, S, W = x.shape
    xb = x.reshape(B, S, N_BLOCKS_B, BLOCK_W)
    g = jnp.einsum('bsnd,gnde->gbsne', xb, w_gate) + b_gate[:, None, None]
    g = jax.nn.sigmoid(g.astype(jnp.float32)).reshape(2, B, S, W)
    r_gate, i_gate = g[0], g[1]
    log_a = -C_LRU * jax.nn.softplus(-lam.astype(jnp.float32)) * r_gate
    a = jnp.exp(log_a)
    mult = jnp.sqrt(-jnp.expm1(2.0 * log_a))
    first = S - 1 if reverse else 0
    pos = jnp.arange(S)[None, :, None]
    mult = jnp.where(pos == first, 1.0, mult)
    bx = mult * i_gate * x.astype(jnp.float32)
    _, h = lax.associative_scan(_lin_combine, (a, bx), axis=1, reverse=reverse)
    return h.astype(x.dtype)


def hybrid_layer(x, w_in, rpb, conv_w, conv_b, gate_w, gate_b, lam, w_ba, w_bb, b_merge, w_out, ln_g, ln_b):
    B, S, D = x.shape
    proj = x @ w_in
    offs = list(np.cumsum(SPLITS)[:-1])
    q, k, v, z_a, u, z_b, g_a, g_b = jnp.split(proj, offs, axis=-1)
    hs = (B, S, N_HEADS_A, HEAD_DIM)
    attn = neighbourhood_attention(q.reshape(hs), k.reshape(hs), v.reshape(hs), rpb)
    y_a = attn * jax.nn.silu(z_a)
    u = centred_depthwise_conv(u, conv_w, conv_b)
    h = rg_lru(u, gate_w[0], gate_b[0], lam[0], False) + rg_lru(u, gate_w[1], gate_b[1], lam[1], True)
    y_b = h * jax.nn.silu(z_b)
    m = jax.nn.sigmoid(g_a + b_merge[0]) * (y_a @ w_ba) + jax.nn.sigmoid(g_b + b_merge[1]) * (y_b @ w_bb)
    out = m @ w_out
    return layer_norm(ALPHA * x + out, ln_g, ln_b)


def setup_inputs(seed: int = 0) -> dict:
    key = jax.random.key(seed)
    ks = jax.random.split(key, 16)
    f32 = jnp.float32
    D = D_MODEL
    x = jax.random.normal(ks[0], (BATCH, SEQ, D), f32)
    emb_ln_g = 1.0 + 0.01 * jax.random.normal(ks[1], (D,), f32)
    emb_ln_b = 0.01 * jax.random.normal(ks[2], (D,), f32)
    w_in = jax.random.normal(ks[3], (DEPTH, D, D_IN), f32) * D ** -0.5
    rpb = 0.1 * jax.random.normal(ks[4], (DEPTH, N_HEADS_A, 2 * WIN_H - 1, 2 * WIN_W - 1), f32)
    conv_w = jax.random.normal(ks[5], (DEPTH, CONV_W, W_LRU), f32) * CONV_W ** -0.5
    conv_b = 0.01 * jax.random.normal(ks[6], (DEPTH, W_LRU), f32)
    lru_gate_w = jax.random.normal(ks[7], (DEPTH, 2, 2, N_BLOCKS_B, BLOCK_W, BLOCK_W), f32) * BLOCK_W ** -0.5
    lru_gate_b = 0.01 * jax.random.normal(ks[8], (DEPTH, 2, 2, N_BLOCKS_B, BLOCK_W), f32)
    a_c = jax.random.uniform(ks[9], (DEPTH, 2, W_LRU), f32, 0.9, 0.999)
    a_base = a_c ** (1.0 / C_LRU)
    lru_lambda = jnp.log(a_base) - jnp.log1p(-a_base)
    w_branch_attn = jax.random.normal(ks[10], (DEPTH, W_ATTN, D), f32) * (W_ATTN ** -0.5) * BETA
    w_branch_lru = jax.random.normal(ks[11], (DEPTH, W_LRU, D), f32) * (W_LRU ** -0.5) * BETA
    b_merge = 0.01 * jax.random.normal(ks[12], (DEPTH, 2, D), f32)
    w_out = jax.random.normal(ks[13], (DEPTH, D, D), f32) * (D ** -0.5) * BETA
    ln_g = 1.0 + 0.01 * jax.random.normal(ks[14], (DEPTH, D), f32)
    ln_b = 0.01 * jax.random.normal(ks[15], (DEPTH, D), f32)
    return {"x": x, "emb_ln_g": emb_ln_g, "emb_ln_b": emb_ln_b, "w_in": w_in, "rpb": rpb,
            "conv_w": conv_w, "conv_b": conv_b, "lru_gate_w": lru_gate_w, "lru_gate_b": lru_gate_b,
            "lru_lambda": lru_lambda, "w_branch_attn": w_branch_attn, "w_branch_lru": w_branch_lru,
            "b_merge": b_merge, "w_out": w_out, "ln_g": ln_g, "ln_b": ln_b}


def reference(x, emb_ln_g, emb_ln_b, w_in, rpb, conv_w, conv_b, lru_gate_w, lru_gate_b, lru_lambda,
              w_branch_attn, w_branch_lru, b_merge, w_out, ln_g, ln_b):
    h = layer_norm(x, emb_ln_g, emb_ln_b)
    for l in range(DEPTH):
        h = hybrid_layer(h, w_in[l], rpb[l], conv_w[l], conv_b[l], lru_gate_w[l], lru_gate_b[l],
                         lru_lambda[l], w_branch_attn[l], w_branch_lru[l], b_merge[l], w_out[l],
                         ln_g[l], ln_b[l])
    return h
```

```python
import functools

import numpy as np
import jax
import jax.numpy as jnp
from jax import lax
from jax.experimental import pallas as pl
from jax.experimental.pallas import tpu as pltpu

D_MODEL = 1024
GRID_W = 64
N_HEADS = 8
HEAD_DIM = 64
W_ATTN = N_HEADS * HEAD_DIM
WIN_H = 8
WIN_W = 16
W_LRU = 512
N_BLOCKS = 8
BLOCK_W = 64
CONV_W = 4
C_LRU = 8.0
LN_EPS = 1e-5
NEG = -1e30

LANES = 128
SUBLANES = 8
VMEM_LIMIT = 56 * 1024 * 1024

TM = 512
Q_ROWS = 8
K_ROWS = 16
KV_BLK_ROWS = 4
QC = 32
KC = 48
KC_START = (0, 16)
TS = 512
N_CHUNK = SUBLANES
PITCH = TS // N_CHUNK + 8

_f32 = jnp.float32
_bf16 = jnp.bfloat16


def _dot(a, b):
    return jnp.dot(a, b, preferred_element_type=_f32)


def _layer_norm(x, g, b):
    mu = jnp.mean(x, axis=-1, keepdims=True)
    xc = x - mu
    var = jnp.mean(xc * xc, axis=-1, keepdims=True)
    return xc * lax.rsqrt(var + LN_EPS) * g + b


def _sigmoid(x):
    return 1.0 / (1.0 + jnp.exp(-x))


def _in_proj_kernel(x_ref, g_ref, b_ref, w_ref, *out_refs, apply_ln):
    x = x_ref[...]
    if apply_ln:
        h_ref, out_refs = out_refs[0], out_refs[1:]
        x = _layer_norm(x, g_ref[...], b_ref[...])
        h_ref[...] = x
    q_ref, k_ref, v_ref, za_ref, u_ref, zb_ref, ga_ref, gb_ref = out_refs
    xb = x.astype(_bf16)
    off = 0
    for ref, scale in ((q_ref, HEAD_DIM ** -0.5), (k_ref, None), (v_ref, None), (za_ref, None),
                       (u_ref, None), (zb_ref, None), (ga_ref, None), (gb_ref, None)):
        n = ref.shape[-1]
        y = _dot(xb, w_ref[:, off:off + n])
        if scale is not None:
            y = y * scale
        ref[...] = y.astype(ref.dtype)
        off += n


def _in_proj(x2, ln_g, ln_b, w_in_bf16, apply_ln):
    t = x2.shape[0]
    row = lambda i: (i, 0)
    const = lambda i: (0, 0)
    widths = (W_ATTN, W_ATTN, W_ATTN, W_ATTN, W_LRU, W_LRU, D_MODEL, D_MODEL)
    dtypes = (_bf16, _bf16, _bf16, _f32, _f32, _f32, _f32, _f32)
    out_shape = [jax.ShapeDtypeStruct((t, n), dt) for n, dt in zip(widths, dtypes)]
    out_specs = [pl.BlockSpec((TM, n), row) for n in widths]
    if apply_ln:
        out_shape = [jax.ShapeDtypeStruct((t, D_MODEL), _f32)] + out_shape
        out_specs = [pl.BlockSpec((TM, D_MODEL), row)] + out_specs
    return pl.pallas_call(
        functools.partial(_in_proj_kernel, apply_ln=apply_ln),
        grid=(t // TM,),
        in_specs=[pl.BlockSpec((TM, D_MODEL), row),
                  pl.BlockSpec((1, D_MODEL), const),
                  pl.BlockSpec((1, D_MODEL), const),
                  pl.BlockSpec(w_in_bf16.shape, const)],
        out_specs=out_specs,
        out_shape=out_shape,
        compiler_params=pltpu.CompilerParams(dimension_semantics=("arbitrary",),
                                             vmem_limit_bytes=VMEM_LIMIT),
        name="in_proj",
    )(x2, ln_g, ln_b, w_in_bf16)


def _attn_bias_tables(rpb, rows):
    r0s = (0, Q_ROWS, rows - Q_ROWS)
    dr = np.zeros((3, Q_ROWS, K_ROWS), np.int32)
    mr = np.zeros((3, Q_ROWS, K_ROWS), bool)
    for v, r0 in enumerate(r0s):
        for qr in range(Q_ROWS):
            r = r0 + qr
            rs = min(max(r - WIN_H // 2, 0), rows - WIN_H)
            for kr in range(K_ROWS):
                key_r = r0 - WIN_H // 2 + kr
                if rs <= key_r < rs + WIN_H:
                    mr[v, qr, kr] = True
                    dr[v, qr, kr] = key_r - r + WIN_H - 1
    n_ct = len(KC_START)
    dc = np.zeros((n_ct, QC, KC), np.int32)
    mc = np.zeros((n_ct, QC, KC), bool)
    for ct, k0 in enumerate(KC_START):
        for qc in range(QC):
            c = ct * QC + qc
            cs = min(max(c - WIN_W // 2, 0), GRID_W - WIN_W)
            for kc in range(KC):
                key_c = k0 + kc
                if cs <= key_c < cs + WIN_W:
                    mc[ct, qc, kc] = True
                    dc[ct, qc, kc] = key_c - c + WIN_W - 1
    t = rpb.astype(_f32)[:, dr]
    t = t[..., dc]
    mask = mr[:, :, :, None, None, None] & mc[None, None, None]
    t = jnp.where(mask[None], t, NEG)
    t = t.reshape(N_HEADS // 2, 2, 3, Q_ROWS, K_ROWS, n_ct, QC, KC)
    t = t.transpose(2, 0, 5, 1, 3, 6, 4, 7)
    return t.reshape(3, N_HEADS // 2, n_ct, 2 * Q_ROWS * QC, K_ROWS * KC)


def _attn_kernel(q_ref, k0_ref, k1_ref, k2_ref, k3_ref, v0_ref, v1_ref, v2_ref, v3_ref,
                 z_ref, bias_ref, o_ref):
    k_refs = (k0_ref, k1_ref, k2_ref, k3_ref)
    v_refs = (v0_ref, v1_ref, v2_ref, v3_ref)
    m_q = Q_ROWS * QC
    lane = lax.broadcasted_iota(jnp.int32, (m_q, LANES), 1)
    even = lane < HEAD_DIM
    for ct, k0 in enumerate(KC_START):
        for pair in range(N_HEADS // 2):
            cols = slice(pair * LANES, (pair + 1) * LANES)

            def key_tile(refs):
                parts = []
                for kr in range(K_ROWS):
                    base = (kr % KV_BLK_ROWS) * GRID_W + k0
                    parts.append(refs[kr // KV_BLK_ROWS][base:base + KC, cols])
                return jnp.concatenate(parts, axis=0)

            kt = key_tile(k_refs)
            vt = key_tile(v_refs)
            qrows = [slice(qr * GRID_W + ct * QC, qr * GRID_W + (ct + 1) * QC) for qr in range(Q_ROWS)]
            qp = jnp.concatenate([q_ref[rs, cols] for rs in qrows], axis=0)
            zero = jnp.zeros_like(qp)
            q2 = jnp.concatenate([jnp.where(even, qp, zero), jnp.where(even, zero, qp)], axis=0)
            s = lax.dot_general(q2, kt, (((1,), (1,)), ((), ())), preferred_element_type=_f32)
            s = s + bias_ref[pair, ct]
            m = jnp.max(s, axis=-1, keepdims=True)
            e = jnp.exp(s - m)
            l = jnp.sum(e, axis=-1, keepdims=True)
            o2 = _dot(e.astype(_bf16), vt)
            o = jnp.where(even, o2[:m_q] / l[:m_q], o2[m_q:] / l[m_q:])
            z = jnp.concatenate([z_ref[rs, cols] for rs in qrows], axis=0)
            y = (o * (z * _sigmoid(z))).astype(o_ref.dtype)
            for qr, rs in enumerate(qrows):
                o_ref[rs, cols] = y[qr * QC:(qr + 1) * QC]


def _attention(q, k, v, z_a, bias):
    b, s, _ = q.shape
    rows = s // GRID_W
    n_blk = rows // Q_ROWS
    n_kv = rows // KV_BLK_ROWS
    tq = Q_ROWS * GRID_W
    tkv = KV_BLK_ROWS * GRID_W

    def kv_spec(j):
        first = -(WIN_H // 2) // KV_BLK_ROWS
        return pl.BlockSpec(
            (None, tkv, W_ATTN),
            lambda bi, i: (bi, jnp.clip(i * (Q_ROWS // KV_BLK_ROWS) + first + j, 0, n_kv - 1), 0))

    def case(i):
        return jnp.where(i == 0, 0, jnp.where(i == n_blk - 1, 2, 1))

    blk = pl.BlockSpec((None, tq, W_ATTN), lambda bi, i: (bi, i, 0))
    return pl.pallas_call(
        _attn_kernel,
        grid=(b, n_blk),
        in_specs=[blk] + [kv_spec(j) for j in range(4)] + [kv_spec(j) for j in range(4)] + [
            blk,
            pl.BlockSpec((None,) + bias.shape[1:], lambda bi, i: (case(i), 0, 0, 0, 0))],
        out_specs=blk,
        out_shape=jax.ShapeDtypeStruct((b, s, W_ATTN), _bf16),
        compiler_params=pltpu.CompilerParams(dimension_semantics=("arbitrary", "arbitrary"),
                                             vmem_limit_bytes=VMEM_LIMIT),
        name="attn",
    )(q, k, k, k, k, v, v, v, v, z_a, bias)


def _lru_direction(u_ref, prev_ref, next_ref, is_first_blk, is_last_blk, cw, cb, wg_ref, bg_ref,
                   coef, carry_ref, ut_ref, ot_ref, o_ref, *, reverse):
    g_len = TS // N_CHUNK
    n_lg = W_LRU // LANES
    sub = lax.broadcasted_iota(jnp.int32, (N_CHUNK, W_LRU), 0)
    for ch in range(N_CHUNK):
        for lg in range(n_lg):
            ut_ref[lg, ch * PITCH:ch * PITCH + g_len, :] = u_ref[ch * g_len:(ch + 1) * g_len,
                                                                lg * LANES:(lg + 1) * LANES]
    u = [jnp.concatenate([ut_ref[lg, pl.ds(g, N_CHUNK, stride=PITCH), :] for lg in range(n_lg)], axis=1)
         for g in range(g_len)]

    def halo(row, blocked):
        return jnp.broadcast_to(jnp.where(blocked, 0.0, row), (N_CHUNK, W_LRU))

    um2 = jnp.where(sub == 0, halo(prev_ref[SUBLANES - 2:SUBLANES - 1, :], is_first_blk),
                    pltpu.roll(u[g_len - 2], 1, axis=0))
    um1 = jnp.where(sub == 0, halo(prev_ref[SUBLANES - 1:SUBLANES, :], is_first_blk),
                    pltpu.roll(u[g_len - 1], 1, axis=0))
    up1 = jnp.where(sub == N_CHUNK - 1, halo(next_ref[0:1, :], is_last_blk),
                    pltpu.roll(u[0], N_CHUNK - 1, axis=0))
    ext = [um2, um1] + u + [up1]
    uc = [ext[g] * cw[0] + ext[g + 1] * cw[1] + ext[g + 2] * cw[2] + ext[g + 3] * cw[3] + cb
          for g in range(g_len)]
    u2 = jnp.concatenate(uc, axis=0)
    ub = u2.astype(_bf16)
    half = W_LRU // 2
    gates = []
    for gi in range(2):
        cols = [_dot(ub[:, hf * half:(hf + 1) * half], wg_ref[gi, hf]) for hf in range(2)]
        gates.append(jnp.concatenate(cols, axis=1) + bg_ref[gi])
    r_gate = _sigmoid(gates[0])
    i_gate = _sigmoid(gates[1])
    log_a = coef * r_gate
    a = jnp.exp(log_a)
    mult = jnp.sqrt((1.0 - a) * (1.0 + a))
    row = lax.broadcasted_iota(jnp.int32, (TS, W_LRU), 0)
    if reverse:
        at_first = jnp.logical_and(is_last_blk, row == TS - 1)
    else:
        at_first = jnp.logical_and(is_first_blk, row == 0)
    mult = jnp.where(at_first, 1.0, mult)
    bx = mult * i_gate * u2

    h = jnp.zeros((N_CHUNK, W_LRU), _f32)
    p = jnp.ones((N_CHUNK, W_LRU), _f32)
    h_loc = [None] * g_len
    p_loc = [None] * g_len
    order = range(g_len - 1, -1, -1) if reverse else range(g_len)
    for g in order:
        ag = a[g * N_CHUNK:(g + 1) * N_CHUNK]
        h = ag * h + bx[g * N_CHUNK:(g + 1) * N_CHUNK]
        p = ag * p
        h_loc[g] = h
        p_loc[g] = p
    h_end, p_end = h, p

    if reverse:
        c = jnp.where(sub == N_CHUNK - 1, jnp.broadcast_to(carry_ref[0:1, :], (N_CHUNK, W_LRU)), 0.0)
        for kk in range(N_CHUNK - 2, -1, -1):
            c = jnp.where(sub == kk, pltpu.roll(p_end * c + h_end, N_CHUNK - 1, axis=0), c)
    else:
        c = jnp.where(sub == 0, jnp.broadcast_to(carry_ref[N_CHUNK - 1:N_CHUNK, :], (N_CHUNK, W_LRU)), 0.0)
        for kk in range(1, N_CHUNK):
            c = jnp.where(sub == kk, pltpu.roll(p_end * c + h_end, 1, axis=0), c)
    carry_ref[...] = p_end * c + h_end

    for g in range(g_len):
        hg = p_loc[g] * c + h_loc[g]
        for lg in range(n_lg):
            ot_ref[lg, pl.ds(g, N_CHUNK, stride=PITCH), :] = hg[:, lg * LANES:(lg + 1) * LANES]
    for ch in range(N_CHUNK):
        for lg in range(n_lg):
            o_ref[ch * g_len:(ch + 1) * g_len, lg * LANES:(lg + 1) * LANES] = (
                ot_ref[lg, ch * PITCH:ch * PITCH + g_len, :])


def _lru_kernel(uf_ref, pf_ref, nf_ref, ub_ref, pb_ref, nb_ref, cw_ref, cb_ref, wg_ref, bg_ref, lam_ref,
                of_ref, ob_ref, carry_f, carry_b, ut_f, ot_f, ut_b, ot_b):
    j = pl.program_id(1)
    n = pl.num_programs(1)

    @pl.when(j == 0)
    def _():
        carry_f[...] = jnp.zeros_like(carry_f)
        carry_b[...] = jnp.zeros_like(carry_b)

    cw = [cw_ref[i:i + 1, :] for i in range(CONV_W)]
    cb = cb_ref[...]
    neg_lam = -lam_ref[...]
    softplus = jnp.maximum(neg_lam, 0.0) + jnp.log(1.0 + jnp.exp(-jnp.abs(neg_lam)))
    coef = -C_LRU * softplus
    _lru_direction(uf_ref, pf_ref, nf_ref, j == 0, j == n - 1, cw, cb, wg_ref.at[0], bg_ref.at[0],
                   coef[0:1], carry_f, ut_f, ot_f, of_ref, reverse=False)
    _lru_direction(ub_ref, pb_ref, nb_ref, j == n - 1, j == 0, cw, cb, wg_ref.at[1], bg_ref.at[1],
                   coef[1:2], carry_b, ut_b, ot_b, ob_ref, reverse=True)


def _block_diag_halves(w):
    per_half = N_BLOCKS // 2
    eye = jnp.eye(per_half, dtype=w.dtype)
    w = w.reshape(w.shape[:-3] + (2, per_half, BLOCK_W, BLOCK_W))
    full = jnp.einsum('...hbde,bc->...hbdce', w, eye)
    return full.reshape(w.shape[:-3] + (per_half * BLOCK_W, per_half * BLOCK_W)).astype(_bf16)


def _lru(u, conv_w, conv_b, gate_w, gate_b, lam):
    b, s, _ = u.shape
    n = s // TS
    per = TS // SUBLANES
    n8 = s // SUBLANES
    wg = _block_diag_halves(gate_w)
    bg = gate_b.reshape(2, 2, 1, W_LRU)
    cur = lambda f: pl.BlockSpec((None, TS, W_LRU), lambda bi, j: (bi, f(j), 0))
    prev = lambda f: pl.BlockSpec((None, SUBLANES, W_LRU),
                                  lambda bi, j: (bi, jnp.maximum(f(j) * per - 1, 0), 0))
    nxt = lambda f: pl.BlockSpec((None, SUBLANES, W_LRU),
                                 lambda bi, j: (bi, jnp.minimum((f(j) + 1) * per, n8 - 1), 0))
    fwd = lambda j: j
    bwd = lambda j: n - 1 - j
    full = lambda a: pl.BlockSpec(a.shape, lambda bi, j: (0,) * a.ndim)
    cb2 = conv_b.reshape(1, W_LRU)
    return pl.pallas_call(
        _lru_kernel,
        grid=(b, n),
        in_specs=[cur(fwd), prev(fwd), nxt(fwd), cur(bwd), prev(bwd), nxt(bwd),
                  full(conv_w), full(cb2), full(wg), full(bg), full(lam)],
        out_specs=[cur(fwd), cur(bwd)],
        out_shape=[jax.ShapeDtypeStruct((b, s, W_LRU), _f32)] * 2,
        scratch_shapes=[pltpu.VMEM((N_CHUNK, W_LRU), _f32)] * 2
        + [pltpu.VMEM((W_LRU // LANES, N_CHUNK * PITCH, LANES), _f32)] * 4,
        compiler_params=pltpu.CompilerParams(dimension_semantics=("arbitrary", "arbitrary"),
                                             vmem_limit_bytes=VMEM_LIMIT),
        name="lru",
    )(u, u, u, u, u, u, conv_w, cb2, wg, bg, lam)


def _out_proj_kernel(x_ref, ya_ref, hf_ref, hb_ref, zb_ref, ga_ref, gb_ref, wba_ref, wbb_ref, bm_ref,
                     wo_ref, g_ref, b_ref, o_ref, *, alpha):
    zb = zb_ref[...]
    yb = ((hf_ref[...] + hb_ref[...]) * (zb * _sigmoid(zb))).astype(_bf16)
    pa = _dot(ya_ref[...], wba_ref[...])
    pb = _dot(yb, wbb_ref[...])
    m = _sigmoid(ga_ref[...] + bm_ref[0:1, :]) * pa + _sigmoid(gb_ref[...] + bm_ref[1:2, :]) * pb
    out = _dot(m.astype(_bf16), wo_ref[...])
    o_ref[...] = _layer_norm(alpha * x_ref[...] + out, g_ref[...], b_ref[...])


def _out_proj(x2, y_a, h_f, h_b, z_b, g_a, g_b, w_ba, w_bb, b_merge, w_out, ln_g, ln_b, alpha):
    t = x2.shape[0]
    row = lambda i: (i, 0)
    const = lambda i: (0, 0)
    rows = lambda n: pl.BlockSpec((TM, n), row)
    full = lambda a: pl.BlockSpec(a.shape, const)
    return pl.pallas_call(
        functools.partial(_out_proj_kernel, alpha=alpha),
        grid=(t // TM,),
        in_specs=[rows(D_MODEL), rows(W_ATTN), rows(W_LRU), rows(W_LRU), rows(W_LRU), rows(D_MODEL),
                  rows(D_MODEL), full(w_ba), full(w_bb), full(b_merge), full(w_out), full(ln_g), full(ln_b)],
        out_specs=rows(D_MODEL),
        out_shape=jax.ShapeDtypeStruct((t, D_MODEL), _f32),
        compiler_params=pltpu.CompilerParams(dimension_semantics=("arbitrary",),
                                             vmem_limit_bytes=VMEM_LIMIT),
        name="out_proj",
    )(x2, y_a, h_f, h_b, z_b, g_a, g_b, w_ba, w_bb, b_merge, w_out, ln_g, ln_b)


def kernel(x, emb_ln_g, emb_ln_b, w_in, rpb, conv_w, conv_b, lru_gate_w, lru_gate_b, lru_lambda,
           w_branch_attn, w_branch_lru, b_merge, w_out, ln_g, ln_b):
    b, s, d = x.shape
    depth = w_in.shape[0]
    assert d == D_MODEL and s % (Q_ROWS * GRID_W) == 0 and s % TS == 0 and (b * s) % TM == 0
    alpha = (2 * depth) ** 0.25
    t = b * s
    h = x.reshape(t, d)
    row2 = lambda a: a.reshape(1, -1)
    for l in range(depth):
        outs = _in_proj(h, row2(emb_ln_g), row2(emb_ln_b), w_in[l].astype(_bf16), apply_ln=(l == 0))
        if l == 0:
            h, outs = outs[0], outs[1:]
        q, k, v, z_a, u, z_b, g_a, g_b = outs
        seq = lambda a: a.reshape(b, s, a.shape[-1])
        bias = _attn_bias_tables(rpb[l], s // GRID_W)
        y_a = _attention(seq(q), seq(k), seq(v), seq(z_a), bias)
        h_f, h_b = _lru(seq(u), conv_w[l], conv_b[l], lru_gate_w[l], lru_gate_b[l], lru_lambda[l])
        flat = lambda a: a.reshape(t, a.shape[-1])
        h = _out_proj(h, flat(y_a), flat(h_f), flat(h_b), z_b, g_a, g_b,
                      w_branch_attn[l].astype(_bf16), w_branch_lru[l].astype(_bf16), b_merge[l],
                      w_out[l].astype(_bf16), row2(ln_g[l]), row2(ln_b[l]), alpha)
    return h.reshape(b, s, d)
```

```python
import functools

import numpy as np
import jax
import jax.numpy as jnp
from jax import lax
from jax.experimental import pallas as pl
from jax.experimental.pallas import tpu as pltpu

D_MODEL = 1024
GRID_W = 64
N_HEADS = 8
HEAD_DIM = 64
W_ATTN = N_HEADS * HEAD_DIM
WIN_H = 8
WIN_W = 16
W_LRU = 512
N_BLOCKS = 8
BLOCK_W = 64
CONV_W = 4
C_LRU = 8.0
LN_EPS = 1e-5
NEG = -1e30

LANES = 128
SUBLANES = 8
VMEM_LIMIT = 56 * 1024 * 1024

TM = 512
Q_ROWS = 8
SUB_ROWS = 4
KEY_ROWS = SUB_ROWS + WIN_H
KV_BLK_ROWS = 4
TS = 512
N_CHUNK = SUBLANES
PITCH = TS // N_CHUNK + 8

_f32 = jnp.float32
_bf16 = jnp.bfloat16


def _dot(a, b):
    return jnp.dot(a, b, preferred_element_type=_f32)


def _layer_norm(x, g, b):
    mu = jnp.mean(x, axis=-1, keepdims=True)
    xc = x - mu
    var = jnp.mean(xc * xc, axis=-1, keepdims=True)
    return xc * lax.rsqrt(var + LN_EPS) * g + b


def _sigmoid(x):
    return 1.0 / (1.0 + jnp.exp(-x))


def _in_proj_kernel(x_ref, g_ref, b_ref, w_ref, *out_refs, apply_ln):
    x = x_ref[...]
    if apply_ln:
        h_ref, out_refs = out_refs[0], out_refs[1:]
        x = _layer_norm(x, g_ref[...], b_ref[...])
        h_ref[...] = x
    q_ref, k_ref, v_ref, za_ref, u_ref, zb_ref, ga_ref, gb_ref = out_refs
    xb = x.astype(_bf16)
    off = 0
    for ref, scale in ((q_ref, HEAD_DIM ** -0.5), (k_ref, None), (v_ref, None), (za_ref, None),
                       (u_ref, None), (zb_ref, None), (ga_ref, None), (gb_ref, None)):
        n = ref.shape[-1]
        y = _dot(xb, w_ref[:, off:off + n])
        if scale is not None:
            y = y * scale
        ref[...] = y.astype(ref.dtype)
        off += n


def _in_proj(x2, ln_g, ln_b, w_in_bf16, apply_ln):
    t = x2.shape[0]
    row = lambda i: (i, 0)
    const = lambda i: (0, 0)
    widths = (W_ATTN, W_ATTN, W_ATTN, W_ATTN, W_LRU, W_LRU, D_MODEL, D_MODEL)
    dtypes = (_bf16, _bf16, _bf16, _f32, _f32, _f32, _f32, _f32)
    out_shape = [jax.ShapeDtypeStruct((t, n), dt) for n, dt in zip(widths, dtypes)]
    out_specs = [pl.BlockSpec((TM, n), row) for n in widths]
    if apply_ln:
        out_shape = [jax.ShapeDtypeStruct((t, D_MODEL), _f32)] + out_shape
        out_specs = [pl.BlockSpec((TM, D_MODEL), row)] + out_specs
    return pl.pallas_call(
        functools.partial(_in_proj_kernel, apply_ln=apply_ln),
        grid=(t // TM,),
        in_specs=[pl.BlockSpec((TM, D_MODEL), row),
                  pl.BlockSpec((1, D_MODEL), const),
                  pl.BlockSpec((1, D_MODEL), const),
                  pl.BlockSpec(w_in_bf16.shape, const)],
        out_specs=out_specs,
        out_shape=out_shape,
        compiler_params=pltpu.CompilerParams(dimension_semantics=("arbitrary",),
                                             vmem_limit_bytes=VMEM_LIMIT),
        name="in_proj",
    )(x2, ln_g, ln_b, w_in_bf16)


def _attn_value_table(rpb):
    qc = np.arange(GRID_W)[:, None]
    kc = np.arange(GRID_W)[None, :]
    cs = np.clip(qc - WIN_W // 2, 0, GRID_W - WIN_W)
    ok = (kc >= cs) & (kc < cs + WIN_W)
    dc = np.where(ok, kc - qc + WIN_W - 1, 0)
    t = jnp.where(ok, rpb.astype(_f32)[:, :, dc], NEG)
    return jnp.concatenate([t[:, :-1], t[:, 1:]], axis=-1)


def _attn_row_mask(rows):
    r_first = (0, Q_ROWS, rows - Q_ROWS)
    m = np.full((3, Q_ROWS // SUB_ROWS, SUB_ROWS, SUBLANES, KEY_ROWS * GRID_W), NEG, np.float32)
    for v, r0 in enumerate(r_first):
        for half in range(Q_ROWS // SUB_ROWS):
            for qr in range(SUB_ROWS):
                r = r0 + half * SUB_ROWS + qr
                rs = min(max(r - WIN_H // 2, 0), rows - WIN_H)
                for kr in range(KEY_ROWS):
                    key_r = r0 + half * SUB_ROWS - WIN_H // 2 + kr
                    if rs <= key_r < rs + WIN_H:
                        m[v, half, qr, :, kr * GRID_W:(kr + 1) * GRID_W] = 0.0
    return m


def _attn_kernel(q_ref, k0_ref, k1_ref, k2_ref, k3_ref, v0_ref, v1_ref, v2_ref, v3_ref,
                 z_ref, val_ref, rowm_ref, o_ref):
    k_refs = (k0_ref, k1_ref, k2_ref, k3_ref)
    v_refs = (v0_ref, v1_ref, v2_ref, v3_ref)
    m_q = SUB_ROWS * GRID_W
    n_kb = KEY_ROWS // KV_BLK_ROWS
    lane = lax.broadcasted_iota(jnp.int32, (m_q, LANES), 1)
    even = lane < HEAD_DIM
    for half in range(Q_ROWS // SUB_ROWS):
        rows = slice(half * m_q, (half + 1) * m_q)
        for pair in range(N_HEADS // 2):
            cols = slice(pair * LANES, (pair + 1) * LANES)
            kt = jnp.concatenate([k_refs[half + j][:, cols] for j in range(n_kb)], axis=0)
            vt = jnp.concatenate([v_refs[half + j][:, cols] for j in range(n_kb)], axis=0)
            qp = q_ref[rows, cols]
            zero = jnp.zeros_like(qp)
            q2 = jnp.concatenate([jnp.where(even, qp, zero), jnp.where(even, zero, qp)], axis=0)
            s = lax.dot_general(q2, kt, (((1,), (1,)), ((), ())), preferred_element_type=_f32)
            blocks = []
            for parity in range(2):
                for qr in range(SUB_ROWS):
                    val = jnp.concatenate(
                        [val_ref[2 * pair + parity, 2 * j - qr + WIN_H // 2 - 1]
                         for j in range(KEY_ROWS // 2)], axis=1)
                    r0 = (parity * SUB_ROWS + qr) * GRID_W
                    blk = s[r0:r0 + GRID_W] + val
                    blk = blk.reshape(GRID_W // SUBLANES, SUBLANES, -1) + rowm_ref[half, qr][None]
                    blocks.append(blk.reshape(GRID_W, -1))
            s = jnp.concatenate(blocks, axis=0)
            m = jnp.max(s, axis=-1, keepdims=True)
            e = jnp.exp(s - m)
            l = jnp.sum(e, axis=-1, keepdims=True)
            o2 = _dot(e.astype(_bf16), vt)
            o = jnp.where(even, o2[:m_q] / l[:m_q], o2[m_q:] / l[m_q:])
            z = z_ref[rows, cols]
            o_ref[rows, cols] = (o * (z * _sigmoid(z))).astype(o_ref.dtype)


def _attention(q, k, v, z_a, val, rowm):
    b, s, _ = q.shape
    rows = s // GRID_W
    n_blk = rows // Q_ROWS
    n_kv = rows // KV_BLK_ROWS
    tq = Q_ROWS * GRID_W
    tkv = KV_BLK_ROWS * GRID_W

    def kv_spec(j):
        first = -(WIN_H // 2) // KV_BLK_ROWS
        return pl.BlockSpec(
            (None, tkv, W_ATTN),
            lambda bi, i: (bi, jnp.clip(i * (Q_ROWS // KV_BLK_ROWS) + first + j, 0, n_kv - 1), 0))

    def case(i):
        return jnp.where(i == 0, 0, jnp.where(i == n_blk - 1, 2, 1))

    blk = pl.BlockSpec((None, tq, W_ATTN), lambda bi, i: (bi, i, 0))
    return pl.pallas_call(
        _attn_kernel,
        grid=(b, n_blk),
        in_specs=[blk] + [kv_spec(j) for j in range(4)] + [kv_spec(j) for j in range(4)] + [
            blk,
            pl.BlockSpec(val.shape, lambda bi, i: (0, 0, 0, 0)),
            pl.BlockSpec((None,) + rowm.shape[1:], lambda bi, i: (case(i), 0, 0, 0, 0))],
        out_specs=blk,
        out_shape=jax.ShapeDtypeStruct((b, s, W_ATTN), _bf16),
        compiler_params=pltpu.CompilerParams(dimension_semantics=("arbitrary", "arbitrary"),
                                             vmem_limit_bytes=VMEM_LIMIT),
        name="attn",
    )(q, k, k, k, k, v, v, v, v, z_a, val, rowm)


def _lru_direction(u_ref, prev_ref, next_ref, is_first_blk, is_last_blk, cw, cb, wg_ref, bg_ref,
                   coef, carry_ref, ut_ref, ot_ref, o_ref, *, reverse):
    g_len = TS // N_CHUNK
    n_lg = W_LRU // LANES
    sub = lax.broadcasted_iota(jnp.int32, (N_CHUNK, W_LRU), 0)
    for ch in range(N_CHUNK):
        for lg in range(n_lg):
            ut_ref[lg, ch * PITCH:ch * PITCH + g_len, :] = u_ref[ch * g_len:(ch + 1) * g_len,
                                                                lg * LANES:(lg + 1) * LANES]
    u = [jnp.concatenate([ut_ref[lg, pl.ds(g, N_CHUNK, stride=PITCH), :] for lg in range(n_lg)], axis=1)
         for g in range(g_len)]

    def halo(row, blocked):
        return jnp.broadcast_to(jnp.where(blocked, 0.0, row), (N_CHUNK, W_LRU))

    um2 = jnp.where(sub == 0, halo(prev_ref[SUBLANES - 2:SUBLANES - 1, :], is_first_blk),
                    pltpu.roll(u[g_len - 2], 1, axis=0))
    um1 = jnp.where(sub == 0, halo(prev_ref[SUBLANES - 1:SUBLANES, :], is_first_blk),
                    pltpu.roll(u[g_len - 1], 1, axis=0))
    up1 = jnp.where(sub == N_CHUNK - 1, halo(next_ref[0:1, :], is_last_blk),
                    pltpu.roll(u[0], N_CHUNK - 1, axis=0))
    ext = [um2, um1] + u + [up1]
    uc = [ext[g] * cw[0] + ext[g + 1] * cw[1] + ext[g + 2] * cw[2] + ext[g + 3] * cw[3] + cb
          for g in range(g_len)]
    u2 = jnp.concatenate(uc, axis=0)
    ub = u2.astype(_bf16)
    half = W_LRU // 2
    gates = []
    for gi in range(2):
        cols = [_dot(ub[:, hf * half:(hf + 1) * half], wg_ref[gi, hf]) for hf in range(2)]
        gates.append(jnp.concatenate(cols, axis=1) + bg_ref[gi])
    r_gate = _sigmoid(gates[0])
    i_gate = _sigmoid(gates[1])
    log_a = coef * r_gate
    a = jnp.exp(log_a)
    mult = jnp.sqrt((1.0 - a) * (1.0 + a))
    row = lax.broadcasted_iota(jnp.int32, (TS, W_LRU), 0)
    if reverse:
        at_first = jnp.logical_and(is_last_blk, row == TS - 1)
    else:
        at_first = jnp.logical_and(is_first_blk, row == 0)
    mult = jnp.where(at_first, 1.0, mult)
    bx = mult * i_gate * u2

    h = jnp.zeros((N_CHUNK, W_LRU), _f32)
    p = jnp.ones((N_CHUNK, W_LRU), _f32)
    h_loc = [None] * g_len
    p_loc = [None] * g_len
    order = range(g_len - 1, -1, -1) if reverse else range(g_len)
    for g in order:
        ag = a[g * N_CHUNK:(g + 1) * N_CHUNK]
        h = ag * h + bx[g * N_CHUNK:(g + 1) * N_CHUNK]
        p = ag * p
        h_loc[g] = h
        p_loc[g] = p
    h_end, p_end = h, p

    if reverse:
        c = jnp.where(sub == N_CHUNK - 1, jnp.broadcast_to(carry_ref[0:1, :], (N_CHUNK, W_LRU)), 0.0)
        for kk in range(N_CHUNK - 2, -1, -1):
            c = jnp.where(sub == kk, pltpu.roll(p_end * c + h_end, N_CHUNK - 1, axis=0), c)
    else:
        c = jnp.where(sub == 0, jnp.broadcast_to(carry_ref[N_CHUNK - 1:N_CHUNK, :], (N_CHUNK, W_LRU)), 0.0)
        for kk in range(1, N_CHUNK):
            c = jnp.where(sub == kk, pltpu.roll(p_end * c + h_end, 1, axis=0), c)
    carry_ref[...] = p_end * c + h_end

    for g in range(g_len):
        hg = p_loc[g] * c + h_loc[g]
        for lg in range(n_lg):
            ot_ref[lg, pl.ds(g, N_CHUNK, stride=PITCH), :] = hg[:, lg * LANES:(lg + 1) * LANES]
    for ch in range(N_CHUNK):
        for lg in range(n_lg):
            o_ref[ch * g_len:(ch + 1) * g_len, lg * LANES:(lg + 1) * LANES] = (
                ot_ref[lg, ch * PITCH:ch * PITCH + g_len, :])


def _lru_kernel(uf_ref, pf_ref, nf_ref, ub_ref, pb_ref, nb_ref, cw_ref, cb_ref, wg_ref, bg_ref, lam_ref,
                of_ref, ob_ref, carry_f, carry_b, ut_f, ot_f, ut_b, ot_b):
    j = pl.program_id(1)
    n = pl.num_programs(1)

    @pl.when(j == 0)
    def _():
        carry_f[...] = jnp.zeros_like(carry_f)
        carry_b[...] = jnp.zeros_like(carry_b)

    cw = [cw_ref[i:i + 1, :] for i in range(CONV_W)]
    cb = cb_ref[...]
    neg_lam = -lam_ref[...]
    softplus = jnp.maximum(neg_lam, 0.0) + jnp.log(1.0 + jnp.exp(-jnp.abs(neg_lam)))
    coef = -C_LRU * softplus
    _lru_direction(uf_ref, pf_ref, nf_ref, j == 0, j == n - 1, cw, cb, wg_ref.at[0], bg_ref.at[0],
                   coef[0:1], carry_f, ut_f, ot_f, of_ref, reverse=False)
    _lru_direction(ub_ref, pb_ref, nb_ref, j == n - 1, j == 0, cw, cb, wg_ref.at[1], bg_ref.at[1],
                   coef[1:2], carry_b, ut_b, ot_b, ob_ref, reverse=True)


def _block_diag_halves(w):
    per_half = N_BLOCKS // 2
    eye = jnp.eye(per_half, dtype=w.dtype)
    w = w.reshape(w.shape[:-3] + (2, per_half, BLOCK_W, BLOCK_W))
    full = jnp.einsum('...hbde,bc->...hbdce', w, eye)
    return full.reshape(w.shape[:-3] + (per_half * BLOCK_W, per_half * BLOCK_W)).astype(_bf16)


def _lru(u, conv_w, conv_b, gate_w, gate_b, lam):
    b, s, _ = u.shape
    n = s // TS
    per = TS // SUBLANES
    n8 = s // SUBLANES
    wg = _block_diag_halves(gate_w)
    bg = gate_b.reshape(2, 2, 1, W_LRU)
    cur = lambda f: pl.BlockSpec((None, TS, W_LRU), lambda bi, j: (bi, f(j), 0))
    prev = lambda f: pl.BlockSpec((None, SUBLANES, W_LRU),
                                  lambda bi, j: (bi, jnp.maximum(f(j) * per - 1, 0), 0))
    nxt = lambda f: pl.BlockSpec((None, SUBLANES, W_LRU),
                                 lambda bi, j: (bi, jnp.minimum((f(j) + 1) * per, n8 - 1), 0))
    fwd = lambda j: j
    bwd = lambda j: n - 1 - j
    full = lambda a: pl.BlockSpec(a.shape, lambda bi, j: (0,) * a.ndim)
    cb2 = conv_b.reshape(1, W_LRU)
    return pl.pallas_call(
        _lru_kernel,
        grid=(b, n),
        in_specs=[cur(fwd), prev(fwd), nxt(fwd), cur(bwd), prev(bwd), nxt(bwd),
                  full(conv_w), full(cb2), full(wg), full(bg), full(lam)],
        out_specs=[cur(fwd), cur(bwd)],
        out_shape=[jax.ShapeDtypeStruct((b, s, W_LRU), _f32)] * 2,
        scratch_shapes=[pltpu.VMEM((N_CHUNK, W_LRU), _f32)] * 2
        + [pltpu.VMEM((W_LRU // LANES, N_CHUNK * PITCH, LANES), _f32)] * 4,
        compiler_params=pltpu.CompilerParams(dimension_semantics=("arbitrary", "arbitrary"),
                                             vmem_limit_bytes=VMEM_LIMIT),
        name="lru",
    )(u, u, u, u, u, u, conv_w, cb2, wg, bg, lam)


def _out_proj_kernel(x_ref, ya_ref, hf_ref, hb_ref, zb_ref, ga_ref, gb_ref, wba_ref, wbb_ref, bm_ref,
                     wo_ref, g_ref, b_ref, o_ref, *, alpha):
    zb = zb_ref[...]
    yb = ((hf_ref[...] + hb_ref[...]) * (zb * _sigmoid(zb))).astype(_bf16)
    pa = _dot(ya_ref[...], wba_ref[...])
    pb = _dot(yb, wbb_ref[...])
    m = _sigmoid(ga_ref[...] + bm_ref[0:1, :]) * pa + _sigmoid(gb_ref[...] + bm_ref[1:2, :]) * pb
    out = _dot(m.astype(_bf16), wo_ref[...])
    o_ref[...] = _layer_norm(alpha * x_ref[...] + out, g_ref[...], b_ref[...])


def _out_proj(x2, y_a, h_f, h_b, z_b, g_a, g_b, w_ba, w_bb, b_merge, w_out, ln_g, ln_b, alpha):
    t = x2.shape[0]
    row = lambda i: (i, 0)
    const = lambda i: (0, 0)
    rows = lambda n: pl.BlockSpec((TM, n), row)
    full = lambda a: pl.BlockSpec(a.shape, const)
    return pl.pallas_call(
        functools.partial(_out_proj_kernel, alpha=alpha),
        grid=(t // TM,),
        in_specs=[rows(D_MODEL), rows(W_ATTN), rows(W_LRU), rows(W_LRU), rows(W_LRU), rows(D_MODEL),
                  rows(D_MODEL), full(w_ba), full(w_bb), full(b_merge), full(w_out), full(ln_g), full(ln_b)],
        out_specs=rows(D_MODEL),
        out_shape=jax.ShapeDtypeStruct((t, D_MODEL), _f32),
        compiler_params=pltpu.CompilerParams(dimension_semantics=("arbitrary",),
                                             vmem_limit_bytes=VMEM_LIMIT),
        name="out_proj",
    )(x2, y_a, h_f, h_b, z_b, g_a, g_b, w_ba, w_bb, b_merge, w_out, ln_g, ln_b)


def kernel(x, emb_ln_g, emb_ln_b, w_in, rpb, conv_w, conv_b, lru_gate_w, lru_gate_b, lru_lambda,
           w_branch_attn, w_branch_lru, b_merge, w_out, ln_g, ln_b):
    b, s, d = x.shape
    depth = w_in.shape[0]
    assert d == D_MODEL and s % (Q_ROWS * GRID_W) == 0 and s % TS == 0 and (b * s) % TM == 0
    alpha = (2 * depth) ** 0.25
    t = b * s
    h = x.reshape(t, d)
    row2 = lambda a: a.reshape(1, -1)
    for l in range(depth):
        outs = _in_proj(h, row2(emb_ln_g), row2(emb_ln_b), w_in[l].astype(_bf16), apply_ln=(l == 0))
        if l == 0:
            h, outs = outs[0], outs[1:]
        q, k, v, z_a, u, z_b, g_a, g_b = outs
        seq = lambda a: a.reshape(b, s, a.shape[-1])
        y_a = _attention(seq(q), seq(k), seq(v), seq(z_a), _attn_value_table(rpb[l]),
                         _attn_row_mask(s // GRID_W))
        h_f, h_b = _lru(seq(u), conv_w[l], conv_b[l], lru_gate_w[l], lru_gate_b[l], lru_lambda[l])
        flat = lambda a: a.reshape(t, a.shape[-1])
        h = _out_proj(h, flat(y_a), flat(h_f), flat(h_b), z_b, g_a, g_b,
                      w_branch_attn[l].astype(_bf16), w_branch_lru[l].astype(_bf16), b_merge[l],
                      w_out[l].astype(_bf16), row2(ln_g[l]), row2(ln_b[l]), alpha)
    return h.reshape(b, s, d)
```

```python
import functools

import numpy as np
import jax
import jax.numpy as jnp
from jax import lax
from jax.experimental import pallas as pl
from jax.experimental.pallas import tpu as pltpu

D_MODEL = 1024
GRID_W = 64
N_HEADS = 8
HEAD_DIM = 64
W_ATTN = N_HEADS * HEAD_DIM
WIN_H = 8
WIN_W = 16
W_LRU = 512
N_BLOCKS = 8
BLOCK_W = 64
CONV_W = 4
C_LRU = 8.0
LN_EPS = 1e-5
NEG = -1e30
TINY = 1e-30
LOG2_E = 1.4426950408889634

LANES = 128
SUBLANES = 8
VMEM_LIMIT = 56 * 1024 * 1024

TM = 512
Q_ROWS = 8
SUB_ROWS = 4
KEY_ROWS = SUB_ROWS + WIN_H
KV_BLK_ROWS = 4
TS = 512
N_CHUNK = SUBLANES
PITCH = TS // N_CHUNK + 8

_f32 = jnp.float32
_bf16 = jnp.bfloat16


def _dot(a, b):
    return jnp.dot(a, b, preferred_element_type=_f32)


def _layer_norm(x, g, b):
    mu = jnp.mean(x, axis=-1, keepdims=True)
    xc = x - mu
    var = jnp.mean(xc * xc, axis=-1, keepdims=True)
    return xc * lax.rsqrt(var + LN_EPS) * g + b


def _sigmoid(x):
    return 0.5 + 0.5 * jnp.tanh(0.5 * x)


def _silu(x):
    hx = 0.5 * x
    return hx + hx * jnp.tanh(hx)


def _in_proj_kernel(x_ref, g_ref, b_ref, w_ref, bm_ref, *out_refs, apply_ln):
    x = x_ref[...]
    if apply_ln:
        h_ref, out_refs = out_refs[0], out_refs[1:]
        x = _layer_norm(x, g_ref[...], b_ref[...])
        h_ref[...] = x
    xb = x.astype(_bf16)
    post = (lambda y: y * HEAD_DIM ** -0.5, None, None, _silu, None, _silu,
            lambda y: _sigmoid(y + bm_ref[0:1, :]), lambda y: _sigmoid(y + bm_ref[1:2, :]))
    off = 0
    for ref, fn in zip(out_refs, post):
        n = ref.shape[-1]
        y = _dot(xb, w_ref[:, off:off + n])
        if fn is not None:
            y = fn(y)
        ref[...] = y.astype(ref.dtype)
        off += n


def _in_proj(x2, ln_g, ln_b, w_in_bf16, b_merge, apply_ln):
    t = x2.shape[0]
    row = lambda i: (i, 0)
    const = lambda i: (0, 0)
    widths = (W_ATTN, W_ATTN, W_ATTN, W_ATTN, W_LRU, W_LRU, D_MODEL, D_MODEL)
    dtypes = (_bf16, _bf16, _bf16, _bf16, _f32, _bf16, _bf16, _bf16)
    out_shape = [jax.ShapeDtypeStruct((t, n), dt) for n, dt in zip(widths, dtypes)]
    out_specs = [pl.BlockSpec((TM, n), row) for n in widths]
    if apply_ln:
        out_shape = [jax.ShapeDtypeStruct((t, D_MODEL), _f32)] + out_shape
        out_specs = [pl.BlockSpec((TM, D_MODEL), row)] + out_specs
    return pl.pallas_call(
        functools.partial(_in_proj_kernel, apply_ln=apply_ln),
        grid=(t // TM,),
        in_specs=[pl.BlockSpec((TM, D_MODEL), row),
                  pl.BlockSpec((1, D_MODEL), const),
                  pl.BlockSpec((1, D_MODEL), const),
                  pl.BlockSpec(w_in_bf16.shape, const),
                  pl.BlockSpec(b_merge.shape, const)],
        out_specs=out_specs,
        out_shape=out_shape,
        compiler_params=pltpu.CompilerParams(dimension_semantics=("arbitrary",),
                                             vmem_limit_bytes=VMEM_LIMIT),
        name="in_proj",
    )(x2, ln_g, ln_b, w_in_bf16, b_merge)


def _attn_value_table(rpb):
    qc = np.arange(GRID_W)[:, None]
    kc = np.arange(GRID_W)[None, :]
    cs = np.clip(qc - WIN_W // 2, 0, GRID_W - WIN_W)
    ok = (kc >= cs) & (kc < cs + WIN_W)
    pad = GRID_W - WIN_W
    padded = jnp.pad(rpb.astype(_f32), ((0, 0), (0, 0), (pad, pad)))
    t = jnp.stack([padded[:, :, GRID_W - 1 - c:2 * GRID_W - 1 - c] for c in range(GRID_W)], axis=2)
    t = jnp.where(ok, t, NEG)
    return jnp.concatenate([t[:, :-1], t[:, 1:]], axis=-1)


def _attn_row_mask(rows):
    r_first = (0, Q_ROWS, rows - Q_ROWS)
    m = np.full((3, Q_ROWS // SUB_ROWS, SUB_ROWS, SUBLANES, KEY_ROWS * GRID_W), NEG, np.float32)
    for v, r0 in enumerate(r_first):
        for half in range(Q_ROWS // SUB_ROWS):
            for qr in range(SUB_ROWS):
                r = r0 + half * SUB_ROWS + qr
                rs = min(max(r - WIN_H // 2, 0), rows - WIN_H)
                for kr in range(KEY_ROWS):
                    key_r = r0 + half * SUB_ROWS - WIN_H // 2 + kr
                    if rs <= key_r < rs + WIN_H:
                        m[v, half, qr, :, kr * GRID_W:(kr + 1) * GRID_W] = 0.0
    return m


def _attn_kernel(q_ref, k0_ref, k1_ref, k2_ref, k3_ref, v0_ref, v1_ref, v2_ref, v3_ref,
                 z_ref, val_ref, rowm_ref, o_ref):
    k_refs = (k0_ref, k1_ref, k2_ref, k3_ref)
    v_refs = (v0_ref, v1_ref, v2_ref, v3_ref)
    m_q = SUB_ROWS * GRID_W
    n_kb = KEY_ROWS // KV_BLK_ROWS
    lane = lax.broadcasted_iota(jnp.int32, (m_q, LANES), 1)
    even = lane < HEAD_DIM
    for half in range(Q_ROWS // SUB_ROWS):
        rows = slice(half * m_q, (half + 1) * m_q)
        for pair in range(N_HEADS // 2):
            cols = slice(pair * LANES, (pair + 1) * LANES)
            kt = jnp.concatenate([k_refs[half + j][:, cols] for j in range(n_kb)], axis=0)
            vt = jnp.concatenate([v_refs[half + j][:, cols] for j in range(n_kb)], axis=0)
            qp = q_ref[rows, cols]
            zero = jnp.zeros_like(qp)
            q2 = jnp.concatenate([jnp.where(even, qp, zero), jnp.where(even, zero, qp)], axis=0)
            s = lax.dot_general(q2, kt, (((1,), (1,)), ((), ())), preferred_element_type=_f32)
            blocks = []
            for parity in range(2):
                for qr in range(SUB_ROWS):
                    val = jnp.concatenate(
                        [val_ref[2 * pair + parity, 2 * j - qr + WIN_H // 2 - 1]
                         for j in range(KEY_ROWS // 2)], axis=1)
                    r0 = (parity * SUB_ROWS + qr) * GRID_W
                    blk = s[r0:r0 + GRID_W] + val
                    blk = blk.reshape(GRID_W // SUBLANES, SUBLANES, -1) + rowm_ref[half, qr][None]
                    blocks.append(blk.reshape(GRID_W, -1))
            s = jnp.concatenate(blocks, axis=0)
            m = jnp.max(s, axis=-1, keepdims=True)
            e = jnp.exp(s - m)
            l = jnp.sum(e, axis=-1, keepdims=True)
            o2 = _dot(e.astype(_bf16), vt)
            o = jnp.where(even, o2[:m_q] / l[:m_q], o2[m_q:] / l[m_q:])
            o_ref[rows, cols] = (o * z_ref[rows, cols].astype(_f32)).astype(o_ref.dtype)


def _attention(q, k, v, z_a, val, rowm):
    b, s, _ = q.shape
    rows = s // GRID_W
    n_blk = rows // Q_ROWS
    n_kv = rows // KV_BLK_ROWS
    tq = Q_ROWS * GRID_W
    tkv = KV_BLK_ROWS * GRID_W

    def kv_spec(j):
        first = -(WIN_H // 2) // KV_BLK_ROWS
        return pl.BlockSpec(
            (None, tkv, W_ATTN),
            lambda bi, i: (bi, jnp.clip(i * (Q_ROWS // KV_BLK_ROWS) + first + j, 0, n_kv - 1), 0))

    def case(i):
        return jnp.where(i == 0, 0, jnp.where(i == n_blk - 1, 2, 1))

    blk = pl.BlockSpec((None, tq, W_ATTN), lambda bi, i: (bi, i, 0))
    return pl.pallas_call(
        _attn_kernel,
        grid=(b, n_blk),
        in_specs=[blk] + [kv_spec(j) for j in range(4)] + [kv_spec(j) for j in range(4)] + [
            blk,
            pl.BlockSpec(val.shape, lambda bi, i: (0, 0, 0, 0)),
            pl.BlockSpec((None,) + rowm.shape[1:], lambda bi, i: (case(i), 0, 0, 0, 0))],
        out_specs=blk,
        out_shape=jax.ShapeDtypeStruct((b, s, W_ATTN), _bf16),
        compiler_params=pltpu.CompilerParams(dimension_semantics=("arbitrary", "arbitrary"),
                                             vmem_limit_bytes=VMEM_LIMIT),
        name="attn",
    )(q, k, k, k, k, v, v, v, v, z_a, val, rowm)


def _lru_direction(u_ref, prev_ref, next_ref, is_first_blk, is_last_blk, cw, cb, wg_ref, bg_ref,
                   coef, carry_ref, ut_ref, ot_ref, o_ref, *, reverse):
    g_len = TS // N_CHUNK
    n_lg = W_LRU // LANES
    sub = lax.broadcasted_iota(jnp.int32, (N_CHUNK, W_LRU), 0)
    for ch in range(N_CHUNK):
        for lg in range(n_lg):
            ut_ref[lg, ch * PITCH:ch * PITCH + g_len, :] = u_ref[ch * g_len:(ch + 1) * g_len,
                                                                lg * LANES:(lg + 1) * LANES]
    u = [jnp.concatenate([ut_ref[lg, pl.ds(g, N_CHUNK, stride=PITCH), :] for lg in range(n_lg)], axis=1)
         for g in range(g_len)]

    def halo(row, blocked):
        return jnp.broadcast_to(jnp.where(blocked, 0.0, row), (N_CHUNK, W_LRU))

    um2 = jnp.where(sub == 0, halo(prev_ref[SUBLANES - 2:SUBLANES - 1, :], is_first_blk),
                    pltpu.roll(u[g_len - 2], 1, axis=0))
    um1 = jnp.where(sub == 0, halo(prev_ref[SUBLANES - 1:SUBLANES, :], is_first_blk),
                    pltpu.roll(u[g_len - 1], 1, axis=0))
    up1 = jnp.where(sub == N_CHUNK - 1, halo(next_ref[0:1, :], is_last_blk),
                    pltpu.roll(u[0], N_CHUNK - 1, axis=0))
    ext = [um2, um1] + u + [up1]
    uc = [ext[g] * cw[0] + ext[g + 1] * cw[1] + ext[g + 2] * cw[2] + ext[g + 3] * cw[3] + cb
          for g in range(g_len)]
    u2 = jnp.concatenate(uc, axis=0)
    ub = u2.astype(_bf16)
    half = W_LRU // 2
    gates = []
    for gi in range(2):
        cols = [_dot(ub[:, hf * half:(hf + 1) * half], wg_ref[gi, hf]) for hf in range(2)]
        gates.append(jnp.concatenate(cols, axis=1) + bg_ref[gi])
    t_r = jnp.tanh(gates[0])
    i_gate = 0.5 + 0.5 * jnp.tanh(gates[1])
    a = jnp.exp2(coef + coef * t_r)
    one_m_a2 = (1.0 - a) * (1.0 + a)
    mult = one_m_a2 * lax.rsqrt(jnp.maximum(one_m_a2, TINY))
    iu = i_gate * u2
    bx = mult * iu

    g_first = g_len - 1 if reverse else 0
    slab = slice(g_first * N_CHUNK, (g_first + 1) * N_CHUNK)
    at_first = jnp.logical_and(is_last_blk if reverse else is_first_blk,
                               sub == (N_CHUNK - 1 if reverse else 0))
    bx_first = jnp.where(at_first, iu[slab], bx[slab])

    h = jnp.zeros((N_CHUNK, W_LRU), _f32)
    p = jnp.ones((N_CHUNK, W_LRU), _f32)
    h_loc = [None] * g_len
    p_loc = [None] * g_len
    order = range(g_len - 1, -1, -1) if reverse else range(g_len)
    for g in order:
        ag = a[g * N_CHUNK:(g + 1) * N_CHUNK]
        h = ag * h + (bx_first if g == g_first else bx[g * N_CHUNK:(g + 1) * N_CHUNK])
        p = ag * p
        h_loc[g] = h
        p_loc[g] = p
    h_end, p_end = h, p

    if reverse:
        c = jnp.where(sub == N_CHUNK - 1, jnp.broadcast_to(carry_ref[0:1, :], (N_CHUNK, W_LRU)), 0.0)
        for kk in range(N_CHUNK - 2, -1, -1):
            c = jnp.where(sub == kk, pltpu.roll(p_end * c + h_end, N_CHUNK - 1, axis=0), c)
    else:
        c = jnp.where(sub == 0, jnp.broadcast_to(carry_ref[N_CHUNK - 1:N_CHUNK, :], (N_CHUNK, W_LRU)), 0.0)
        for kk in range(1, N_CHUNK):
            c = jnp.where(sub == kk, pltpu.roll(p_end * c + h_end, 1, axis=0), c)
    carry_ref[...] = p_end * c + h_end

    for g in range(g_len):
        hg = p_loc[g] * c + h_loc[g]
        for lg in range(n_lg):
            ot_ref[lg, pl.ds(g, N_CHUNK, stride=PITCH), :] = hg[:, lg * LANES:(lg + 1) * LANES]
    for ch in range(N_CHUNK):
        for lg in range(n_lg):
            o_ref[ch * g_len:(ch + 1) * g_len, lg * LANES:(lg + 1) * LANES] = (
                ot_ref[lg, ch * PITCH:ch * PITCH + g_len, :])


def _lru_kernel(uf_ref, pf_ref, nf_ref, ub_ref, pb_ref, nb_ref, cw_ref, cb_ref, wg_ref, bg_ref, lam_ref,
                of_ref, ob_ref, carry_f, carry_b, ut_f, ot_f, ut_b, ot_b):
    j = pl.program_id(1)
    n = pl.num_programs(1)

    @pl.when(j == 0)
    def _():
        carry_f[...] = jnp.zeros_like(carry_f)
        carry_b[...] = jnp.zeros_like(carry_b)

    cw = [cw_ref[i:i + 1, :] for i in range(CONV_W)]
    cb = cb_ref[...]
    neg_lam = -lam_ref[...]
    softplus = jnp.maximum(neg_lam, 0.0) + jnp.log(1.0 + jnp.exp(-jnp.abs(neg_lam)))
    coef = (-0.5 * C_LRU * LOG2_E) * softplus
    _lru_direction(uf_ref, pf_ref, nf_ref, j == 0, j == n - 1, cw, cb, wg_ref.at[0], bg_ref.at[0],
                   coef[0:1], carry_f, ut_f, ot_f, of_ref, reverse=False)
    _lru_direction(ub_ref, pb_ref, nb_ref, j == n - 1, j == 0, cw, cb, wg_ref.at[1], bg_ref.at[1],
                   coef[1:2], carry_b, ut_b, ot_b, ob_ref, reverse=True)


def _block_diag_halves(w):
    per_half = N_BLOCKS // 2
    eye = jnp.eye(per_half, dtype=w.dtype)
    w = w.reshape(w.shape[:-3] + (2, per_half, BLOCK_W, BLOCK_W))
    full = jnp.einsum('...hbde,bc->...hbdce', w, eye)
    return full.reshape(w.shape[:-3] + (per_half * BLOCK_W, per_half * BLOCK_W)).astype(_bf16)


def _lru(u, conv_w, conv_b, gate_w, gate_b, lam):
    b, s, _ = u.shape
    n = s // TS
    per = TS // SUBLANES
    n8 = s // SUBLANES
    wg = _block_diag_halves(0.5 * gate_w)
    bg = (0.5 * gate_b).reshape(2, 2, 1, W_LRU)
    cur = lambda f: pl.BlockSpec((None, TS, W_LRU), lambda bi, j: (bi, f(j), 0))
    prev = lambda f: pl.BlockSpec((None, SUBLANES, W_LRU),
                                  lambda bi, j: (bi, jnp.maximum(f(j) * per - 1, 0), 0))
    nxt = lambda f: pl.BlockSpec((None, SUBLANES, W_LRU),
                                 lambda bi, j: (bi, jnp.minimum((f(j) + 1) * per, n8 - 1), 0))
    fwd = lambda j: j
    bwd = lambda j: n - 1 - j
    full = lambda a: pl.BlockSpec(a.shape, lambda bi, j: (0,) * a.ndim)
    cb2 = conv_b.reshape(1, W_LRU)
    return pl.pallas_call(
        _lru_kernel,
        grid=(b, n),
        in_specs=[cur(fwd), prev(fwd), nxt(fwd), cur(bwd), prev(bwd), nxt(bwd),
                  full(conv_w), full(cb2), full(wg), full(bg), full(lam)],
        out_specs=[cur(fwd), cur(bwd)],
        out_shape=[jax.ShapeDtypeStruct((b, s, W_LRU), _f32)] * 2,
        scratch_shapes=[pltpu.VMEM((N_CHUNK, W_LRU), _f32)] * 2
        + [pltpu.VMEM((W_LRU // LANES, N_CHUNK * PITCH, LANES), _f32)] * 4,
        compiler_params=pltpu.CompilerParams(dimension_semantics=("arbitrary", "arbitrary"),
                                             vmem_limit_bytes=VMEM_LIMIT),
        name="lru",
    )(u, u, u, u, u, u, conv_w, cb2, wg, bg, lam)


def _out_proj_kernel(x_ref, ya_ref, hf_ref, hb_ref, zb_ref, ga_ref, gb_ref, wba_ref, wbb_ref,
                     wo_ref, g_ref, b_ref, o_ref, *, alpha):
    yb = ((hf_ref[...] + hb_ref[...]) * zb_ref[...].astype(_f32)).astype(_bf16)
    pa = _dot(ya_ref[...], wba_ref[...])
    pb = _dot(yb, wbb_ref[...])
    m = ga_ref[...].astype(_f32) * pa + gb_ref[...].astype(_f32) * pb
    out = _dot(m.astype(_bf16), wo_ref[...])
    o_ref[...] = _layer_norm(alpha * x_ref[...] + out, g_ref[...], b_ref[...])


def _out_proj(x2, y_a, h_f, h_b, z_b, g_a, g_b, w_ba, w_bb, w_out, ln_g, ln_b, alpha):
    t = x2.shape[0]
    row = lambda i: (i, 0)
    const = lambda i: (0, 0)
    rows = lambda n: pl.BlockSpec((TM, n), row)
    full = lambda a: pl.BlockSpec(a.shape, const)
    return pl.pallas_call(
        functools.partial(_out_proj_kernel, alpha=alpha),
        grid=(t // TM,),
        in_specs=[rows(D_MODEL), rows(W_ATTN), rows(W_LRU), rows(W_LRU), rows(W_LRU), rows(D_MODEL),
                  rows(D_MODEL), full(w_ba), full(w_bb), full(w_out), full(ln_g), full(ln_b)],
        out_specs=rows(D_MODEL),
        out_shape=jax.ShapeDtypeStruct((t, D_MODEL), _f32),
        compiler_params=pltpu.CompilerParams(dimension_semantics=("arbitrary",),
                                             vmem_limit_bytes=VMEM_LIMIT),
        name="out_proj",
    )(x2, y_a, h_f, h_b, z_b, g_a, g_b, w_ba, w_bb, w_out, ln_g, ln_b)


def kernel(x, emb_ln_g, emb_ln_b, w_in, rpb, conv_w, conv_b, lru_gate_w, lru_gate_b, lru_lambda,
           w_branch_attn, w_branch_lru, b_merge, w_out, ln_g, ln_b):
    b, s, d = x.shape
    depth = w_in.shape[0]
    assert d == D_MODEL and s % (Q_ROWS * GRID_W) == 0 and s % TS == 0 and (b * s) % TM == 0
    alpha = (2 * depth) ** 0.25
    t = b * s
    h = x.reshape(t, d)
    row2 = lambda a: a.reshape(1, -1)
    for l in range(depth):
        outs = _in_proj(h, row2(emb_ln_g), row2(emb_ln_b), w_in[l].astype(_bf16), b_merge[l],
                        apply_ln=(l == 0))
        if l == 0:
            h, outs = outs[0], outs[1:]
        q, k, v, z_a, u, z_b, g_a, g_b = outs
        seq = lambda a: a.reshape(b, s, a.shape[-1])
        y_a = _attention(seq(q), seq(k), seq(v), seq(z_a), _attn_value_table(rpb[l]),
                         _attn_row_mask(s // GRID_W))
        h_f, h_b = _lru(seq(u), conv_w[l], conv_b[l], lru_gate_w[l], lru_gate_b[l], lru_lambda[l])
        flat = lambda a: a.reshape(t, a.shape[-1])
        h = _out_proj(h, flat(y_a), flat(h_f), flat(h_b), z_b, g_a, g_b,
                      w_branch_attn[l].astype(_bf16), w_branch_lru[l].astype(_bf16),
                      w_out[l].astype(_bf16), row2(ln_g[l]), row2(ln_b[l]), alpha)
    return h.reshape(b, s, d)
```

```python
import functools

import numpy as np
import jax
import jax.numpy as jnp
from jax import lax
from jax.experimental import pallas as pl
from jax.experimental.pallas import tpu as pltpu

D_MODEL = 1024
GRID_W = 64
N_HEADS = 8
HEAD_DIM = 64
W_ATTN = N_HEADS * HEAD_DIM
WIN_H = 8
WIN_W = 16
W_LRU = 512
N_BLOCKS = 8
BLOCK_W = 64
CONV_W = 4
C_LRU = 8.0
LN_EPS = 1e-5
NEG = -1e30
TINY = 1e-30
LOG2_E = 1.4426950408889634

LANES = 128
SUBLANES = 8
VMEM_LIMIT = 56 * 1024 * 1024

TM = 512
Q_ROWS = 8
SUB_ROWS = 4
KEY_ROWS = SUB_ROWS + WIN_H
KV_BLK_ROWS = 4
N_CHUNK = SUBLANES
G_LEN = TM // N_CHUNK
PITCH = G_LEN + 8
N_LG = W_LRU // LANES
PROJ_COLS = 256

_f32 = jnp.float32
_bf16 = jnp.bfloat16


def _dot(a, b):
    return jnp.dot(a, b, preferred_element_type=_f32)


def _layer_norm(x, g, b):
    mu = jnp.mean(x, axis=-1, keepdims=True)
    xc = x - mu
    var = jnp.mean(xc * xc, axis=-1, keepdims=True)
    return xc * lax.rsqrt(var + LN_EPS) * g + b


def _sigmoid(x):
    return 0.5 + 0.5 * jnp.tanh(0.5 * x)


def _silu(x):
    hx = 0.5 * x
    return hx + hx * jnp.tanh(hx)


def _stage_time_major(u, ut_ref):
    for ch in range(N_CHUNK):
        for lg in range(N_LG):
            ut_ref[lg, ch * PITCH:ch * PITCH + G_LEN, :] = u[ch * G_LEN:(ch + 1) * G_LEN,
                                                            lg * LANES:(lg + 1) * LANES]


def _load_slab(ut_ref, g):
    return jnp.concatenate([ut_ref[lg, pl.ds(g, N_CHUNK, stride=PITCH), :] for lg in range(N_LG)], axis=1)


def _slab_rows(g):
    return slice(g * N_CHUNK, (g + 1) * N_CHUNK)


def _unstage_slabs(slab_fn, ot_ref, g_range):
    for g in g_range:
        slab = slab_fn(g)
        for lg in range(N_LG):
            ot_ref[lg, pl.ds(g, N_CHUNK, stride=PITCH), :] = slab[:, lg * LANES:(lg + 1) * LANES]


def _copy_time_major(ot_ref, o_ref, ch_range):
    for ch in ch_range:
        for lg in range(N_LG):
            o_ref[ch * G_LEN:(ch + 1) * G_LEN, lg * LANES:(lg + 1) * LANES] = (
                ot_ref[lg, ch * PITCH:ch * PITCH + G_LEN, :].astype(o_ref.dtype))


def _gate_dots(ub_ref, wg_ref):
    half = W_LRU // 2
    pre = []
    for gi in range(2):
        cols = [_dot(ub_ref[:, hf * half:(hf + 1) * half], wg_ref[gi, hf]) for hf in range(2)]
        pre.append(jnp.concatenate(cols, axis=1))
    return pre


def _gates(pre, u2_ref, bg_ref, coef, first, g_first, a_ref, bx_ref):
    t_r = jnp.tanh(pre[0] + bg_ref[0])
    i_gate = 0.5 + 0.5 * jnp.tanh(pre[1] + bg_ref[1])
    a = jnp.exp2(coef + coef * t_r)
    one_m_a2 = (1.0 - a) * (1.0 + a)
    mult = one_m_a2 * lax.rsqrt(jnp.maximum(one_m_a2, TINY))
    iu = i_gate * u2_ref[...]
    bx = mult * iu
    a_ref[...] = a
    bx_ref[...] = bx
    fix = _slab_rows(g_first)
    bx_ref[fix, :] = jnp.where(first, iu[fix], bx[fix])


def _chunk_scan(a_ref, bx_ref, hl_ref, pl_ref, g_range):
    h = jnp.zeros((N_CHUNK, W_LRU), _f32)
    p = jnp.ones((N_CHUNK, W_LRU), _f32)
    for g in g_range:
        rows = _slab_rows(g)
        ag = a_ref[rows, :]
        h = ag * h + bx_ref[rows, :]
        p = ag * p
        hl_ref[rows, :] = h
        pl_ref[rows, :] = p
    return h, p


def _front_kernel(x_ref, xn_ref, g_ref, b_ref, w_ref, bm_ref, cw_ref, cb_ref, wg_ref, bg_ref, lam_ref,
                  *refs, apply_ln, blocks_per_seq):
    n_out = 11 if apply_ln else 10
    out_refs = refs[:n_out]
    (uprev_ref, carry_ref, xb_ref, u2_ref, ub_ref, a_ref, bx_ref, hl_ref, pl_ref,
     ut_ref, ot_f, ot_h, ot_p, pre_ref) = refs[n_out:]
    if apply_ln:
        h_ref, out_refs = out_refs[0], out_refs[1:]
    q_ref, k_ref, v_ref, za_ref, zb_ref, ga_ref, gb_ref, hf_ref, hb0_ref, pb_ref = out_refs
    j = pl.program_id(0)
    is_first_blk = j % blocks_per_seq == 0
    is_last_blk = j % blocks_per_seq == blocks_per_seq - 1
    u_off = 4 * W_ATTN
    g_off = u_off + 2 * W_LRU
    sub = lax.broadcasted_iota(jnp.int32, (N_CHUNK, W_LRU), 0)

    def edge(row):
        return jnp.broadcast_to(row, (N_CHUNK, W_LRU))

    def proj(o_ref, off, post):
        for c0 in range(0, o_ref.shape[-1], PROJ_COLS):
            y = _dot(xb_ref[...], w_ref[:, off + c0:off + c0 + PROJ_COLS])
            o_ref[:, c0:c0 + PROJ_COLS] = post(y, slice(c0, c0 + PROJ_COLS)).astype(o_ref.dtype)

    @pl.when(is_first_blk)
    def _():
        uprev_ref[...] = jnp.zeros_like(uprev_ref)
        carry_ref[...] = jnp.zeros_like(carry_ref)

    x = x_ref[...]
    xn = xn_ref[...]
    if apply_ln:
        x = _layer_norm(x, g_ref[...], b_ref[...])
        xn = _layer_norm(xn, g_ref[...], b_ref[...])
        h_ref[...] = x
    xb_ref[...] = x.astype(_bf16)
    u = _dot(xb_ref[...], w_ref[:, u_off:u_off + W_LRU])
    u_next = _dot(xn.astype(_bf16), w_ref[:, u_off:u_off + W_LRU])
    _stage_time_major(u, ut_ref)

    slabs = {}
    slabs[-2] = jnp.where(sub == 0, edge(uprev_ref[SUBLANES - 2:SUBLANES - 1, :]),
                          pltpu.roll(_load_slab(ut_ref, G_LEN - 2), 1, axis=0))
    slabs[-1] = jnp.where(sub == 0, edge(uprev_ref[SUBLANES - 1:SUBLANES, :]),
                          pltpu.roll(_load_slab(ut_ref, G_LEN - 1), 1, axis=0))
    nxt = jnp.where(is_last_blk, 0.0, u_next[0:1, :])
    slabs[G_LEN] = jnp.where(sub == N_CHUNK - 1, edge(nxt), pltpu.roll(_load_slab(ut_ref, 0), N_CHUNK - 1, axis=0))
    uprev_ref[...] = u[TM - SUBLANES:TM, :]

    def slab(g):
        if g not in slabs:
            slabs[g] = _load_slab(ut_ref, g)
        return slabs[g]

    cw = [cw_ref[i:i + 1, :] for i in range(CONV_W)]
    cb = cb_ref[...]
    for g in range(G_LEN):
        u2_ref[_slab_rows(g), :] = (slab(g - 2) * cw[0] + slab(g - 1) * cw[1] + slab(g) * cw[2]
                                    + slab(g + 1) * cw[3] + cb)
    ub_ref[...] = u2_ref[...].astype(_bf16)
    proj(q_ref, 0, lambda y, cols: y * (HEAD_DIM ** -0.5 * LOG2_E))
    proj(k_ref, W_ATTN, lambda y, cols: y)

    for di in range(2):
        pre = _gate_dots(ub_ref, wg_ref.at[di])
        pre_ref[2 * di] = pre[0]
        pre_ref[2 * di + 1] = pre[1]

    neg_lam = -lam_ref[...]
    softplus = jnp.maximum(neg_lam, 0.0) + jnp.log(1.0 + jnp.exp(-jnp.abs(neg_lam)))
    coef = (-0.5 * C_LRU * LOG2_E) * softplus

    first = jnp.logical_and(is_first_blk, sub == 0)
    _gates((pre_ref[0], pre_ref[1]), u2_ref, bg_ref.at[0], coef[0:1], first, 0, a_ref, bx_ref)
    h_end, p_end = _chunk_scan(a_ref, bx_ref, hl_ref, pl_ref, range(G_LEN))
    c = jnp.where(sub == 0, edge(carry_ref[N_CHUNK - 1:N_CHUNK, :]), 0.0)
    for kk in range(1, N_CHUNK):
        c = jnp.where(sub == kk, pltpu.roll(p_end * c + h_end, 1, axis=0), c)
    carry_ref[...] = p_end * c + h_end
    _unstage_slabs(lambda g: pl_ref[_slab_rows(g), :] * c + hl_ref[_slab_rows(g), :], ot_f, range(G_LEN))
    _copy_time_major(ot_f, hf_ref, range(N_CHUNK))
    proj(v_ref, 2 * W_ATTN, lambda y, cols: y)
    proj(za_ref, 3 * W_ATTN, lambda y, cols: _silu(y))
    proj(zb_ref, u_off + W_LRU, lambda y, cols: _silu(y))

    last = jnp.logical_and(is_last_blk, sub == N_CHUNK - 1)
    _gates((pre_ref[2], pre_ref[3]), u2_ref, bg_ref.at[1], coef[1:2], last, G_LEN - 1, a_ref, bx_ref)
    h_end, p_end = _chunk_scan(a_ref, bx_ref, hl_ref, pl_ref, range(G_LEN - 1, -1, -1))
    cz = jnp.zeros((N_CHUNK, W_LRU), _f32)
    d = jnp.where(sub == N_CHUNK - 1, 1.0, 0.0)
    for kk in range(N_CHUNK - 2, -1, -1):
        cz = jnp.where(sub == kk, pltpu.roll(p_end * cz + h_end, N_CHUNK - 1, axis=0), cz)
        d = jnp.where(sub == kk, pltpu.roll(p_end * d, N_CHUNK - 1, axis=0), d)
    _unstage_slabs(lambda g: pl_ref[_slab_rows(g), :] * cz + hl_ref[_slab_rows(g), :], ot_h, range(G_LEN))
    _unstage_slabs(lambda g: pl_ref[_slab_rows(g), :] * d, ot_p, range(G_LEN))
    _copy_time_major(ot_h, hb0_ref, range(N_CHUNK))
    _copy_time_major(ot_p, pb_ref, range(N_CHUNK))
    proj(ga_ref, g_off, lambda y, cols: _sigmoid(y + bm_ref[0:1, cols]))
    proj(gb_ref, g_off + D_MODEL, lambda y, cols: _sigmoid(y + bm_ref[1:2, cols]))


def _block_diag_halves(w):
    per_half = N_BLOCKS // 2
    eye = jnp.eye(per_half, dtype=w.dtype)
    w = w.reshape(w.shape[:-3] + (2, per_half, BLOCK_W, BLOCK_W))
    full = jnp.einsum('...hbde,bc->...hbdce', w, eye)
    return full.reshape(w.shape[:-3] + (per_half * BLOCK_W, per_half * BLOCK_W)).astype(_bf16)


def _front(x2, seq_len, ln_g, ln_b, w_in_bf16, b_merge, conv_w, conv_b, gate_w, gate_b, lam, apply_ln):
    t = x2.shape[0]
    n = t // TM
    per = TM // SUBLANES
    row = lambda i: (i, 0)
    full = lambda a: pl.BlockSpec(a.shape, lambda i: (0,) * a.ndim)
    wg = _block_diag_halves(0.5 * gate_w)
    bg = (0.5 * gate_b).reshape(2, 2, 1, W_LRU)
    cb2 = conv_b.reshape(1, W_LRU)
    widths = (W_ATTN, W_ATTN, W_ATTN, W_ATTN, W_LRU, D_MODEL, D_MODEL, W_LRU, W_LRU, W_LRU)
    out_shape = [jax.ShapeDtypeStruct((t, w), _bf16) for w in widths]
    out_specs = [pl.BlockSpec((TM, w), row) for w in widths]
    if apply_ln:
        out_shape = [jax.ShapeDtypeStruct((t, D_MODEL), _f32)] + out_shape
        out_specs = [pl.BlockSpec((TM, D_MODEL), row)] + out_specs
    stage = pltpu.VMEM((N_LG, N_CHUNK * PITCH, LANES), _f32)
    block_f32 = pltpu.VMEM((TM, W_LRU), _f32)
    return pl.pallas_call(
        functools.partial(_front_kernel, apply_ln=apply_ln, blocks_per_seq=seq_len // TM),
        grid=(n,),
        in_specs=[pl.BlockSpec((TM, D_MODEL), row),
                  pl.BlockSpec((SUBLANES, D_MODEL), lambda i: (jnp.minimum((i + 1) * per, n * per - 1), 0)),
                  full(ln_g), full(ln_b), full(w_in_bf16), full(b_merge), full(conv_w), full(cb2),
                  full(wg), full(bg), full(lam)],
        out_specs=out_specs,
        out_shape=out_shape,
        scratch_shapes=[pltpu.VMEM((SUBLANES, W_LRU), _f32), pltpu.VMEM((N_CHUNK, W_LRU), _f32),
                        pltpu.VMEM((TM, D_MODEL), _bf16), block_f32, pltpu.VMEM((TM, W_LRU), _bf16),
                        block_f32, block_f32, block_f32, block_f32, stage, stage, stage, stage,
                        pltpu.VMEM((4, TM, W_LRU), _f32)],
        compiler_params=pltpu.CompilerParams(dimension_semantics=("arbitrary",),
                                             vmem_limit_bytes=VMEM_LIMIT),
        name="front",
    )(x2, x2, ln_g, ln_b, w_in_bf16, b_merge, conv_w, cb2, wg, bg, lam)


def _attn_value_table(rpb):
    qc = np.arange(GRID_W)[:, None]
    kc = np.arange(GRID_W)[None, :]
    cs = np.clip(qc - WIN_W // 2, 0, GRID_W - WIN_W)
    ok = (kc >= cs) & (kc < cs + WIN_W)
    pad = GRID_W - WIN_W
    padded = jnp.pad(rpb.astype(_f32), ((0, 0), (0, 0), (pad, pad)))
    t = jnp.stack([padded[:, :, GRID_W - 1 - c:2 * GRID_W - 1 - c] for c in range(GRID_W)], axis=2)
    t = jnp.where(ok, t * LOG2_E, NEG)
    return jnp.concatenate([t[:, :-1], t[:, 1:]], axis=-1)


def _attn_row_mask(rows):
    r_first = (0, Q_ROWS, rows - Q_ROWS)
    m = np.full((3, Q_ROWS // SUB_ROWS, SUB_ROWS, SUBLANES, KEY_ROWS * GRID_W), NEG, np.float32)
    for v, r0 in enumerate(r_first):
        for half in range(Q_ROWS // SUB_ROWS):
            for qr in range(SUB_ROWS):
                r = r0 + half * SUB_ROWS + qr
                rs = min(max(r - WIN_H // 2, 0), rows - WIN_H)
                for kr in range(KEY_ROWS):
                    key_r = r0 + half * SUB_ROWS - WIN_H // 2 + kr
                    if rs <= key_r < rs + WIN_H:
                        m[v, half, qr, :, kr * GRID_W:(kr + 1) * GRID_W] = 0.0
    return m


def _attn_kernel(q_ref, k0_ref, k1_ref, k2_ref, k3_ref, v0_ref, v1_ref, v2_ref, v3_ref,
                 z_ref, val_ref, rowm_ref, o_ref):
    k_refs = (k0_ref, k1_ref, k2_ref, k3_ref)
    v_refs = (v0_ref, v1_ref, v2_ref, v3_ref)
    m_q = SUB_ROWS * GRID_W
    n_kb = KEY_ROWS // KV_BLK_ROWS
    lane = lax.broadcasted_iota(jnp.int32, (m_q, LANES), 1)
    even = lane < HEAD_DIM
    for half in range(Q_ROWS // SUB_ROWS):
        rows = slice(half * m_q, (half + 1) * m_q)
        for pair in range(N_HEADS // 2):
            cols = slice(pair * LANES, (pair + 1) * LANES)
            kt = jnp.concatenate([k_refs[half + j][:, cols] for j in range(n_kb)], axis=0)
            vt = jnp.concatenate([v_refs[half + j][:, cols] for j in range(n_kb)], axis=0)
            qp = q_ref[rows, cols]
            zero = jnp.zeros_like(qp)
            q2 = jnp.concatenate([jnp.where(even, qp, zero), jnp.where(even, zero, qp)], axis=0)
            s = lax.dot_general(q2, kt, (((1,), (1,)), ((), ())), preferred_element_type=_f32)
            blocks = []
            for parity in range(2):
                for qr in range(SUB_ROWS):
                    val = jnp.concatenate(
                        [val_ref[2 * pair + parity, 2 * j - qr + WIN_H // 2 - 1]
                         for j in range(KEY_ROWS // 2)], axis=1)
                    r0 = (parity * SUB_ROWS + qr) * GRID_W
                    blk = s[r0:r0 + GRID_W] + val
                    blk = blk.reshape(GRID_W // SUBLANES, SUBLANES, -1) + rowm_ref[half, qr][None]
                    blocks.append(blk.reshape(GRID_W, -1))
            s = jnp.concatenate(blocks, axis=0)
            m = jnp.max(s, axis=-1, keepdims=True)
            e = jnp.exp2(s - m)
            l = jnp.sum(e, axis=-1, keepdims=True)
            o2 = _dot(e.astype(_bf16), vt)
            o = jnp.where(even, o2[:m_q] / l[:m_q], o2[m_q:] / l[m_q:])
            o_ref[rows, cols] = (o * z_ref[rows, cols].astype(_f32)).astype(o_ref.dtype)


def _attention(q, k, v, z_a, val, rowm):
    b, s, _ = q.shape
    rows = s // GRID_W
    n_blk = rows // Q_ROWS
    n_kv = rows // KV_BLK_ROWS
    tq = Q_ROWS * GRID_W
    tkv = KV_BLK_ROWS * GRID_W

    def kv_spec(j):
        first = -(WIN_H // 2) // KV_BLK_ROWS
        return pl.BlockSpec(
            (None, tkv, W_ATTN),
            lambda bi, i: (bi, jnp.clip(i * (Q_ROWS // KV_BLK_ROWS) + first + j, 0, n_kv - 1), 0))

    def case(i):
        return jnp.where(i == 0, 0, jnp.where(i == n_blk - 1, 2, 1))

    blk = pl.BlockSpec((None, tq, W_ATTN), lambda bi, i: (bi, i, 0))
    return pl.pallas_call(
        _attn_kernel,
        grid=(b, n_blk),
        in_specs=[blk] + [kv_spec(j) for j in range(4)] + [kv_spec(j) for j in range(4)] + [
            blk,
            pl.BlockSpec(val.shape, lambda bi, i: (0, 0, 0, 0)),
            pl.BlockSpec((None,) + rowm.shape[1:], lambda bi, i: (case(i), 0, 0, 0, 0))],
        out_specs=blk,
        out_shape=jax.ShapeDtypeStruct((b, s, W_ATTN), _bf16),
        compiler_params=pltpu.CompilerParams(dimension_semantics=("arbitrary", "arbitrary"),
                                             vmem_limit_bytes=VMEM_LIMIT),
        name="attn",
    )(q, k, k, k, k, v, v, v, v, z_a, val, rowm)


def _back_kernel(x_ref, ya_ref, hf_ref, hb0_ref, pb_ref, zb_ref, ga_ref, gb_ref, wba_ref, wbb_ref,
                 wo_ref, g_ref, b_ref, o_ref, carry_ref, *, alpha, blocks_per_seq):
    i = pl.program_id(0)

    @pl.when(i % blocks_per_seq == 0)
    def _():
        carry_ref[...] = jnp.zeros_like(carry_ref)

    carry = carry_ref[0:1, :]
    h_b = hb0_ref[...].astype(_f32) + pb_ref[...].astype(_f32) * carry
    carry_ref[...] = h_b[0:SUBLANES, :]
    yb = ((hf_ref[...].astype(_f32) + h_b) * zb_ref[...].astype(_f32)).astype(_bf16)
    pa = _dot(ya_ref[...], wba_ref[...])
    pb = _dot(yb, wbb_ref[...])
    m = ga_ref[...].astype(_f32) * pa + gb_ref[...].astype(_f32) * pb
    out = _dot(m.astype(_bf16), wo_ref[...])
    o_ref[...] = _layer_norm(alpha * x_ref[...] + out, g_ref[...], b_ref[...])


def _back(x2, seq_len, y_a, h_f, h_b0, p_b, z_b, g_a, g_b, w_ba, w_bb, w_out, ln_g, ln_b, alpha):
    t = x2.shape[0]
    n = t // TM
    rev = lambda i: (n - 1 - i, 0)
    const = lambda i: (0, 0)
    rows = lambda w: pl.BlockSpec((TM, w), rev)
    full = lambda a: pl.BlockSpec(a.shape, const)
    return pl.pallas_call(
        functools.partial(_back_kernel, alpha=alpha, blocks_per_seq=seq_len // TM),
        grid=(n,),
        in_specs=[rows(D_MODEL), rows(W_ATTN), rows(W_LRU), rows(W_LRU), rows(W_LRU), rows(W_LRU),
                  rows(D_MODEL), rows(D_MODEL), full(w_ba), full(w_bb), full(w_out), full(ln_g), full(ln_b)],
        out_specs=rows(D_MODEL),
        out_shape=jax.ShapeDtypeStruct((t, D_MODEL), _f32),
        scratch_shapes=[pltpu.VMEM((SUBLANES, W_LRU), _f32)],
        compiler_params=pltpu.CompilerParams(dimension_semantics=("arbitrary",),
                                             vmem_limit_bytes=VMEM_LIMIT),
        name="back",
    )(x2, y_a, h_f, h_b0, p_b, z_b, g_a, g_b, w_ba, w_bb, w_out, ln_g, ln_b)


def kernel(x, emb_ln_g, emb_ln_b, w_in, rpb, conv_w, conv_b, lru_gate_w, lru_gate_b, lru_lambda,
           w_branch_attn, w_branch_lru, b_merge, w_out, ln_g, ln_b):
    b, s, d = x.shape
    depth = w_in.shape[0]
    assert d == D_MODEL and s % (Q_ROWS * GRID_W) == 0 and s % TM == 0
    alpha = (2 * depth) ** 0.25
    t = b * s
    h = x.reshape(t, d)
    row2 = lambda a: a.reshape(1, -1)
    rowm = _attn_row_mask(s // GRID_W)
    for l in range(depth):
        outs = _front(h, s, row2(emb_ln_g), row2(emb_ln_b), w_in[l].astype(_bf16), b_merge[l], conv_w[l],
                      conv_b[l], lru_gate_w[l], lru_gate_b[l], lru_lambda[l], apply_ln=(l == 0))
        if l == 0:
            h, outs = outs[0], outs[1:]
        q, k, v, z_a, z_b, g_a, g_b, h_f, h_b0, p_b = outs
        seq = lambda a: a.reshape(b, s, a.shape[-1])
        y_a = _attention(seq(q), seq(k), seq(v), seq(z_a), _attn_value_table(rpb[l]), rowm)
        h = _back(h, s, y_a.reshape(t, W_ATTN), h_f, h_b0, p_b, z_b, g_a, g_b,
                  w_branch_attn[l].astype(_bf16), w_branch_lru[l].astype(_bf16),
                  w_out[l].astype(_bf16), row2(ln_g[l]), row2(ln_b[l]), alpha)
    return h.reshape(b, s, d)
```

```python
import functools

import numpy as np
import jax
import jax.numpy as jnp
from jax import lax
from jax.experimental import pallas as pl
from jax.experimental.pallas import tpu as pltpu

D_MODEL = 1024
GRID_W = 64
N_HEADS = 8
HEAD_DIM = 64
W_ATTN = N_HEADS * HEAD_DIM
WIN_H = 8
WIN_W = 16
W_LRU = 512
N_BLOCKS = 8
BLOCK_W = 64
CONV_W = 4
C_LRU = 8.0
LN_EPS = 1e-5
NEG = -1e30
TINY = 1e-30
LOG2_E = 1.4426950408889634

LANES = 128
SUBLANES = 8
VMEM_LIMIT = 56 * 1024 * 1024

TM = 512
TB = 1024
Q_ROWS = 16
SUB_ROWS = 4
KEY_ROWS = SUB_ROWS + WIN_H
KV_BLK_ROWS = 4
N_KV_BLKS = (Q_ROWS + WIN_H) // KV_BLK_ROWS
N_CHUNK = SUBLANES
G_LEN = TM // N_CHUNK
PITCH = G_LEN + 8
N_LG = W_LRU // LANES
PROJ_COLS = 256

_f32 = jnp.float32
_bf16 = jnp.bfloat16


def _dot(a, b):
    return jnp.dot(a, b, preferred_element_type=_f32)


def _layer_norm(x, g, b):
    mu = jnp.mean(x, axis=-1, keepdims=True)
    xc = x - mu
    var = jnp.mean(xc * xc, axis=-1, keepdims=True)
    return xc * lax.rsqrt(var + LN_EPS) * g + b


def _sigmoid(x):
    return 0.5 + 0.5 * jnp.tanh(0.5 * x)


def _silu(x):
    hx = 0.5 * x
    return hx + hx * jnp.tanh(hx)


def _stage_time_major(u, ut_ref):
    for ch in range(N_CHUNK):
        for lg in range(N_LG):
            ut_ref[lg, ch * PITCH:ch * PITCH + G_LEN, :] = u[ch * G_LEN:(ch + 1) * G_LEN,
                                                            lg * LANES:(lg + 1) * LANES]


def _load_slab(ut_ref, g):
    return jnp.concatenate([ut_ref[lg, pl.ds(g, N_CHUNK, stride=PITCH), :] for lg in range(N_LG)], axis=1)


def _slab_rows(g):
    return slice(g * N_CHUNK, (g + 1) * N_CHUNK)


def _unstage_slabs(slab_fn, ot_ref, g_range):
    for g in g_range:
        slab = slab_fn(g)
        for lg in range(N_LG):
            ot_ref[lg, pl.ds(g, N_CHUNK, stride=PITCH), :] = slab[:, lg * LANES:(lg + 1) * LANES]


def _copy_time_major(ot_ref, o_ref, ch_range):
    for ch in ch_range:
        for lg in range(N_LG):
            o_ref[ch * G_LEN:(ch + 1) * G_LEN, lg * LANES:(lg + 1) * LANES] = (
                ot_ref[lg, ch * PITCH:ch * PITCH + G_LEN, :].astype(o_ref.dtype))


def _gate_dots(ub_ref, wg_ref):
    half = W_LRU // 2
    pre = []
    for gi in range(2):
        cols = [_dot(ub_ref[:, hf * half:(hf + 1) * half], wg_ref[gi, hf]) for hf in range(2)]
        pre.append(jnp.concatenate(cols, axis=1))
    return pre


def _gates(pre, hu_ref, bg_ref, coef, first, g_first, a_ref, bx_ref):
    t_r = jnp.tanh(pre[0] + bg_ref[0])
    t_i = jnp.tanh(pre[1] + bg_ref[1])
    a = jnp.exp2(coef + coef * t_r)
    one_m_a2 = (1.0 - a) * (1.0 + a)
    mult = one_m_a2 * lax.rsqrt(jnp.maximum(one_m_a2, TINY))
    hu = hu_ref[...]
    iu = hu + hu * t_i
    bx = mult * iu
    a_ref[...] = a
    bx_ref[...] = bx
    fix = _slab_rows(g_first)
    bx_ref[fix, :] = jnp.where(first, iu[fix], bx[fix])


def _chunk_scan(a_ref, bx_ref, hl_ref, pl_ref, g_range):
    h = jnp.zeros((N_CHUNK, W_LRU), _f32)
    p = jnp.ones((N_CHUNK, W_LRU), _f32)
    for g in g_range:
        rows = _slab_rows(g)
        ag = a_ref[rows, :]
        h = ag * h + bx_ref[rows, :]
        p = ag * p
        hl_ref[rows, :] = h
        pl_ref[rows, :] = p
    return h, p


def _front_kernel(x_ref, xn_ref, g_ref, b_ref, w_ref, bm_ref, cw_ref, cb_ref, wg_ref, bg_ref, lam_ref,
                  *refs, apply_ln, blocks_per_seq):
    n_out = 11 if apply_ln else 10
    out_refs = refs[:n_out]
    (uprev_ref, carry_ref, xb_ref, hu_ref, ub_ref, a_ref, bx_ref, hl_ref, pl_ref,
     ut_ref, ot_f, ot_h, ot_p, pre_ref) = refs[n_out:]
    if apply_ln:
        h_ref, out_refs = out_refs[0], out_refs[1:]
    q_ref, k_ref, v_ref, za_ref, zb_ref, ga_ref, gb_ref, hf_ref, hb0_ref, pb_ref = out_refs
    j = pl.program_id(0)
    is_first_blk = j % blocks_per_seq == 0
    is_last_blk = j % blocks_per_seq == blocks_per_seq - 1
    u_off = 4 * W_ATTN
    g_off = u_off + 2 * W_LRU
    sub = lax.broadcasted_iota(jnp.int32, (N_CHUNK, W_LRU), 0)

    def edge(row):
        return jnp.broadcast_to(row, (N_CHUNK, W_LRU))

    def proj(o_ref, off, post):
        for c0 in range(0, o_ref.shape[-1], PROJ_COLS):
            y = _dot(xb_ref[...], w_ref[:, off + c0:off + c0 + PROJ_COLS])
            o_ref[:, c0:c0 + PROJ_COLS] = post(y, slice(c0, c0 + PROJ_COLS)).astype(o_ref.dtype)

    @pl.when(is_first_blk)
    def _():
        uprev_ref[...] = jnp.zeros_like(uprev_ref)
        carry_ref[...] = jnp.zeros_like(carry_ref)

    x = x_ref[...]
    xn = xn_ref[...]
    if apply_ln:
        x = _layer_norm(x, g_ref[...], b_ref[...])
        xn = _layer_norm(xn, g_ref[...], b_ref[...])
        h_ref[...] = x
    xb_ref[...] = x.astype(_bf16)
    u = _dot(xb_ref[...], w_ref[:, u_off:u_off + W_LRU])
    u_next = _dot(xn.astype(_bf16), w_ref[:, u_off:u_off + W_LRU])
    _stage_time_major(u, ut_ref)

    slabs = {}
    slabs[-2] = jnp.where(sub == 0, edge(uprev_ref[SUBLANES - 2:SUBLANES - 1, :]),
                          pltpu.roll(_load_slab(ut_ref, G_LEN - 2), 1, axis=0))
    slabs[-1] = jnp.where(sub == 0, edge(uprev_ref[SUBLANES - 1:SUBLANES, :]),
                          pltpu.roll(_load_slab(ut_ref, G_LEN - 1), 1, axis=0))
    nxt = jnp.where(is_last_blk, 0.0, u_next[0:1, :])
    slabs[G_LEN] = jnp.where(sub == N_CHUNK - 1, edge(nxt), pltpu.roll(_load_slab(ut_ref, 0), N_CHUNK - 1, axis=0))
    uprev_ref[...] = u[TM - SUBLANES:TM, :]

    def slab(g):
        if g not in slabs:
            slabs[g] = _load_slab(ut_ref, g)
        return slabs[g]

    cw = [cw_ref[i:i + 1, :] for i in range(CONV_W)]
    cb = cb_ref[...]
    for g in range(G_LEN):
        hu_ref[_slab_rows(g), :] = (slab(g - 2) * cw[0] + slab(g - 1) * cw[1] + slab(g) * cw[2]
                                    + slab(g + 1) * cw[3] + cb)
    ub_ref[...] = hu_ref[...].astype(_bf16)
    proj(q_ref, 0, lambda y, cols: y * (HEAD_DIM ** -0.5 * LOG2_E))
    proj(k_ref, W_ATTN, lambda y, cols: y)

    for di in range(2):
        pre = _gate_dots(ub_ref, wg_ref.at[di])
        pre_ref[2 * di] = pre[0]
        pre_ref[2 * di + 1] = pre[1]

    neg_lam = -lam_ref[...]
    softplus = jnp.maximum(neg_lam, 0.0) + jnp.log(1.0 + jnp.exp(-jnp.abs(neg_lam)))
    coef = (-0.5 * C_LRU * LOG2_E) * softplus

    first = jnp.logical_and(is_first_blk, sub == 0)
    _gates((pre_ref[0], pre_ref[1]), hu_ref, bg_ref.at[0], coef[0:1], first, 0, a_ref, bx_ref)
    h_end, p_end = _chunk_scan(a_ref, bx_ref, hl_ref, pl_ref, range(G_LEN))
    c = jnp.where(sub == 0, edge(carry_ref[N_CHUNK - 1:N_CHUNK, :]), 0.0)
    for kk in range(1, N_CHUNK):
        c = jnp.where(sub == kk, pltpu.roll(p_end * c + h_end, 1, axis=0), c)
    carry_ref[...] = p_end * c + h_end
    _unstage_slabs(lambda g: pl_ref[_slab_rows(g), :] * c + hl_ref[_slab_rows(g), :], ot_f, range(G_LEN))
    _copy_time_major(ot_f, hf_ref, range(N_CHUNK))
    proj(v_ref, 2 * W_ATTN, lambda y, cols: y)
    proj(za_ref, 3 * W_ATTN, lambda y, cols: _silu(y))
    proj(zb_ref, u_off + W_LRU, lambda y, cols: _silu(y))

    last = jnp.logical_and(is_last_blk, sub == N_CHUNK - 1)
    _gates((pre_ref[2], pre_ref[3]), hu_ref, bg_ref.at[1], coef[1:2], last, G_LEN - 1, a_ref, bx_ref)
    h_end, p_end = _chunk_scan(a_ref, bx_ref, hl_ref, pl_ref, range(G_LEN - 1, -1, -1))
    cz = jnp.zeros((N_CHUNK, W_LRU), _f32)
    d = jnp.where(sub == N_CHUNK - 1, 1.0, 0.0)
    for kk in range(N_CHUNK - 2, -1, -1):
        cz = jnp.where(sub == kk, pltpu.roll(p_end * cz + h_end, N_CHUNK - 1, axis=0), cz)
        d = jnp.where(sub == kk, pltpu.roll(p_end * d, N_CHUNK - 1, axis=0), d)
    _unstage_slabs(lambda g: pl_ref[_slab_rows(g), :] * cz + hl_ref[_slab_rows(g), :], ot_h, range(G_LEN))
    _unstage_slabs(lambda g: pl_ref[_slab_rows(g), :] * d, ot_p, range(G_LEN))
    _copy_time_major(ot_h, hb0_ref, range(N_CHUNK))
    _copy_time_major(ot_p, pb_ref, range(N_CHUNK))
    proj(ga_ref, g_off, lambda y, cols: _sigmoid(y + bm_ref[0:1, cols]))
    proj(gb_ref, g_off + D_MODEL, lambda y, cols: _sigmoid(y + bm_ref[1:2, cols]))


def _block_diag_halves(w):
    per_half = N_BLOCKS // 2
    eye = jnp.eye(per_half, dtype=w.dtype)
    w = w.reshape(w.shape[:-3] + (2, per_half, BLOCK_W, BLOCK_W))
    full = jnp.einsum('...hbde,bc->...hbdce', w, eye)
    return full.reshape(w.shape[:-3] + (per_half * BLOCK_W, per_half * BLOCK_W)).astype(_bf16)


def _front(x2, seq_len, ln_g, ln_b, w_in_bf16, b_merge, conv_w, conv_b, gate_w, gate_b, lam, apply_ln):
    t = x2.shape[0]
    n = t // TM
    per = TM // SUBLANES
    row = lambda i: (i, 0)
    full = lambda a: pl.BlockSpec(a.shape, lambda i: (0,) * a.ndim)
    wg = _block_diag_halves(gate_w)
    bg = (0.5 * gate_b).reshape(2, 2, 1, W_LRU)
    conv_w = 0.5 * conv_w
    cb2 = (0.5 * conv_b).reshape(1, W_LRU)
    widths = (W_ATTN, W_ATTN, W_ATTN, W_ATTN, W_LRU, D_MODEL, D_MODEL, W_LRU, W_LRU, W_LRU)
    out_shape = [jax.ShapeDtypeStruct((t, w), _bf16) for w in widths]
    out_specs = [pl.BlockSpec((TM, w), row) for w in widths]
    if apply_ln:
        out_shape = [jax.ShapeDtypeStruct((t, D_MODEL), _f32)] + out_shape
        out_specs = [pl.BlockSpec((TM, D_MODEL), row)] + out_specs
    stage = pltpu.VMEM((N_LG, N_CHUNK * PITCH, LANES), _f32)
    block_f32 = pltpu.VMEM((TM, W_LRU), _f32)
    return pl.pallas_call(
        functools.partial(_front_kernel, apply_ln=apply_ln, blocks_per_seq=seq_len // TM),
        grid=(n,),
        in_specs=[pl.BlockSpec((TM, D_MODEL), row),
                  pl.BlockSpec((SUBLANES, D_MODEL), lambda i: (jnp.minimum((i + 1) * per, n * per - 1), 0)),
                  full(ln_g), full(ln_b), full(w_in_bf16), full(b_merge), full(conv_w), full(cb2),
                  full(wg), full(bg), full(lam)],
        out_specs=out_specs,
        out_shape=out_shape,
        scratch_shapes=[pltpu.VMEM((SUBLANES, W_LRU), _f32), pltpu.VMEM((N_CHUNK, W_LRU), _f32),
                        pltpu.VMEM((TM, D_MODEL), _bf16), block_f32, pltpu.VMEM((TM, W_LRU), _bf16),
                        block_f32, block_f32, block_f32, block_f32, stage, stage, stage, stage,
                        pltpu.VMEM((4, TM, W_LRU), _f32)],
        compiler_params=pltpu.CompilerParams(dimension_semantics=("arbitrary",),
                                             vmem_limit_bytes=VMEM_LIMIT),
        name="front",
    )(x2, x2, ln_g, ln_b, w_in_bf16, b_merge, conv_w, cb2, wg, bg, lam)


def _attn_value_table(rpb):
    qc = np.arange(GRID_W)[:, None]
    kc = np.arange(GRID_W)[None, :]
    cs = np.clip(qc - WIN_W // 2, 0, GRID_W - WIN_W)
    ok = (kc >= cs) & (kc < cs + WIN_W)
    pad = GRID_W - WIN_W
    padded = jnp.pad(rpb.astype(_f32), ((0, 0), (0, 0), (pad, pad)))
    skew = jnp.tile(padded, (1, 1, GRID_W + 1))[:, :, :GRID_W * 2 * GRID_W]
    skew = skew.reshape(skew.shape[:2] + (GRID_W, 2 * GRID_W))
    t = skew[:, :, ::-1, :GRID_W]
    t = jnp.where(ok, t * LOG2_E, NEG)
    return jnp.concatenate([t[:, :-1], t[:, 1:]], axis=-1)


def _attn_row_mask(rows):
    r_first = (0, Q_ROWS, rows - Q_ROWS)
    m = np.full((3, Q_ROWS // SUB_ROWS, SUB_ROWS, SUBLANES, KEY_ROWS * GRID_W), NEG, np.float32)
    for v, r0 in enumerate(r_first):
        for half in range(Q_ROWS // SUB_ROWS):
            for qr in range(SUB_ROWS):
                r = r0 + half * SUB_ROWS + qr
                rs = min(max(r - WIN_H // 2, 0), rows - WIN_H)
                for kr in range(KEY_ROWS):
                    key_r = r0 + half * SUB_ROWS - WIN_H // 2 + kr
                    if rs <= key_r < rs + WIN_H:
                        m[v, half, qr, :, kr * GRID_W:(kr + 1) * GRID_W] = 0.0
    return m


def _attn_kernel(q_ref, *refs):
    k_refs, v_refs = refs[:N_KV_BLKS], refs[N_KV_BLKS:2 * N_KV_BLKS]
    z_ref, val_ref, rowm_ref, o_ref = refs[2 * N_KV_BLKS:]
    m_q = SUB_ROWS * GRID_W
    n_kb = KEY_ROWS // KV_BLK_ROWS
    blk_keys = KV_BLK_ROWS * GRID_W
    lane = lax.broadcasted_iota(jnp.int32, (m_q, LANES), 1)
    even = lane < HEAD_DIM
    for half in range(Q_ROWS // SUB_ROWS):
        rows = slice(half * m_q, (half + 1) * m_q)
        for pair in range(N_HEADS // 2):
            cols = slice(pair * LANES, (pair + 1) * LANES)
            qp = q_ref[rows, cols]
            zero = jnp.zeros_like(qp)
            q2 = jnp.concatenate([jnp.where(even, qp, zero), jnp.where(even, zero, qp)], axis=0)
            sk = [lax.dot_general(q2, k_refs[half + j][:, cols], (((1,), (1,)), ((), ())),
                                  preferred_element_type=_f32) for j in range(n_kb)]
            s = jnp.concatenate(sk, axis=1)
            blocks = []
            for parity in range(2):
                for qr in range(SUB_ROWS):
                    val = jnp.concatenate(
                        [val_ref[2 * pair + parity, 2 * j - qr + WIN_H // 2 - 1]
                         for j in range(KEY_ROWS // 2)], axis=1)
                    r0 = (parity * SUB_ROWS + qr) * GRID_W
                    blk = s[r0:r0 + GRID_W] + val
                    blk = blk.reshape(GRID_W // SUBLANES, SUBLANES, -1) + rowm_ref[half, qr][None]
                    blocks.append(blk.reshape(GRID_W, -1))
            s = jnp.concatenate(blocks, axis=0)
            m = jnp.max(s, axis=-1, keepdims=True)
            e = jnp.exp2(s - m)
            l = jnp.sum(e, axis=-1, keepdims=True)
            eb = e.astype(_bf16)
            o2 = sum(_dot(eb[:, j * blk_keys:(j + 1) * blk_keys], v_refs[half + j][:, cols])
                     for j in range(n_kb))
            o = jnp.where(even, o2[:m_q] / l[:m_q], o2[m_q:] / l[m_q:])
            o_ref[rows, cols] = (o * z_ref[rows, cols].astype(_f32)).astype(o_ref.dtype)


def _attention(q, k, v, z_a, val, rowm):
    b, s, _ = q.shape
    rows = s // GRID_W
    n_blk = rows // Q_ROWS
    n_kv = rows // KV_BLK_ROWS
    tq = Q_ROWS * GRID_W
    tkv = KV_BLK_ROWS * GRID_W

    def kv_spec(j):
        first = -(WIN_H // 2) // KV_BLK_ROWS
        return pl.BlockSpec(
            (None, tkv, W_ATTN),
            lambda bi, i: (bi, jnp.clip(i * (Q_ROWS // KV_BLK_ROWS) + first + j, 0, n_kv - 1), 0))

    def case(i):
        return jnp.where(i == 0, 0, jnp.where(i == n_blk - 1, 2, 1))

    blk = pl.BlockSpec((None, tq, W_ATTN), lambda bi, i: (bi, i, 0))
    return pl.pallas_call(
        _attn_kernel,
        grid=(b, n_blk),
        in_specs=[blk] + [kv_spec(j) for j in range(N_KV_BLKS)] * 2 + [
            blk,
            pl.BlockSpec(val.shape, lambda bi, i: (0, 0, 0, 0)),
            pl.BlockSpec((None,) + rowm.shape[1:], lambda bi, i: (case(i), 0, 0, 0, 0))],
        out_specs=blk,
        out_shape=jax.ShapeDtypeStruct((b, s, W_ATTN), _bf16),
        compiler_params=pltpu.CompilerParams(dimension_semantics=("arbitrary", "arbitrary"),
                                             vmem_limit_bytes=VMEM_LIMIT),
        name="attn",
    )(q, *([k] * N_KV_BLKS), *([v] * N_KV_BLKS), z_a, val, rowm)


def _back_kernel(x_ref, ya_ref, hf_ref, hb0_ref, pb_ref, zb_ref, ga_ref, gb_ref, wba_ref, wbb_ref,
                 wo_ref, g_ref, b_ref, o_ref, carry_ref, *, alpha, blocks_per_seq):
    i = pl.program_id(0)

    @pl.when(i % blocks_per_seq == 0)
    def _():
        carry_ref[...] = jnp.zeros_like(carry_ref)

    carry = carry_ref[0:1, :]
    parts = []
    for sb in range(TB // TM - 1, -1, -1):
        rows = slice(sb * TM, (sb + 1) * TM)
        part = hb0_ref[rows, :].astype(_f32) + pb_ref[rows, :].astype(_f32) * carry
        carry = part[0:1, :]
        parts.insert(0, part)
    h_b = jnp.concatenate(parts, axis=0)
    carry_ref[...] = h_b[0:SUBLANES, :]
    yb = ((hf_ref[...].astype(_f32) + h_b) * zb_ref[...].astype(_f32)).astype(_bf16)
    pa = _dot(ya_ref[...], wba_ref[...])
    pb = _dot(yb, wbb_ref[...])
    m = ga_ref[...].astype(_f32) * pa + gb_ref[...].astype(_f32) * pb
    out = _dot(m.astype(_bf16), wo_ref[...])
    o_ref[...] = _layer_norm(alpha * x_ref[...] + out, g_ref[...], b_ref[...])


def _back(x2, seq_len, y_a, h_f, h_b0, p_b, z_b, g_a, g_b, w_ba, w_bb, w_out, ln_g, ln_b, alpha):
    t = x2.shape[0]
    n = t // TB
    rev = lambda i: (n - 1 - i, 0)
    const = lambda i: (0, 0)
    rows = lambda w: pl.BlockSpec((TB, w), rev)
    full = lambda a: pl.BlockSpec(a.shape, const)
    return pl.pallas_call(
        functools.partial(_back_kernel, alpha=alpha, blocks_per_seq=seq_len // TB),
        grid=(n,),
        in_specs=[rows(D_MODEL), rows(W_ATTN), rows(W_LRU), rows(W_LRU), rows(W_LRU), rows(W_LRU),
                  rows(D_MODEL), rows(D_MODEL), full(w_ba), full(w_bb), full(w_out), full(ln_g), full(ln_b)],
        out_specs=rows(D_MODEL),
        out_shape=jax.ShapeDtypeStruct((t, D_MODEL), _f32),
        scratch_shapes=[pltpu.VMEM((SUBLANES, W_LRU), _f32)],
        compiler_params=pltpu.CompilerParams(dimension_semantics=("arbitrary",),
                                             vmem_limit_bytes=VMEM_LIMIT),
        name="back",
    )(x2, y_a, h_f, h_b0, p_b, z_b, g_a, g_b, w_ba, w_bb, w_out, ln_g, ln_b)


def kernel(x, emb_ln_g, emb_ln_b, w_in, rpb, conv_w, conv_b, lru_gate_w, lru_gate_b, lru_lambda,
           w_branch_attn, w_branch_lru, b_merge, w_out, ln_g, ln_b):
    b, s, d = x.shape
    depth = w_in.shape[0]
    assert d == D_MODEL and s % (Q_ROWS * GRID_W) == 0 and s % TM == 0 and s % TB == 0 and TB % TM == 0
    alpha = (2 * depth) ** 0.25
    t = b * s
    h = x.reshape(t, d)
    row2 = lambda a: a.reshape(1, -1)
    rowm = _attn_row_mask(s // GRID_W)
    for l in range(depth):
        outs = _front(h, s, row2(emb_ln_g), row2(emb_ln_b), w_in[l].astype(_bf16), b_merge[l], conv_w[l],
                      conv_b[l], lru_gate_w[l], lru_gate_b[l], lru_lambda[l], apply_ln=(l == 0))
        if l == 0:
            h, outs = outs[0], outs[1:]
        q, k, v, z_a, z_b, g_a, g_b, h_f, h_b0, p_b = outs
        seq = lambda a: a.reshape(b, s, a.shape[-1])
        y_a = _attention(seq(q), seq(k), seq(v), seq(z_a), _attn_value_table(rpb[l]), rowm)
        h = _back(h, s, y_a.reshape(t, W_ATTN), h_f, h_b0, p_b, z_b, g_a, g_b,
                  w_branch_attn[l].astype(_bf16), w_branch_lru[l].astype(_bf16),
                  w_out[l].astype(_bf16), row2(ln_g[l]), row2(ln_b[l]), alpha)
    return h.reshape(b, s, d)
```

```python
import functools

import numpy as np
import jax
import jax.numpy as jnp
from jax import lax
from jax.experimental import pallas as pl
from jax.experimental.pallas import tpu as pltpu

D_MODEL = 1024
GRID_W = 64
N_HEADS = 8
HEAD_DIM = 64
W_ATTN = N_HEADS * HEAD_DIM
WIN_H = 8
WIN_W = 16
W_LRU = 512
N_BLOCKS = 8
BLOCK_W = 64
CONV_W = 4
C_LRU = 8.0
LN_EPS = 1e-5
NEG = -1e30
TINY = 1e-30
LOG2_E = 1.4426950408889634

LANES = 128
SUBLANES = 8
VMEM_LIMIT = 56 * 1024 * 1024

TM = 512
TB = 1024
Q_ROWS = 16
SUB_ROWS = 4
KEY_ROWS = SUB_ROWS + WIN_H
KV_BLK_ROWS = 4
N_KV_BLKS = (Q_ROWS + WIN_H) // KV_BLK_ROWS
N_CHUNK = SUBLANES
G_LEN = TM // N_CHUNK
PITCH = G_LEN + 8
N_LG = W_LRU // LANES
PROJ_COLS = 256
N_VEC_PIECES = 12

_f32 = jnp.float32
_bf16 = jnp.bfloat16


def _dot(a, b):
    return jnp.dot(a, b, preferred_element_type=_f32)


def _layer_norm(x, g, b):
    mu = jnp.mean(x, axis=-1, keepdims=True)
    xc = x - mu
    var = jnp.mean(xc * xc, axis=-1, keepdims=True)
    return xc * lax.rsqrt(var + LN_EPS) * g + b


def _sigmoid(x):
    return 0.5 + 0.5 * jnp.tanh(0.5 * x)


def _silu(x):
    hx = 0.5 * x
    return hx + hx * jnp.tanh(hx)


def _stage_time_major(u, ut_ref):
    for ch in range(N_CHUNK):
        for lg in range(N_LG):
            ut_ref[lg, ch * PITCH:ch * PITCH + G_LEN, :] = u[ch * G_LEN:(ch + 1) * G_LEN,
                                                            lg * LANES:(lg + 1) * LANES]


def _load_slab(ut_ref, g):
    return jnp.concatenate([ut_ref[lg, pl.ds(g, N_CHUNK, stride=PITCH), :] for lg in range(N_LG)], axis=1)


def _slab_rows(g):
    return slice(g * N_CHUNK, (g + 1) * N_CHUNK)


def _unstage_slabs(slab_fn, ot_ref, g_range):
    for g in g_range:
        slab = slab_fn(g)
        for lg in range(N_LG):
            ot_ref[lg, pl.ds(g, N_CHUNK, stride=PITCH), :] = slab[:, lg * LANES:(lg + 1) * LANES]


def _copy_time_major(ot_ref, o_ref, ch_range):
    for ch in ch_range:
        for lg in range(N_LG):
            o_ref[ch * G_LEN:(ch + 1) * G_LEN, lg * LANES:(lg + 1) * LANES] = (
                ot_ref[lg, ch * PITCH:ch * PITCH + G_LEN, :].astype(o_ref.dtype))


def _gate_dots(ub_ref, wg_ref):
    half = W_LRU // 2
    pre = []
    for gi in range(2):
        cols = [_dot(ub_ref[:, hf * half:(hf + 1) * half], wg_ref[gi, hf]) for hf in range(2)]
        pre.append(jnp.concatenate(cols, axis=1))
    return pre


def _gates(pre, hu_ref, bg_ref, coef, first, g_first, a_ref, bx_ref):
    t_r = jnp.tanh(pre[0] + bg_ref[0])
    t_i = jnp.tanh(pre[1] + bg_ref[1])
    a = jnp.exp2(coef + coef * t_r)
    one_m_a2 = (1.0 - a) * (1.0 + a)
    mult = one_m_a2 * lax.rsqrt(jnp.maximum(one_m_a2, TINY))
    hu = hu_ref[...]
    iu = hu + hu * t_i
    bx = mult * iu
    a_ref[...] = a
    bx_ref[...] = bx
    fix = _slab_rows(g_first)
    bx_ref[fix, :] = jnp.where(first, iu[fix], bx[fix])


def _chunk_scan(a_ref, bx_ref, hl_ref, pl_ref, g_range, zero):
    h = zero
    p = jnp.ones((N_CHUNK, W_LRU), _f32)
    for g in g_range:
        rows = _slab_rows(g)
        ag = a_ref[rows, :]
        h = ag * h + bx_ref[rows, :]
        p = ag * p
        hl_ref[rows, :] = h
        pl_ref[rows, :] = p
    return h, p


def _front_kernel(x_ref, xn_ref, g_ref, b_ref, w_ref, bm_ref, cw_ref, cb_ref, wg_ref, bg_ref, lam_ref,
                  *refs, apply_ln, blocks_per_seq):
    n_out = 11 if apply_ln else 10
    out_refs = refs[:n_out]
    (uprev_ref, carry_ref, xb_ref, hu_ref, ub_ref, a_ref, bx_ref, hl_ref, pl_ref,
     ut_ref, ot_f, ot_h, ot_p, pre_ref) = refs[n_out:]
    if apply_ln:
        h_ref, out_refs = out_refs[0], out_refs[1:]
    q_ref, k_ref, v_ref, za_ref, zb_ref, ga_ref, gb_ref, hf_ref, hb0_ref, pb_ref = out_refs
    j = pl.program_id(0)
    is_first_blk = j % blocks_per_seq == 0
    is_last_blk = j % blocks_per_seq == blocks_per_seq - 1
    u_off = 4 * W_ATTN
    g_off = u_off + 2 * W_LRU
    sub = lax.broadcasted_iota(jnp.int32, (N_CHUNK, W_LRU), 0)

    def edge(row):
        return jnp.broadcast_to(row, (N_CHUNK, W_LRU))

    def proj_chunks(o_ref, off, post):
        def chunk(c0):
            def emit():
                y = _dot(xb_ref[...], w_ref[:, off + c0:off + c0 + PROJ_COLS])
                o_ref[:, c0:c0 + PROJ_COLS] = post(y, slice(c0, c0 + PROJ_COLS)).astype(o_ref.dtype)
                bits = pltpu.bitcast(y[0:SUBLANES, 0:LANES], jnp.int32)
                zero = lax.shift_right_logical(lax.shift_right_logical(bits, 16), 16).astype(_f32)
                return jnp.concatenate([zero] * N_LG, axis=1)
            return emit
        return [chunk(c0) for c0 in range(0, o_ref.shape[-1], PROJ_COLS)]

    mxu_work = (proj_chunks(q_ref, 0, lambda y, cols: y * (HEAD_DIM ** -0.5 * LOG2_E))
                + proj_chunks(k_ref, W_ATTN, lambda y, cols: y)
                + proj_chunks(v_ref, 2 * W_ATTN, lambda y, cols: y)
                + proj_chunks(za_ref, 3 * W_ATTN, lambda y, cols: _silu(y))
                + proj_chunks(zb_ref, u_off + W_LRU, lambda y, cols: _silu(y))
                + proj_chunks(ga_ref, g_off, lambda y, cols: _sigmoid(y + bm_ref[0:1, cols]))
                + proj_chunks(gb_ref, g_off + D_MODEL, lambda y, cols: _sigmoid(y + bm_ref[1:2, cols])))
    n_mxu, done = len(mxu_work), [0]

    def with_mxu():
        done[0] += 1
        zero = jnp.zeros((N_CHUNK, W_LRU), _f32)
        while len(mxu_work) > n_mxu - (done[0] * n_mxu) // N_VEC_PIECES:
            zero = zero + mxu_work.pop(0)()
        return zero

    @pl.when(is_first_blk)
    def _():
        uprev_ref[...] = jnp.zeros_like(uprev_ref)
        carry_ref[...] = jnp.zeros_like(carry_ref)

    x = x_ref[...]
    xn = xn_ref[...]
    if apply_ln:
        x = _layer_norm(x, g_ref[...], b_ref[...])
        xn = _layer_norm(xn, g_ref[...], b_ref[...])
        h_ref[...] = x
    xb_ref[...] = x.astype(_bf16)
    u = _dot(xb_ref[...], w_ref[:, u_off:u_off + W_LRU])
    u_next = _dot(xn.astype(_bf16), w_ref[:, u_off:u_off + W_LRU])
    _stage_time_major(u, ut_ref)

    slabs = {}
    slabs[-2] = jnp.where(sub == 0, edge(uprev_ref[SUBLANES - 2:SUBLANES - 1, :]),
                          pltpu.roll(_load_slab(ut_ref, G_LEN - 2), 1, axis=0))
    slabs[-1] = jnp.where(sub == 0, edge(uprev_ref[SUBLANES - 1:SUBLANES, :]),
                          pltpu.roll(_load_slab(ut_ref, G_LEN - 1), 1, axis=0))
    nxt = jnp.where(is_last_blk, 0.0, u_next[0:1, :])
    slabs[G_LEN] = jnp.where(sub == N_CHUNK - 1, edge(nxt), pltpu.roll(_load_slab(ut_ref, 0), N_CHUNK - 1, axis=0))
    uprev_ref[...] = u[TM - SUBLANES:TM, :]

    def slab(g):
        if g not in slabs:
            slabs[g] = _load_slab(ut_ref, g)
        return slabs[g]

    cw = [cw_ref[i:i + 1, :] for i in range(CONV_W)]
    cb = cb_ref[...]
    quarter = G_LEN // 4
    for part in range(4):
        cb_part = cb + with_mxu()[0:1]
        for g in range(part * quarter, (part + 1) * quarter):
            hu_ref[_slab_rows(g), :] = (slab(g - 2) * cw[0] + slab(g - 1) * cw[1] + slab(g) * cw[2]
                                        + slab(g + 1) * cw[3] + cb_part)
    ub_ref[...] = hu_ref[...].astype(_bf16)

    for di in range(2):
        pre = _gate_dots(ub_ref, wg_ref.at[di])
        pre_ref[2 * di] = pre[0]
        pre_ref[2 * di + 1] = pre[1]

    neg_lam = -lam_ref[...]
    softplus = jnp.maximum(neg_lam, 0.0) + jnp.log(1.0 + jnp.exp(-jnp.abs(neg_lam)))
    coef = (-0.5 * C_LRU * LOG2_E) * softplus

    halves = (range(0, G_LEN // 2), range(G_LEN // 2, G_LEN))
    first = jnp.logical_and(is_first_blk, sub == 0)
    _gates((pre_ref[0], pre_ref[1]), hu_ref, bg_ref.at[0], coef[0:1] + with_mxu()[0:1], first, 0,
           a_ref, bx_ref)
    h_end, p_end = _chunk_scan(a_ref, bx_ref, hl_ref, pl_ref, range(G_LEN), with_mxu())
    c = jnp.where(sub == 0, edge(carry_ref[N_CHUNK - 1:N_CHUNK, :]), 0.0)
    for kk in range(1, N_CHUNK):
        c = jnp.where(sub == kk, pltpu.roll(p_end * c + h_end, 1, axis=0), c)
    carry_ref[...] = p_end * c + h_end
    for g_range in halves:
        c_part = c + with_mxu()
        _unstage_slabs(lambda g: pl_ref[_slab_rows(g), :] * c_part + hl_ref[_slab_rows(g), :], ot_f, g_range)
    _copy_time_major(ot_f, hf_ref, range(N_CHUNK))

    last = jnp.logical_and(is_last_blk, sub == N_CHUNK - 1)
    _gates((pre_ref[2], pre_ref[3]), hu_ref, bg_ref.at[1], coef[1:2] + with_mxu()[0:1], last, G_LEN - 1,
           a_ref, bx_ref)
    h_end, p_end = _chunk_scan(a_ref, bx_ref, hl_ref, pl_ref, range(G_LEN - 1, -1, -1), with_mxu())
    cz = jnp.zeros((N_CHUNK, W_LRU), _f32)
    d = jnp.where(sub == N_CHUNK - 1, 1.0, 0.0)
    for kk in range(N_CHUNK - 2, -1, -1):
        cz = jnp.where(sub == kk, pltpu.roll(p_end * cz + h_end, N_CHUNK - 1, axis=0), cz)
        d = jnp.where(sub == kk, pltpu.roll(p_end * d, N_CHUNK - 1, axis=0), d)
    for g_range in halves:
        cz_part = cz + with_mxu()
        _unstage_slabs(lambda g: pl_ref[_slab_rows(g), :] * cz_part + hl_ref[_slab_rows(g), :], ot_h, g_range)
        _unstage_slabs(lambda g: pl_ref[_slab_rows(g), :] * d, ot_p, g_range)
    _copy_time_major(ot_h, hb0_ref, range(N_CHUNK))
    _copy_time_major(ot_p, pb_ref, range(N_CHUNK))
    assert done[0] == N_VEC_PIECES and not mxu_work


def _block_diag_halves(w):
    per_half = N_BLOCKS // 2
    eye = jnp.eye(per_half, dtype=w.dtype)
    w = w.reshape(w.shape[:-3] + (2, per_half, BLOCK_W, BLOCK_W))
    full = jnp.einsum('...hbde,bc->...hbdce', w, eye)
    return full.reshape(w.shape[:-3] + (per_half * BLOCK_W, per_half * BLOCK_W)).astype(_bf16)


def _front(x2, seq_len, ln_g, ln_b, w_in_bf16, b_merge, conv_w, conv_b, gate_w, gate_b, lam, apply_ln):
    t = x2.shape[0]
    n = t // TM
    per = TM // SUBLANES
    row = lambda i: (i, 0)
    full = lambda a: pl.BlockSpec(a.shape, lambda i: (0,) * a.ndim)
    wg = _block_diag_halves(gate_w)
    bg = (0.5 * gate_b).reshape(2, 2, 1, W_LRU)
    conv_w = 0.5 * conv_w
    cb2 = (0.5 * conv_b).reshape(1, W_LRU)
    widths = (W_ATTN, W_ATTN, W_ATTN, W_ATTN, W_LRU, D_MODEL, D_MODEL, W_LRU, W_LRU, W_LRU)
    out_shape = [jax.ShapeDtypeStruct((t, w), _bf16) for w in widths]
    out_specs = [pl.BlockSpec((TM, w), row) for w in widths]
    if apply_ln:
        out_shape = [jax.ShapeDtypeStruct((t, D_MODEL), _f32)] + out_shape
        out_specs = [pl.BlockSpec((TM, D_MODEL), row)] + out_specs
    stage = pltpu.VMEM((N_LG, N_CHUNK * PITCH, LANES), _f32)
    block_f32 = pltpu.VMEM((TM, W_LRU), _f32)
    return pl.pallas_call(
        functools.partial(_front_kernel, apply_ln=apply_ln, blocks_per_seq=seq_len // TM),
        grid=(n,),
        in_specs=[pl.BlockSpec((TM, D_MODEL), row),
                  pl.BlockSpec((SUBLANES, D_MODEL), lambda i: (jnp.minimum((i + 1) * per, n * per - 1), 0)),
                  full(ln_g), full(ln_b), full(w_in_bf16), full(b_merge), full(conv_w), full(cb2),
                  full(wg), full(bg), full(lam)],
        out_specs=out_specs,
        out_shape=out_shape,
        scratch_shapes=[pltpu.VMEM((SUBLANES, W_LRU), _f32), pltpu.VMEM((N_CHUNK, W_LRU), _f32),
                        pltpu.VMEM((TM, D_MODEL), _bf16), block_f32, pltpu.VMEM((TM, W_LRU), _bf16),
                        block_f32, block_f32, block_f32, block_f32, stage, stage, stage, stage,
                        pltpu.VMEM((4, TM, W_LRU), _f32)],
        compiler_params=pltpu.CompilerParams(dimension_semantics=("arbitrary",),
                                             vmem_limit_bytes=VMEM_LIMIT),
        name="front",
    )(x2, x2, ln_g, ln_b, w_in_bf16, b_merge, conv_w, cb2, wg, bg, lam)


def _attn_value_table(rpb):
    qc = np.arange(GRID_W)[:, None]
    kc = np.arange(GRID_W)[None, :]
    cs = np.clip(qc - WIN_W // 2, 0, GRID_W - WIN_W)
    ok = (kc >= cs) & (kc < cs + WIN_W)
    pad = GRID_W - WIN_W
    padded = jnp.pad(rpb.astype(_f32), ((0, 0), (0, 0), (pad, pad)))
    skew = jnp.tile(padded, (1, 1, GRID_W + 1))[:, :, :GRID_W * 2 * GRID_W]
    skew = skew.reshape(skew.shape[:2] + (GRID_W, 2 * GRID_W))
    t = skew[:, :, ::-1, :GRID_W]
    t = jnp.where(ok, t * LOG2_E, NEG)
    return jnp.concatenate([t[:, :-1], t[:, 1:]], axis=-1)


def _attn_row_mask(rows):
    r_first = (0, Q_ROWS, rows - Q_ROWS)
    m = np.full((3, Q_ROWS // SUB_ROWS, SUB_ROWS, SUBLANES, KEY_ROWS * GRID_W), NEG, np.float32)
    for v, r0 in enumerate(r_first):
        for half in range(Q_ROWS // SUB_ROWS):
            for qr in range(SUB_ROWS):
                r = r0 + half * SUB_ROWS + qr
                rs = min(max(r - WIN_H // 2, 0), rows - WIN_H)
                for kr in range(KEY_ROWS):
                    key_r = r0 + half * SUB_ROWS - WIN_H // 2 + kr
                    if rs <= key_r < rs + WIN_H:
                        m[v, half, qr, :, kr * GRID_W:(kr + 1) * GRID_W] = 0.0
    return m


def _attn_kernel(q_ref, *refs):
    k_refs, v_refs = refs[:N_KV_BLKS], refs[N_KV_BLKS:2 * N_KV_BLKS]
    z_ref, val_ref, rowm_ref, o_ref = refs[2 * N_KV_BLKS:]
    m_q = SUB_ROWS * GRID_W
    n_kb = KEY_ROWS // KV_BLK_ROWS
    blk_keys = KV_BLK_ROWS * GRID_W
    lane = lax.broadcasted_iota(jnp.int32, (m_q, LANES), 1)
    even = lane < HEAD_DIM
    for half in range(Q_ROWS // SUB_ROWS):
        rows = slice(half * m_q, (half + 1) * m_q)
        for pair in range(N_HEADS // 2):
            cols = slice(pair * LANES, (pair + 1) * LANES)
            qp = q_ref[rows, cols]
            zero = jnp.zeros_like(qp)
            q2 = jnp.concatenate([jnp.where(even, qp, zero), jnp.where(even, zero, qp)], axis=0)
            sk = [lax.dot_general(q2, k_refs[half + j][:, cols], (((1,), (1,)), ((), ())),
                                  preferred_element_type=_f32) for j in range(n_kb)]
            s = jnp.concatenate(sk, axis=1)
            blocks = []
            for parity in range(2):
                for qr in range(SUB_ROWS):
                    val = jnp.concatenate(
                        [val_ref[2 * pair + parity, 2 * j - qr + WIN_H // 2 - 1]
                         for j in range(KEY_ROWS // 2)], axis=1)
                    r0 = (parity * SUB_ROWS + qr) * GRID_W
                    blk = s[r0:r0 + GRID_W] + val
                    blk = blk.reshape(GRID_W // SUBLANES, SUBLANES, -1) + rowm_ref[half, qr][None]
                    blocks.append(blk.reshape(GRID_W, -1))
            s = jnp.concatenate(blocks, axis=0)
            m = jnp.max(s, axis=-1, keepdims=True)
            e = jnp.exp2(s - m)
            l = jnp.sum(e, axis=-1, keepdims=True)
            eb = e.astype(_bf16)
            o2 = sum(_dot(eb[:, j * blk_keys:(j + 1) * blk_keys], v_refs[half + j][:, cols])
                     for j in range(n_kb))
            o = jnp.where(even, o2[:m_q] / l[:m_q], o2[m_q:] / l[m_q:])
            o_ref[rows, cols] = (o * z_ref[rows, cols].astype(_f32)).astype(o_ref.dtype)


def _attention(q, k, v, z_a, val, rowm):
    b, s, _ = q.shape
    rows = s // GRID_W
    n_blk = rows // Q_ROWS
    n_kv = rows // KV_BLK_ROWS
    tq = Q_ROWS * GRID_W
    tkv = KV_BLK_ROWS * GRID_W

    def kv_spec(j):
        first = -(WIN_H // 2) // KV_BLK_ROWS
        return pl.BlockSpec(
            (None, tkv, W_ATTN),
            lambda bi, i: (bi, jnp.clip(i * (Q_ROWS // KV_BLK_ROWS) + first + j, 0, n_kv - 1), 0))

    def case(i):
        return jnp.where(i == 0, 0, jnp.where(i == n_blk - 1, 2, 1))

    blk = pl.BlockSpec((None, tq, W_ATTN), lambda bi, i: (bi, i, 0))
    return pl.pallas_call(
        _attn_kernel,
        grid=(b, n_blk),
        in_specs=[blk] + [kv_spec(j) for j in range(N_KV_BLKS)] * 2 + [
            blk,
            pl.BlockSpec(val.shape, lambda bi, i: (0, 0, 0, 0)),
            pl.BlockSpec((None,) + rowm.shape[1:], lambda bi, i: (case(i), 0, 0, 0, 0))],
        out_specs=blk,
        out_shape=jax.ShapeDtypeStruct((b, s, W_ATTN), _bf16),
        compiler_params=pltpu.CompilerParams(dimension_semantics=("arbitrary", "arbitrary"),
                                             vmem_limit_bytes=VMEM_LIMIT),
        name="attn",
    )(q, *([k] * N_KV_BLKS), *([v] * N_KV_BLKS), z_a, val, rowm)


def _back_kernel(x_ref, ya_ref, hf_ref, hb0_ref, pb_ref, zb_ref, ga_ref, gb_ref, wba_ref, wbb_ref,
                 wo_ref, g_ref, b_ref, o_ref, carry_ref, *, alpha, blocks_per_seq):
    i = pl.program_id(0)

    @pl.when(i % blocks_per_seq == 0)
    def _():
        carry_ref[...] = jnp.zeros_like(carry_ref)

    carry = carry_ref[0:1, :]
    parts = []
    for sb in range(TB // TM - 1, -1, -1):
        rows = slice(sb * TM, (sb + 1) * TM)
        part = hb0_ref[rows, :].astype(_f32) + pb_ref[rows, :].astype(_f32) * carry
        carry = part[0:1, :]
        parts.insert(0, part)
    h_b = jnp.concatenate(parts, axis=0)
    carry_ref[...] = h_b[0:SUBLANES, :]
    yb = ((hf_ref[...].astype(_f32) + h_b) * zb_ref[...].astype(_f32)).astype(_bf16)
    pa = _dot(ya_ref[...], wba_ref[...])
    pb = _dot(yb, wbb_ref[...])
    m = ga_ref[...].astype(_f32) * pa + gb_ref[...].astype(_f32) * pb
    out = _dot(m.astype(_bf16), wo_ref[...])
    o_ref[...] = _layer_norm(alpha * x_ref[...] + out, g_ref[...], b_ref[...])


def _back(x2, seq_len, y_a, h_f, h_b0, p_b, z_b, g_a, g_b, w_ba, w_bb, w_out, ln_g, ln_b, alpha):
    t = x2.shape[0]
    n = t // TB
    rev = lambda i: (n - 1 - i, 0)
    const = lambda i: (0, 0)
    rows = lambda w: pl.BlockSpec((TB, w), rev)
    full = lambda a: pl.BlockSpec(a.shape, const)
    return pl.pallas_call(
        functools.partial(_back_kernel, alpha=alpha, blocks_per_seq=seq_len // TB),
        grid=(n,),
        in_specs=[rows(D_MODEL), rows(W_ATTN), rows(W_LRU), rows(W_LRU), rows(W_LRU), rows(W_LRU),
                  rows(D_MODEL), rows(D_MODEL), full(w_ba), full(w_bb), full(w_out), full(ln_g), full(ln_b)],
        out_specs=rows(D_MODEL),
        out_shape=jax.ShapeDtypeStruct((t, D_MODEL), _f32),
        scratch_shapes=[pltpu.VMEM((SUBLANES, W_LRU), _f32)],
        compiler_params=pltpu.CompilerParams(dimension_semantics=("arbitrary",),
                                             vmem_limit_bytes=VMEM_LIMIT),
        name="back",
    )(x2, y_a, h_f, h_b0, p_b, z_b, g_a, g_b, w_ba, w_bb, w_out, ln_g, ln_b)


def kernel(x, emb_ln_g, emb_ln_b, w_in, rpb, conv_w, conv_b, lru_gate_w, lru_gate_b, lru_lambda,
           w_branch_attn, w_branch_lru, b_merge, w_out, ln_g, ln_b):
    b, s, d = x.shape
    depth = w_in.shape[0]
    assert d == D_MODEL and s % (Q_ROWS * GRID_W) == 0 and s % TM == 0 and s % TB == 0 and TB % TM == 0
    alpha = (2 * depth) ** 0.25
    t = b * s
    h = x.reshape(t, d)
    row2 = lambda a: a.reshape(1, -1)
    rowm = _attn_row_mask(s // GRID_W)
    for l in range(depth):
        outs = _front(h, s, row2(emb_ln_g), row2(emb_ln_b), w_in[l].astype(_bf16), b_merge[l], conv_w[l],
                      conv_b[l], lru_gate_w[l], lru_gate_b[l], lru_lambda[l], apply_ln=(l == 0))
        if l == 0:
            h, outs = outs[0], outs[1:]
        q, k, v, z_a, z_b, g_a, g_b, h_f, h_b0, p_b = outs
        seq = lambda a: a.reshape(b, s, a.shape[-1])
        y_a = _attention(seq(q), seq(k), seq(v), seq(z_a), _attn_value_table(rpb[l]), rowm)
        h = _back(h, s, y_a.reshape(t, W_ATTN), h_f, h_b0, p_b, z_b, g_a, g_b,
                  w_branch_attn[l].astype(_bf16), w_branch_lru[l].astype(_bf16),
                  w_out[l].astype(_bf16), row2(ln_g[l]), row2(ln_b[l]), alpha)
    return h.reshape(b, s, d)
```

```python
import functools

import numpy as np
import jax
import jax.numpy as jnp
from jax import lax
from jax.experimental import pallas as pl
from jax.experimental.pallas import tpu as pltpu

D_MODEL = 1024
GRID_W = 64
N_HEADS = 8
HEAD_DIM = 64
W_ATTN = N_HEADS * HEAD_DIM
WIN_H = 8
WIN_W = 16
W_LRU = 512
N_BLOCKS = 8
BLOCK_W = 64
CONV_W = 4
C_LRU = 8.0
LN_EPS = 1e-5
NEG = -1e30
TINY = 1e-30
LOG2_E = 1.4426950408889634

LANES = 128
SUBLANES = 8
VMEM_LIMIT = 56 * 1024 * 1024

TM = 512
TB = 1024
Q_ROWS = 16
SUB_ROWS = 4
KEY_ROWS = SUB_ROWS + WIN_H
KV_BLK_ROWS = 4
N_KV_BLKS = (Q_ROWS + WIN_H) // KV_BLK_ROWS
N_CHUNK = SUBLANES
G_LEN = TM // N_CHUNK
PITCH = G_LEN + 8
N_LG = W_LRU // LANES
PROJ_COLS = 256

_f32 = jnp.float32
_bf16 = jnp.bfloat16


def _dot(a, b):
    return jnp.dot(a, b, preferred_element_type=_f32)


def _layer_norm(x, g, b):
    mu = jnp.mean(x, axis=-1, keepdims=True)
    xc = x - mu
    var = jnp.mean(xc * xc, axis=-1, keepdims=True)
    return xc * lax.rsqrt(var + LN_EPS) * g + b


def _sigmoid(x):
    return 0.5 + 0.5 * jnp.tanh(0.5 * x)


def _silu(x):
    hx = 0.5 * x
    return hx + hx * jnp.tanh(hx)


def _stage_time_major(u, ut_ref):
    for ch in range(N_CHUNK):
        for lg in range(N_LG):
            ut_ref[lg, ch * PITCH:ch * PITCH + G_LEN, :] = u[ch * G_LEN:(ch + 1) * G_LEN,
                                                            lg * LANES:(lg + 1) * LANES]


def _load_slab(ut_ref, g):
    return jnp.concatenate([ut_ref[lg, pl.ds(g, N_CHUNK, stride=PITCH), :] for lg in range(N_LG)], axis=1)


def _slab_rows(g):
    return slice(g * N_CHUNK, (g + 1) * N_CHUNK)


def _unstage_slabs(slab_fn, ot_ref, g_range):
    for g in g_range:
        slab = slab_fn(g)
        for lg in range(N_LG):
            ot_ref[lg, pl.ds(g, N_CHUNK, stride=PITCH), :] = slab[:, lg * LANES:(lg + 1) * LANES]


def _copy_time_major(ot_ref, o_ref, ch_range):
    for ch in ch_range:
        for lg in range(N_LG):
            o_ref[ch * G_LEN:(ch + 1) * G_LEN, lg * LANES:(lg + 1) * LANES] = (
                ot_ref[lg, ch * PITCH:ch * PITCH + G_LEN, :].astype(o_ref.dtype))


def _gate_dots(ub_ref, wg_ref):
    half = W_LRU // 2
    pre = []
    for gi in range(2):
        cols = [_dot(ub_ref[:, hf * half:(hf + 1) * half], wg_ref[gi, hf]) for hf in range(2)]
        pre.append(jnp.concatenate(cols, axis=1))
    return pre


def _gates(pre, hu_ref, bg_ref, coef, first, g_first, a_ref, bx_ref):
    t_r = jnp.tanh(pre[0] + bg_ref[0])
    t_i = jnp.tanh(pre[1] + bg_ref[1])
    a = jnp.exp2(coef + coef * t_r)
    one_m_a2 = (1.0 - a) * (1.0 + a)
    mult = one_m_a2 * lax.rsqrt(jnp.maximum(one_m_a2, TINY))
    hu = hu_ref[...]
    iu = hu + hu * t_i
    bx = mult * iu
    a_ref[...] = a
    bx_ref[...] = bx
    fix = _slab_rows(g_first)
    bx_ref[fix, :] = jnp.where(first, iu[fix], bx[fix])


def _chunk_scan(a_ref, bx_ref, hl_ref, pl_ref, g_range):
    h = jnp.zeros((N_CHUNK, W_LRU), _f32)
    p = jnp.ones((N_CHUNK, W_LRU), _f32)
    for g in g_range:
        rows = _slab_rows(g)
        ag = a_ref[rows, :]
        h = ag * h + bx_ref[rows, :]
        p = ag * p
        hl_ref[rows, :] = h
        pl_ref[rows, :] = p
    return h, p


def _front_kernel(x_ref, xn_ref, g_ref, b_ref, w_ref, bm_ref, cw_ref, cb_ref, wg_ref, bg_ref, lam_ref,
                  *refs, apply_ln, blocks_per_seq):
    n_out = 11 if apply_ln else 10
    out_refs = refs[:n_out]
    (uprev_ref, carry_ref, xb_ref, hu_ref, ub_ref, a_ref, bx_ref, hl_ref, pl_ref,
     ut_ref, ot_f, ot_h, ot_p, pre_ref) = refs[n_out:]
    if apply_ln:
        h_ref, out_refs = out_refs[0], out_refs[1:]
    q_ref, k_ref, v_ref, za_ref, zb_ref, ga_ref, gb_ref, hf_ref, hb0_ref, pb_ref = out_refs
    j = pl.program_id(0)
    is_first_blk = j % blocks_per_seq == 0
    is_last_blk = j % blocks_per_seq == blocks_per_seq - 1
    u_off = 4 * W_ATTN
    g_off = u_off + 2 * W_LRU
    sub = lax.broadcasted_iota(jnp.int32, (N_CHUNK, W_LRU), 0)

    def edge(row):
        return jnp.broadcast_to(row, (N_CHUNK, W_LRU))

    def proj(o_ref, off, post):
        for c0 in range(0, o_ref.shape[-1], PROJ_COLS):
            y = _dot(xb_ref[...], w_ref[:, off + c0:off + c0 + PROJ_COLS])
            o_ref[:, c0:c0 + PROJ_COLS] = post(y, slice(c0, c0 + PROJ_COLS)).astype(o_ref.dtype)

    @pl.when(is_first_blk)
    def _():
        uprev_ref[...] = jnp.zeros_like(uprev_ref)
        carry_ref[...] = jnp.zeros_like(carry_ref)

    x = x_ref[...]
    xn = xn_ref[...]
    if apply_ln:
        x = _layer_norm(x, g_ref[...], b_ref[...])
        xn = _layer_norm(xn, g_ref[...], b_ref[...])
        h_ref[...] = x
    xb_ref[...] = x.astype(_bf16)
    u = _dot(xb_ref[...], w_ref[:, u_off:u_off + W_LRU])
    u_next = _dot(xn.astype(_bf16), w_ref[:, u_off:u_off + W_LRU])
    _stage_time_major(u, ut_ref)

    slabs = {}
    slabs[-2] = jnp.where(sub == 0, edge(uprev_ref[SUBLANES - 2:SUBLANES - 1, :]),
                          pltpu.roll(_load_slab(ut_ref, G_LEN - 2), 1, axis=0))
    slabs[-1] = jnp.where(sub == 0, edge(uprev_ref[SUBLANES - 1:SUBLANES, :]),
                          pltpu.roll(_load_slab(ut_ref, G_LEN - 1), 1, axis=0))
    nxt = jnp.where(is_last_blk, 0.0, u_next[0:1, :])
    slabs[G_LEN] = jnp.where(sub == N_CHUNK - 1, edge(nxt), pltpu.roll(_load_slab(ut_ref, 0), N_CHUNK - 1, axis=0))
    uprev_ref[...] = u[TM - SUBLANES:TM, :]

    def slab(g):
        if g not in slabs:
            slabs[g] = _load_slab(ut_ref, g)
        return slabs[g]

    cw = [cw_ref[i:i + 1, :] for i in range(CONV_W)]
    cb = cb_ref[...]
    for g in range(G_LEN):
        hu_ref[_slab_rows(g), :] = (slab(g - 2) * cw[0] + slab(g - 1) * cw[1] + slab(g) * cw[2]
                                    + slab(g + 1) * cw[3] + cb)
    ub_ref[...] = hu_ref[...].astype(_bf16)

    for di in range(2):
        pre = _gate_dots(ub_ref, wg_ref.at[di])
        pre_ref[2 * di] = pre[0]
        pre_ref[2 * di + 1] = pre[1]
    proj(q_ref, 0, lambda y, cols: y * (HEAD_DIM ** -0.5 * LOG2_E))
    proj(k_ref, W_ATTN, lambda y, cols: y)
    proj(v_ref, 2 * W_ATTN, lambda y, cols: y)
    proj(za_ref, 3 * W_ATTN, lambda y, cols: _silu(y))
    proj(zb_ref, u_off + W_LRU, lambda y, cols: _silu(y))
    proj(ga_ref, g_off, lambda y, cols: _sigmoid(y + bm_ref[0:1, cols]))
    proj(gb_ref, g_off + D_MODEL, lambda y, cols: _sigmoid(y + bm_ref[1:2, cols]))

    neg_lam = -lam_ref[...]
    softplus = jnp.maximum(neg_lam, 0.0) + jnp.log(1.0 + jnp.exp(-jnp.abs(neg_lam)))
    coef = (-0.5 * C_LRU * LOG2_E) * softplus

    first = jnp.logical_and(is_first_blk, sub == 0)
    _gates((pre_ref[0], pre_ref[1]), hu_ref, bg_ref.at[0], coef[0:1], first, 0, a_ref, bx_ref)
    h_end, p_end = _chunk_scan(a_ref, bx_ref, hl_ref, pl_ref, range(G_LEN))
    c = jnp.where(sub == 0, edge(carry_ref[N_CHUNK - 1:N_CHUNK, :]), 0.0)
    for kk in range(1, N_CHUNK):
        c = jnp.where(sub == kk, pltpu.roll(p_end * c + h_end, 1, axis=0), c)
    carry_ref[...] = p_end * c + h_end
    _unstage_slabs(lambda g: pl_ref[_slab_rows(g), :] * c + hl_ref[_slab_rows(g), :], ot_f, range(G_LEN))
    _copy_time_major(ot_f, hf_ref, range(N_CHUNK))

    last = jnp.logical_and(is_last_blk, sub == N_CHUNK - 1)
    _gates((pre_ref[2], pre_ref[3]), hu_ref, bg_ref.at[1], coef[1:2], last, G_LEN - 1, a_ref, bx_ref)
    h_end, p_end = _chunk_scan(a_ref, bx_ref, hl_ref, pl_ref, range(G_LEN - 1, -1, -1))
    cz = jnp.zeros((N_CHUNK, W_LRU), _f32)
    d = jnp.where(sub == N_CHUNK - 1, 1.0, 0.0)
    for kk in range(N_CHUNK - 2, -1, -1):
        cz = jnp.where(sub == kk, pltpu.roll(p_end * cz + h_end, N_CHUNK - 1, axis=0), cz)
        d = jnp.where(sub == kk, pltpu.roll(p_end * d, N_CHUNK - 1, axis=0), d)
    _unstage_slabs(lambda g: pl_ref[_slab_rows(g), :] * cz + hl_ref[_slab_rows(g), :], ot_h, range(G_LEN))
    _unstage_slabs(lambda g: pl_ref[_slab_rows(g), :] * d, ot_p, range(G_LEN))
    _copy_time_major(ot_h, hb0_ref, range(N_CHUNK))
    _copy_time_major(ot_p, pb_ref, range(N_CHUNK))


def _block_diag_halves(w):
    per_half = N_BLOCKS // 2
    eye = jnp.eye(per_half, dtype=w.dtype)
    w = w.reshape(w.shape[:-3] + (2, per_half, BLOCK_W, BLOCK_W))
    full = jnp.einsum('...hbde,bc->...hbdce', w, eye)
    return full.reshape(w.shape[:-3] + (per_half * BLOCK_W, per_half * BLOCK_W)).astype(_bf16)


def _front(x2, seq_len, ln_g, ln_b, w_in_bf16, b_merge, conv_w, conv_b, gate_w, gate_b, lam, apply_ln):
    t = x2.shape[0]
    n = t // TM
    per = TM // SUBLANES
    row = lambda i: (i, 0)
    full = lambda a: pl.BlockSpec(a.shape, lambda i: (0,) * a.ndim)
    wg = _block_diag_halves(gate_w)
    bg = (0.5 * gate_b).reshape(2, 2, 1, W_LRU)
    conv_w = 0.5 * conv_w
    cb2 = (0.5 * conv_b).reshape(1, W_LRU)
    widths = (W_ATTN, W_ATTN, W_ATTN, W_ATTN, W_LRU, D_MODEL, D_MODEL, W_LRU, W_LRU, W_LRU)
    out_shape = [jax.ShapeDtypeStruct((t, w), _bf16) for w in widths]
    out_specs = [pl.BlockSpec((TM, w), row) for w in widths]
    if apply_ln:
        out_shape = [jax.ShapeDtypeStruct((t, D_MODEL), _f32)] + out_shape
        out_specs = [pl.BlockSpec((TM, D_MODEL), row)] + out_specs
    stage = pltpu.VMEM((N_LG, N_CHUNK * PITCH, LANES), _f32)
    block_f32 = pltpu.VMEM((TM, W_LRU), _f32)
    return pl.pallas_call(
        functools.partial(_front_kernel, apply_ln=apply_ln, blocks_per_seq=seq_len // TM),
        grid=(n,),
        in_specs=[pl.BlockSpec((TM, D_MODEL), row),
                  pl.BlockSpec((SUBLANES, D_MODEL), lambda i: (jnp.minimum((i + 1) * per, n * per - 1), 0)),
                  full(ln_g), full(ln_b), full(w_in_bf16), full(b_merge), full(conv_w), full(cb2),
                  full(wg), full(bg), full(lam)],
        out_specs=out_specs,
        out_shape=out_shape,
        scratch_shapes=[pltpu.VMEM((SUBLANES, W_LRU), _f32), pltpu.VMEM((N_CHUNK, W_LRU), _f32),
                        pltpu.VMEM((TM, D_MODEL), _bf16), block_f32, pltpu.VMEM((TM, W_LRU), _bf16),
                        block_f32, block_f32, block_f32, block_f32, stage, stage, stage, stage,
                        pltpu.VMEM((4, TM, W_LRU), _f32)],
        compiler_params=pltpu.CompilerParams(dimension_semantics=("arbitrary",),
                                             vmem_limit_bytes=VMEM_LIMIT),
        name="front",
    )(x2, x2, ln_g, ln_b, w_in_bf16, b_merge, conv_w, cb2, wg, bg, lam)


def _attn_value_table(rpb):
    qc = np.arange(GRID_W)[:, None]
    kc = np.arange(GRID_W)[None, :]
    cs = np.clip(qc - WIN_W // 2, 0, GRID_W - WIN_W)
    ok = (kc >= cs) & (kc < cs + WIN_W)
    pad = GRID_W - WIN_W
    padded = jnp.pad(rpb.astype(_f32), ((0, 0), (0, 0), (pad, pad)))
    skew = jnp.tile(padded, (1, 1, GRID_W + 1))[:, :, :GRID_W * 2 * GRID_W]
    skew = skew.reshape(skew.shape[:2] + (GRID_W, 2 * GRID_W))
    t = skew[:, :, ::-1, :GRID_W]
    t = jnp.where(ok, t * LOG2_E, NEG)
    return jnp.concatenate([t[:, :-1], t[:, 1:]], axis=-1)


def _attn_row_mask(rows):
    r_first = (0, Q_ROWS, rows - Q_ROWS)
    m = np.full((3, Q_ROWS // SUB_ROWS, SUB_ROWS, SUBLANES, KEY_ROWS * GRID_W), NEG, np.float32)
    for v, r0 in enumerate(r_first):
        for half in range(Q_ROWS // SUB_ROWS):
            for qr in range(SUB_ROWS):
                r = r0 + half * SUB_ROWS + qr
                rs = min(max(r - WIN_H // 2, 0), rows - WIN_H)
                for kr in range(KEY_ROWS):
                    key_r = r0 + half * SUB_ROWS - WIN_H // 2 + kr
                    if rs <= key_r < rs + WIN_H:
                        m[v, half, qr, :, kr * GRID_W:(kr + 1) * GRID_W] = 0.0
    return m


def _attn_kernel(q_ref, *refs):
    k_refs, v_refs = refs[:N_KV_BLKS], refs[N_KV_BLKS:2 * N_KV_BLKS]
    z_ref, val_ref, rowm_ref, o_ref = refs[2 * N_KV_BLKS:]
    m_q = SUB_ROWS * GRID_W
    n_kb = KEY_ROWS // KV_BLK_ROWS
    blk_keys = KV_BLK_ROWS * GRID_W
    lane = lax.broadcasted_iota(jnp.int32, (m_q, LANES), 1)
    even = lane < HEAD_DIM
    for half in range(Q_ROWS // SUB_ROWS):
        rows = slice(half * m_q, (half + 1) * m_q)
        for pair in range(N_HEADS // 2):
            cols = slice(pair * LANES, (pair + 1) * LANES)
            qp = q_ref[rows, cols]
            zero = jnp.zeros_like(qp)
            q2 = jnp.concatenate([jnp.where(even, qp, zero), jnp.where(even, zero, qp)], axis=0)
            sk = [lax.dot_general(q2, k_refs[half + j][:, cols], (((1,), (1,)), ((), ())),
                                  preferred_element_type=_f32) for j in range(n_kb)]
            s = jnp.concatenate(sk, axis=1)
            blocks = []
            for parity in range(2):
                for qr in range(SUB_ROWS):
                    val = jnp.concatenate(
                        [val_ref[2 * pair + parity, 2 * j - qr + WIN_H // 2 - 1]
                         for j in range(KEY_ROWS // 2)], axis=1)
                    r0 = (parity * SUB_ROWS + qr) * GRID_W
                    blk = s[r0:r0 + GRID_W] + val
                    blk = blk.reshape(GRID_W // SUBLANES, SUBLANES, -1) + rowm_ref[half, qr][None]
                    blocks.append(blk.reshape(GRID_W, -1))
            s = jnp.concatenate(blocks, axis=0)
            m = jnp.max(s, axis=-1, keepdims=True)
            e = jnp.exp2(s - m)
            l = jnp.sum(e, axis=-1, keepdims=True)
            eb = e.astype(_bf16)
            o2 = sum(_dot(eb[:, j * blk_keys:(j + 1) * blk_keys], v_refs[half + j][:, cols])
                     for j in range(n_kb))
            o = jnp.where(even, o2[:m_q] / l[:m_q], o2[m_q:] / l[m_q:])
            o_ref[rows, cols] = (o * z_ref[rows, cols].astype(_f32)).astype(o_ref.dtype)


def _attention(q, k, v, z_a, val, rowm):
    b, s, _ = q.shape
    rows = s // GRID_W
    n_blk = rows // Q_ROWS
    n_kv = rows // KV_BLK_ROWS
    tq = Q_ROWS * GRID_W
    tkv = KV_BLK_ROWS * GRID_W

    def kv_spec(j):
        first = -(WIN_H // 2) // KV_BLK_ROWS
        return pl.BlockSpec(
            (None, tkv, W_ATTN),
            lambda bi, i: (bi, jnp.clip(i * (Q_ROWS // KV_BLK_ROWS) + first + j, 0, n_kv - 1), 0))

    def case(i):
        return jnp.where(i == 0, 0, jnp.where(i == n_blk - 1, 2, 1))

    blk = pl.BlockSpec((None, tq, W_ATTN), lambda bi, i: (bi, i, 0))
    return pl.pallas_call(
        _attn_kernel,
        grid=(b, n_blk),
        in_specs=[blk] + [kv_spec(j) for j in range(N_KV_BLKS)] * 2 + [
            blk,
            pl.BlockSpec(val.shape, lambda bi, i: (0, 0, 0, 0)),
            pl.BlockSpec((None,) + rowm.shape[1:], lambda bi, i: (case(i), 0, 0, 0, 0))],
        out_specs=blk,
        out_shape=jax.ShapeDtypeStruct((b, s, W_ATTN), _bf16),
        compiler_params=pltpu.CompilerParams(dimension_semantics=("arbitrary", "arbitrary"),
                                             vmem_limit_bytes=VMEM_LIMIT),
        name="attn",
    )(q, *([k] * N_KV_BLKS), *([v] * N_KV_BLKS), z_a, val, rowm)


def _back_kernel(x_ref, ya_ref, hf_ref, hb0_ref, pb_ref, zb_ref, ga_ref, gb_ref, wba_ref, wbb_ref,
                 wo_ref, g_ref, b_ref, o_ref, carry_ref, *, alpha, blocks_per_seq):
    i = pl.program_id(0)

    @pl.when(i % blocks_per_seq == 0)
    def _():
        carry_ref[...] = jnp.zeros_like(carry_ref)

    carry = carry_ref[0:1, :]
    parts = []
    for sb in range(TB // TM - 1, -1, -1):
        rows = slice(sb * TM, (sb + 1) * TM)
        part = hb0_ref[rows, :].astype(_f32) + pb_ref[rows, :].astype(_f32) * carry
        carry = part[0:1, :]
        parts.insert(0, part)
    h_b = jnp.concatenate(parts, axis=0)
    carry_ref[...] = h_b[0:SUBLANES, :]
    yb = ((hf_ref[...].astype(_f32) + h_b) * zb_ref[...].astype(_f32)).astype(_bf16)
    pa = _dot(ya_ref[...], wba_ref[...])
    pb = _dot(yb, wbb_ref[...])
    m = ga_ref[...].astype(_f32) * pa + gb_ref[...].astype(_f32) * pb
    out = _dot(m.astype(_bf16), wo_ref[...])
    o_ref[...] = _layer_norm(alpha * x_ref[...] + out, g_ref[...], b_ref[...])


def _back(x2, seq_len, y_a, h_f, h_b0, p_b, z_b, g_a, g_b, w_ba, w_bb, w_out, ln_g, ln_b, alpha):
    t = x2.shape[0]
    n = t // TB
    rev = lambda i: (n - 1 - i, 0)
    const = lambda i: (0, 0)
    rows = lambda w: pl.BlockSpec((TB, w), rev)
    full = lambda a: pl.BlockSpec(a.shape, const)
    return pl.pallas_call(
        functools.partial(_back_kernel, alpha=alpha, blocks_per_seq=seq_len // TB),
        grid=(n,),
        in_specs=[rows(D_MODEL), rows(W_ATTN), rows(W_LRU), rows(W_LRU), rows(W_LRU), rows(W_LRU),
                  rows(D_MODEL), rows(D_MODEL), full(w_ba), full(w_bb), full(w_out), full(ln_g), full(ln_b)],
        out_specs=rows(D_MODEL),
        out_shape=jax.ShapeDtypeStruct((t, D_MODEL), _f32),
        scratch_shapes=[pltpu.VMEM((SUBLANES, W_LRU), _f32)],
        compiler_params=pltpu.CompilerParams(dimension_semantics=("arbitrary",),
                                             vmem_limit_bytes=VMEM_LIMIT),
        name="back",
    )(x2, y_a, h_f, h_b0, p_b, z_b, g_a, g_b, w_ba, w_bb, w_out, ln_g, ln_b)


def kernel(x, emb_ln_g, emb_ln_b, w_in, rpb, conv_w, conv_b, lru_gate_w, lru_gate_b, lru_lambda,
           w_branch_attn, w_branch_lru, b_merge, w_out, ln_g, ln_b):
    b, s, d = x.shape
    depth = w_in.shape[0]
    assert d == D_MODEL and s % (Q_ROWS * GRID_W) == 0 and s % TM == 0 and s % TB == 0 and TB % TM == 0
    alpha = (2 * depth) ** 0.25
    t = b * s
    h = x.reshape(t, d)
    row2 = lambda a: a.reshape(1, -1)
    rowm = _attn_row_mask(s // GRID_W)
    for l in range(depth):
        outs = _front(h, s, row2(emb_ln_g), row2(emb_ln_b), w_in[l].astype(_bf16), b_merge[l], conv_w[l],
                      conv_b[l], lru_gate_w[l], lru_gate_b[l], lru_lambda[l], apply_ln=(l == 0))
        if l == 0:
            h, outs = outs[0], outs[1:]
        q, k, v, z_a, z_b, g_a, g_b, h_f, h_b0, p_b = outs
        seq = lambda a: a.reshape(b, s, a.shape[-1])
        y_a = _attention(seq(q), seq(k), seq(v), seq(z_a), _attn_value_table(rpb[l]), rowm)
        h = _back(h, s, y_a.reshape(t, W_ATTN), h_f, h_b0, p_b, z_b, g_a, g_b,
                  w_branch_attn[l].astype(_bf16), w_branch_lru[l].astype(_bf16),
                  w_out[l].astype(_bf16), row2(ln_g[l]), row2(ln_b[l]), alpha)
    return h.reshape(b, s, d)
```

```python
import functools

import numpy as np
import jax
import jax.numpy as jnp
from jax import lax
from jax.experimental import pallas as pl
from jax.experimental.pallas import tpu as pltpu

D_MODEL = 1024
GRID_W = 64
N_HEADS = 8
HEAD_DIM = 64
W_ATTN = N_HEADS * HEAD_DIM
WIN_H = 8
WIN_W = 16
W_LRU = 512
N_BLOCKS = 8
BLOCK_W = 64
CONV_W = 4
C_LRU = 8.0
LN_EPS = 1e-5
NEG = -1e30
TINY = 1e-30
LOG2_E = 1.4426950408889634

LANES = 128
SUBLANES = 8
VMEM_LIMIT = 56 * 1024 * 1024

TM = 512
TB = 1024
BACK_ROWS = 256
Q_ROWS = 16
SUB_ROWS = 4
KEY_ROWS = SUB_ROWS + WIN_H
KV_BLK_ROWS = 4
N_KV_BLKS = (Q_ROWS + WIN_H) // KV_BLK_ROWS
N_CHUNK = SUBLANES
G_LEN = TM // N_CHUNK
PITCH = G_LEN + 8
N_LG = W_LRU // LANES
PROJ_COLS = 256

_f32 = jnp.float32
_bf16 = jnp.bfloat16


def _dot(a, b):
    return jnp.dot(a, b, preferred_element_type=_f32)


def _layer_norm(x, g, b):
    mu = jnp.mean(x, axis=-1, keepdims=True)
    xc = x - mu
    var = jnp.mean(xc * xc, axis=-1, keepdims=True)
    return xc * lax.rsqrt(var + LN_EPS) * g + b


def _sigmoid(x):
    return 0.5 + 0.5 * jnp.tanh(0.5 * x)


def _silu(x):
    hx = 0.5 * x
    return hx + hx * jnp.tanh(hx)


def _stage_time_major(u, ut_ref):
    for ch in range(N_CHUNK):
        for lg in range(N_LG):
            ut_ref[lg, ch * PITCH:ch * PITCH + G_LEN, :] = u[ch * G_LEN:(ch + 1) * G_LEN,
                                                            lg * LANES:(lg + 1) * LANES]


def _load_slab(ut_ref, g):
    return jnp.concatenate([ut_ref[lg, pl.ds(g, N_CHUNK, stride=PITCH), :] for lg in range(N_LG)], axis=1)


def _slab_rows(g):
    return slice(g * N_CHUNK, (g + 1) * N_CHUNK)


def _unstage_slabs(slab_fn, ot_ref, g_range):
    for g in g_range:
        slab = slab_fn(g)
        for lg in range(N_LG):
            ot_ref[lg, pl.ds(g, N_CHUNK, stride=PITCH), :] = slab[:, lg * LANES:(lg + 1) * LANES]


def _copy_time_major(ot_ref, o_ref, ch_range):
    for ch in ch_range:
        for lg in range(N_LG):
            o_ref[ch * G_LEN:(ch + 1) * G_LEN, lg * LANES:(lg + 1) * LANES] = (
                ot_ref[lg, ch * PITCH:ch * PITCH + G_LEN, :].astype(o_ref.dtype))


def _gate_dots(ub_ref, wg_ref):
    half = W_LRU // 2
    pre = []
    for gi in range(2):
        cols = [_dot(ub_ref[:, hf * half:(hf + 1) * half], wg_ref[gi, hf]) for hf in range(2)]
        pre.append(jnp.concatenate(cols, axis=1))
    return pre


def _gates(pre, hu_ref, bg_ref, coef, first, g_first, a_ref, bx_ref):
    t_r = jnp.tanh(pre[0] + bg_ref[0])
    t_i = jnp.tanh(pre[1] + bg_ref[1])
    a = jnp.exp2(coef + coef * t_r)
    one_m_a2 = (1.0 - a) * (1.0 + a)
    mult = one_m_a2 * lax.rsqrt(jnp.maximum(one_m_a2, TINY))
    hu = hu_ref[...]
    iu = hu + hu * t_i
    bx = mult * iu
    a_ref[...] = a
    bx_ref[...] = bx
    fix = _slab_rows(g_first)
    bx_ref[fix, :] = jnp.where(first, iu[fix], bx[fix])


def _chunk_scan(a_ref, bx_ref, hl_ref, pl_ref, g_range):
    h = jnp.zeros((N_CHUNK, W_LRU), _f32)
    p = jnp.ones((N_CHUNK, W_LRU), _f32)
    for g in g_range:
        rows = _slab_rows(g)
        ag = a_ref[rows, :]
        h = ag * h + bx_ref[rows, :]
        p = ag * p
        hl_ref[rows, :] = h
        pl_ref[rows, :] = p
    return h, p


def _front_kernel(x_ref, xn_ref, g_ref, b_ref, w_ref, bm_ref, cw_ref, cb_ref, wg_ref, bg_ref, lam_ref,
                  *refs, apply_ln, blocks_per_seq):
    n_out = 11 if apply_ln else 10
    out_refs = refs[:n_out]
    (uprev_ref, carry_ref, xb_ref, hu_ref, ub_ref, a_ref, bx_ref, hl_ref, pl_ref,
     ut_ref, ot_f, ot_h, ot_p, pre_ref) = refs[n_out:]
    if apply_ln:
        h_ref, out_refs = out_refs[0], out_refs[1:]
    q_ref, k_ref, v_ref, za_ref, zb_ref, ga_ref, gb_ref, hf_ref, hb0_ref, pb_ref = out_refs
    j = pl.program_id(0)
    is_first_blk = j % blocks_per_seq == 0
    is_last_blk = j % blocks_per_seq == blocks_per_seq - 1
    u_off = 4 * W_ATTN
    g_off = u_off + 2 * W_LRU
    sub = lax.broadcasted_iota(jnp.int32, (N_CHUNK, W_LRU), 0)

    def edge(row):
        return jnp.broadcast_to(row, (N_CHUNK, W_LRU))

    def proj(o_ref, off, post):
        for c0 in range(0, o_ref.shape[-1], PROJ_COLS):
            y = _dot(xb_ref[...], w_ref[:, off + c0:off + c0 + PROJ_COLS])
            o_ref[:, c0:c0 + PROJ_COLS] = post(y, slice(c0, c0 + PROJ_COLS)).astype(o_ref.dtype)

    @pl.when(is_first_blk)
    def _():
        uprev_ref[...] = jnp.zeros_like(uprev_ref)
        carry_ref[...] = jnp.zeros_like(carry_ref)

    x = x_ref[...]
    xn = xn_ref[...]
    if apply_ln:
        x = _layer_norm(x, g_ref[...], b_ref[...])
        xn = _layer_norm(xn, g_ref[...], b_ref[...])
        h_ref[...] = x
    xb_ref[...] = x.astype(_bf16)
    u = _dot(xb_ref[...], w_ref[:, u_off:u_off + W_LRU])
    u_next = _dot(xn.astype(_bf16), w_ref[:, u_off:u_off + W_LRU])
    _stage_time_major(u, ut_ref)

    slabs = {}
    slabs[-2] = jnp.where(sub == 0, edge(uprev_ref[SUBLANES - 2:SUBLANES - 1, :]),
                          pltpu.roll(_load_slab(ut_ref, G_LEN - 2), 1, axis=0))
    slabs[-1] = jnp.where(sub == 0, edge(uprev_ref[SUBLANES - 1:SUBLANES, :]),
                          pltpu.roll(_load_slab(ut_ref, G_LEN - 1), 1, axis=0))
    nxt = jnp.where(is_last_blk, 0.0, u_next[0:1, :])
    slabs[G_LEN] = jnp.where(sub == N_CHUNK - 1, edge(nxt), pltpu.roll(_load_slab(ut_ref, 0), N_CHUNK - 1, axis=0))
    uprev_ref[...] = u[TM - SUBLANES:TM, :]

    def slab(g):
        if g not in slabs:
            slabs[g] = _load_slab(ut_ref, g)
        return slabs[g]

    cw = [cw_ref[i:i + 1, :] for i in range(CONV_W)]
    cb = cb_ref[...]
    for g in range(G_LEN):
        hu_ref[_slab_rows(g), :] = (slab(g - 2) * cw[0] + slab(g - 1) * cw[1] + slab(g) * cw[2]
                                    + slab(g + 1) * cw[3] + cb)
    ub_ref[...] = hu_ref[...].astype(_bf16)
    proj(q_ref, 0, lambda y, cols: y * (HEAD_DIM ** -0.5 * LOG2_E))
    proj(k_ref, W_ATTN, lambda y, cols: y)

    for di in range(2):
        pre = _gate_dots(ub_ref, wg_ref.at[di])
        pre_ref[2 * di] = pre[0]
        pre_ref[2 * di + 1] = pre[1]

    neg_lam = -lam_ref[...]
    softplus = jnp.maximum(neg_lam, 0.0) + jnp.log(1.0 + jnp.exp(-jnp.abs(neg_lam)))
    coef = (-0.5 * C_LRU * LOG2_E) * softplus

    first = jnp.logical_and(is_first_blk, sub == 0)
    _gates((pre_ref[0], pre_ref[1]), hu_ref, bg_ref.at[0], coef[0:1], first, 0, a_ref, bx_ref)
    h_end, p_end = _chunk_scan(a_ref, bx_ref, hl_ref, pl_ref, range(G_LEN))
    c = jnp.where(sub == 0, edge(carry_ref[N_CHUNK - 1:N_CHUNK, :]), 0.0)
    for kk in range(1, N_CHUNK):
        c = jnp.where(sub == kk, pltpu.roll(p_end * c + h_end, 1, axis=0), c)
    carry_ref[...] = p_end * c + h_end
    _unstage_slabs(lambda g: pl_ref[_slab_rows(g), :] * c + hl_ref[_slab_rows(g), :], ot_f, range(G_LEN))
    _copy_time_major(ot_f, hf_ref, range(N_CHUNK))
    proj(v_ref, 2 * W_ATTN, lambda y, cols: y)
    proj(za_ref, 3 * W_ATTN, lambda y, cols: _silu(y))
    proj(zb_ref, u_off + W_LRU, lambda y, cols: _silu(y))

    last = jnp.logical_and(is_last_blk, sub == N_CHUNK - 1)
    _gates((pre_ref[2], pre_ref[3]), hu_ref, bg_ref.at[1], coef[1:2], last, G_LEN - 1, a_ref, bx_ref)
    h_end, p_end = _chunk_scan(a_ref, bx_ref, hl_ref, pl_ref, range(G_LEN - 1, -1, -1))
    cz = jnp.zeros((N_CHUNK, W_LRU), _f32)
    d = jnp.where(sub == N_CHUNK - 1, 1.0, 0.0)
    for kk in range(N_CHUNK - 2, -1, -1):
        cz = jnp.where(sub == kk, pltpu.roll(p_end * cz + h_end, N_CHUNK - 1, axis=0), cz)
        d = jnp.where(sub == kk, pltpu.roll(p_end * d, N_CHUNK - 1, axis=0), d)
    _unstage_slabs(lambda g: pl_ref[_slab_rows(g), :] * cz + hl_ref[_slab_rows(g), :], ot_h, range(G_LEN))
    _unstage_slabs(lambda g: pl_ref[_slab_rows(g), :] * d, ot_p, range(G_LEN))
    _copy_time_major(ot_h, hb0_ref, range(N_CHUNK))
    _copy_time_major(ot_p, pb_ref, range(N_CHUNK))
    proj(ga_ref, g_off, lambda y, cols: _sigmoid(y + bm_ref[0:1, cols]))
    proj(gb_ref, g_off + D_MODEL, lambda y, cols: _sigmoid(y + bm_ref[1:2, cols]))


def _block_diag_halves(w):
    per_half = N_BLOCKS // 2
    eye = jnp.eye(per_half, dtype=w.dtype)
    w = w.reshape(w.shape[:-3] + (2, per_half, BLOCK_W, BLOCK_W))
    full = jnp.einsum('...hbde,bc->...hbdce', w, eye)
    return full.reshape(w.shape[:-3] + (per_half * BLOCK_W, per_half * BLOCK_W)).astype(_bf16)


def _front(x2, seq_len, ln_g, ln_b, w_in_bf16, b_merge, conv_w, conv_b, gate_w, gate_b, lam, apply_ln):
    t = x2.shape[0]
    n = t // TM
    per = TM // SUBLANES
    row = lambda i: (i, 0)
    full = lambda a: pl.BlockSpec(a.shape, lambda i: (0,) * a.ndim)
    wg = _block_diag_halves(gate_w)
    bg = (0.5 * gate_b).reshape(2, 2, 1, W_LRU)
    conv_w = 0.5 * conv_w
    cb2 = (0.5 * conv_b).reshape(1, W_LRU)
    widths = (W_ATTN, W_ATTN, W_ATTN, W_ATTN, W_LRU, D_MODEL, D_MODEL, W_LRU, W_LRU, W_LRU)
    out_shape = [jax.ShapeDtypeStruct((t, w), _bf16) for w in widths]
    out_specs = [pl.BlockSpec((TM, w), row) for w in widths]
    if apply_ln:
        out_shape = [jax.ShapeDtypeStruct((t, D_MODEL), _f32)] + out_shape
        out_specs = [pl.BlockSpec((TM, D_MODEL), row)] + out_specs
    stage = pltpu.VMEM((N_LG, N_CHUNK * PITCH, LANES), _f32)
    block_f32 = pltpu.VMEM((TM, W_LRU), _f32)
    return pl.pallas_call(
        functools.partial(_front_kernel, apply_ln=apply_ln, blocks_per_seq=seq_len // TM),
        grid=(n,),
        in_specs=[pl.BlockSpec((TM, D_MODEL), row),
                  pl.BlockSpec((SUBLANES, D_MODEL), lambda i: (jnp.minimum((i + 1) * per, n * per - 1), 0)),
                  full(ln_g), full(ln_b), full(w_in_bf16), full(b_merge), full(conv_w), full(cb2),
                  full(wg), full(bg), full(lam)],
        out_specs=out_specs,
        out_shape=out_shape,
        scratch_shapes=[pltpu.VMEM((SUBLANES, W_LRU), _f32), pltpu.VMEM((N_CHUNK, W_LRU), _f32),
                        pltpu.VMEM((TM, D_MODEL), _bf16), block_f32, pltpu.VMEM((TM, W_LRU), _bf16),
                        block_f32, block_f32, block_f32, block_f32, stage, stage, stage, stage,
                        pltpu.VMEM((4, TM, W_LRU), _f32)],
        compiler_params=pltpu.CompilerParams(dimension_semantics=("arbitrary",),
                                             vmem_limit_bytes=VMEM_LIMIT),
        name="front",
    )(x2, x2, ln_g, ln_b, w_in_bf16, b_merge, conv_w, cb2, wg, bg, lam)


def _attn_value_table(rpb):
    qc = np.arange(GRID_W)[:, None]
    kc = np.arange(GRID_W)[None, :]
    cs = np.clip(qc - WIN_W // 2, 0, GRID_W - WIN_W)
    ok = (kc >= cs) & (kc < cs + WIN_W)
    pad = GRID_W - WIN_W
    padded = jnp.pad(rpb.astype(_f32), ((0, 0), (0, 0), (pad, pad)))
    skew = jnp.tile(padded, (1, 1, GRID_W + 1))[:, :, :GRID_W * 2 * GRID_W]
    skew = skew.reshape(skew.shape[:2] + (GRID_W, 2 * GRID_W))
    t = skew[:, :, ::-1, :GRID_W]
    t = jnp.where(ok, t * LOG2_E, NEG)
    return jnp.concatenate([t[:, :-1], t[:, 1:]], axis=-1)


def _attn_row_mask(rows):
    r_first = (0, Q_ROWS, rows - Q_ROWS)
    m = np.full((3, Q_ROWS // SUB_ROWS, SUB_ROWS, SUBLANES, KEY_ROWS * GRID_W), NEG, np.float32)
    for v, r0 in enumerate(r_first):
        for half in range(Q_ROWS // SUB_ROWS):
            for qr in range(SUB_ROWS):
                r = r0 + half * SUB_ROWS + qr
                rs = min(max(r - WIN_H // 2, 0), rows - WIN_H)
                for kr in range(KEY_ROWS):
                    key_r = r0 + half * SUB_ROWS - WIN_H // 2 + kr
                    if rs <= key_r < rs + WIN_H:
                        m[v, half, qr, :, kr * GRID_W:(kr + 1) * GRID_W] = 0.0
    return m


def _attn_kernel(q_ref, *refs):
    k_refs, v_refs = refs[:N_KV_BLKS], refs[N_KV_BLKS:2 * N_KV_BLKS]
    z_ref, val_ref, rowm_ref, o_ref = refs[2 * N_KV_BLKS:]
    m_q = SUB_ROWS * GRID_W
    n_kb = KEY_ROWS // KV_BLK_ROWS
    blk_keys = KV_BLK_ROWS * GRID_W
    lane = lax.broadcasted_iota(jnp.int32, (m_q, LANES), 1)
    even = lane < HEAD_DIM
    for half in range(Q_ROWS // SUB_ROWS):
        rows = slice(half * m_q, (half + 1) * m_q)
        for pair in range(N_HEADS // 2):
            cols = slice(pair * LANES, (pair + 1) * LANES)
            qp = q_ref[rows, cols]
            zero = jnp.zeros_like(qp)
            q2 = jnp.concatenate([jnp.where(even, qp, zero), jnp.where(even, zero, qp)], axis=0)
            sk = [lax.dot_general(q2, k_refs[half + j][:, cols], (((1,), (1,)), ((), ())),
                                  preferred_element_type=_f32) for j in range(n_kb)]
            s = jnp.concatenate(sk, axis=1)
            blocks = []
            for parity in range(2):
                for qr in range(SUB_ROWS):
                    val = jnp.concatenate(
                        [val_ref[2 * pair + parity, 2 * j - qr + WIN_H // 2 - 1]
                         for j in range(KEY_ROWS // 2)], axis=1)
                    r0 = (parity * SUB_ROWS + qr) * GRID_W
                    blk = s[r0:r0 + GRID_W] + val
                    blk = blk.reshape(GRID_W // SUBLANES, SUBLANES, -1) + rowm_ref[half, qr][None]
                    blocks.append(blk.reshape(GRID_W, -1))
            s = jnp.concatenate(blocks, axis=0)
            m = jnp.max(s, axis=-1, keepdims=True)
            e = jnp.exp2(s - m)
            l = jnp.sum(e, axis=-1, keepdims=True)
            eb = e.astype(_bf16)
            o2 = sum(_dot(eb[:, j * blk_keys:(j + 1) * blk_keys], v_refs[half + j][:, cols])
                     for j in range(n_kb))
            o = jnp.where(even, o2[:m_q] / l[:m_q], o2[m_q:] / l[m_q:])
            o_ref[rows, cols] = (o * z_ref[rows, cols].astype(_f32)).astype(o_ref.dtype)


def _attention(q, k, v, z_a, val, rowm):
    b, s, _ = q.shape
    rows = s // GRID_W
    n_blk = rows // Q_ROWS
    n_kv = rows // KV_BLK_ROWS
    tq = Q_ROWS * GRID_W
    tkv = KV_BLK_ROWS * GRID_W

    def kv_spec(j):
        first = -(WIN_H // 2) // KV_BLK_ROWS
        return pl.BlockSpec(
            (None, tkv, W_ATTN),
            lambda bi, i: (bi, jnp.clip(i * (Q_ROWS // KV_BLK_ROWS) + first + j, 0, n_kv - 1), 0))

    def case(i):
        return jnp.where(i == 0, 0, jnp.where(i == n_blk - 1, 2, 1))

    blk = pl.BlockSpec((None, tq, W_ATTN), lambda bi, i: (bi, i, 0))
    return pl.pallas_call(
        _attn_kernel,
        grid=(b, n_blk),
        in_specs=[blk] + [kv_spec(j) for j in range(N_KV_BLKS)] * 2 + [
            blk,
            pl.BlockSpec(val.shape, lambda bi, i: (0, 0, 0, 0)),
            pl.BlockSpec((None,) + rowm.shape[1:], lambda bi, i: (case(i), 0, 0, 0, 0))],
        out_specs=blk,
        out_shape=jax.ShapeDtypeStruct((b, s, W_ATTN), _bf16),
        compiler_params=pltpu.CompilerParams(dimension_semantics=("arbitrary", "arbitrary"),
                                             vmem_limit_bytes=VMEM_LIMIT),
        name="attn",
    )(q, *([k] * N_KV_BLKS), *([v] * N_KV_BLKS), z_a, val, rowm)


def _back_kernel(x_ref, ya_ref, hf_ref, hb0_ref, pb_ref, zb_ref, ga_ref, gb_ref, wba_ref, wbb_ref,
                 wo_ref, g_ref, b_ref, o_ref, carry_ref, cin_ref, out_ref, *, alpha, blocks_per_seq):
    i = pl.program_id(0)

    @pl.when(i % blocks_per_seq == 0)
    def _():
        carry_ref[...] = jnp.zeros_like(carry_ref)

    carry = carry_ref[0:1, :]
    for sb in range(TB // TM - 1, -1, -1):
        cin_ref[sb:sb + 1, :] = carry
        r0 = sb * TM
        carry = hb0_ref[r0:r0 + 1, :].astype(_f32) + pb_ref[r0:r0 + 1, :].astype(_f32) * carry
    carry_ref[0:1, :] = carry

    def rows_of(c):
        return pl.ds(pl.multiple_of(c * BACK_ROWS, BACK_ROWS), BACK_ROWS)

    def matmuls(c):
        rows = rows_of(c)
        cin = cin_ref[pl.ds(c // (TM // BACK_ROWS), 1), :]
        h_b = hb0_ref[rows, :].astype(_f32) + pb_ref[rows, :].astype(_f32) * cin
        yb = ((hf_ref[rows, :].astype(_f32) + h_b) * zb_ref[rows, :].astype(_f32)).astype(_bf16)
        pa = _dot(ya_ref[rows, :], wba_ref[...])
        pb = _dot(yb, wbb_ref[...])
        m = ga_ref[rows, :].astype(_f32) * pa + gb_ref[rows, :].astype(_f32) * pb
        out_ref[c] = _dot(m.astype(_bf16), wo_ref[...])

    def epilogue(c):
        rows = rows_of(c)
        o_ref[rows, :] = _layer_norm(alpha * x_ref[rows, :] + out_ref[c], g_ref[...], b_ref[...])

    n_c = TB // BACK_ROWS
    matmuls(0)

    def step(c, _):
        matmuls(c + 1)
        epilogue(c)
        return 0

    lax.fori_loop(0, n_c - 1, step, 0)
    epilogue(n_c - 1)


def _back(x2, seq_len, y_a, h_f, h_b0, p_b, z_b, g_a, g_b, w_ba, w_bb, w_out, ln_g, ln_b, alpha):
    t = x2.shape[0]
    n = t // TB
    rev = lambda i: (n - 1 - i, 0)
    const = lambda i: (0, 0)
    rows = lambda w: pl.BlockSpec((TB, w), rev)
    full = lambda a: pl.BlockSpec(a.shape, const)
    return pl.pallas_call(
        functools.partial(_back_kernel, alpha=alpha, blocks_per_seq=seq_len // TB),
        grid=(n,),
        in_specs=[rows(D_MODEL), rows(W_ATTN), rows(W_LRU), rows(W_LRU), rows(W_LRU), rows(W_LRU),
                  rows(D_MODEL), rows(D_MODEL), full(w_ba), full(w_bb), full(w_out), full(ln_g), full(ln_b)],
        out_specs=rows(D_MODEL),
        out_shape=jax.ShapeDtypeStruct((t, D_MODEL), _f32),
        scratch_shapes=[pltpu.VMEM((SUBLANES, W_LRU), _f32), pltpu.VMEM((SUBLANES, W_LRU), _f32),
                        pltpu.VMEM((TB // BACK_ROWS, BACK_ROWS, D_MODEL), _f32)],
        compiler_params=pltpu.CompilerParams(dimension_semantics=("arbitrary",),
                                             vmem_limit_bytes=VMEM_LIMIT),
        name="back",
    )(x2, y_a, h_f, h_b0, p_b, z_b, g_a, g_b, w_ba, w_bb, w_out, ln_g, ln_b)


def kernel(x, emb_ln_g, emb_ln_b, w_in, rpb, conv_w, conv_b, lru_gate_w, lru_gate_b, lru_lambda,
           w_branch_attn, w_branch_lru, b_merge, w_out, ln_g, ln_b):
    b, s, d = x.shape
    depth = w_in.shape[0]
    assert d == D_MODEL and s % (Q_ROWS * GRID_W) == 0 and s % TM == 0 and s % TB == 0 and TB % TM == 0
    alpha = (2 * depth) ** 0.25
    t = b * s
    h = x.reshape(t, d)
    row2 = lambda a: a.reshape(1, -1)
    rowm = _attn_row_mask(s // GRID_W)
    for l in range(depth):
        outs = _front(h, s, row2(emb_ln_g), row2(emb_ln_b), w_in[l].astype(_bf16), b_merge[l], conv_w[l],
                      conv_b[l], lru_gate_w[l], lru_gate_b[l], lru_lambda[l], apply_ln=(l == 0))
        if l == 0:
            h, outs = outs[0], outs[1:]
        q, k, v, z_a, z_b, g_a, g_b, h_f, h_b0, p_b = outs
        seq = lambda a: a.reshape(b, s, a.shape[-1])
        y_a = _attention(seq(q), seq(k), seq(v), seq(z_a), _attn_value_table(rpb[l]), rowm)
        h = _back(h, s, y_a.reshape(t, W_ATTN), h_f, h_b0, p_b, z_b, g_a, g_b,
                  w_branch_attn[l].astype(_bf16), w_branch_lru[l].astype(_bf16),
                  w_out[l].astype(_bf16), row2(ln_g[l]), row2(ln_b[l]), alpha)
    return h.reshape(b, s, d)
```

```python
import functools

import numpy as np
import jax
import jax.numpy as jnp
from jax import lax
from jax.experimental import pallas as pl
from jax.experimental.pallas import tpu as pltpu

D_MODEL = 1024
GRID_W = 64
N_HEADS = 8
HEAD_DIM = 64
W_ATTN = N_HEADS * HEAD_DIM
WIN_H = 8
WIN_W = 16
W_LRU = 512
N_BLOCKS = 8
BLOCK_W = 64
CONV_W = 4
C_LRU = 8.0
LN_EPS = 1e-5
NEG = -1e30
TINY = 1e-30
LOG2_E = 1.4426950408889634

LANES = 128
SUBLANES = 8
VMEM_LIMIT = 56 * 1024 * 1024

TM = 512
TB = 1024
Q_ROWS = 16
SUB_ROWS = 4
KEY_ROWS = SUB_ROWS + WIN_H
KV_BLK_ROWS = 4
N_KV_BLKS = (Q_ROWS + WIN_H) // KV_BLK_ROWS
N_CHUNK = SUBLANES
G_LEN = TM // N_CHUNK
PITCH = G_LEN + 8
N_LG = W_LRU // LANES
PROJ_COLS = 256

_f32 = jnp.float32
_bf16 = jnp.bfloat16


def _dot(a, b):
    return jnp.dot(a, b, preferred_element_type=_f32)


def _layer_norm(x, g, b):
    mu = jnp.mean(x, axis=-1, keepdims=True)
    xc = x - mu
    var = jnp.mean(xc * xc, axis=-1, keepdims=True)
    return xc * lax.rsqrt(var + LN_EPS) * g + b


def _sigmoid(x):
    return 0.5 + 0.5 * jnp.tanh(0.5 * x)


def _silu(x):
    hx = 0.5 * x
    return hx + hx * jnp.tanh(hx)


def _stage_time_major(u, ut_ref):
    for ch in range(N_CHUNK):
        for lg in range(N_LG):
            ut_ref[lg, ch * PITCH:ch * PITCH + G_LEN, :] = u[ch * G_LEN:(ch + 1) * G_LEN,
                                                            lg * LANES:(lg + 1) * LANES]


def _load_slab(ut_ref, g):
    return jnp.concatenate([ut_ref[lg, pl.ds(g, N_CHUNK, stride=PITCH), :] for lg in range(N_LG)], axis=1)


def _slab_rows(g):
    return slice(g * N_CHUNK, (g + 1) * N_CHUNK)


def _unstage_slabs(slab_fn, ot_ref, g_range):
    for g in g_range:
        slab = slab_fn(g)
        for lg in range(N_LG):
            ot_ref[lg, pl.ds(g, N_CHUNK, stride=PITCH), :] = slab[:, lg * LANES:(lg + 1) * LANES]


def _copy_time_major(ot_ref, o_ref, ch_range):
    for ch in ch_range:
        for lg in range(N_LG):
            o_ref[ch * G_LEN:(ch + 1) * G_LEN, lg * LANES:(lg + 1) * LANES] = (
                ot_ref[lg, ch * PITCH:ch * PITCH + G_LEN, :].astype(o_ref.dtype))


def _gate_dots(ub_ref, wg_ref):
    half = W_LRU // 2
    pre = []
    for gi in range(2):
        cols = [_dot(ub_ref[:, hf * half:(hf + 1) * half], wg_ref[gi, hf]) for hf in range(2)]
        pre.append(jnp.concatenate(cols, axis=1))
    return pre


def _gates(pre, hu_ref, bg_ref, coef, first, g_first, a_ref, bx_ref):
    t_r = jnp.tanh(pre[0] + bg_ref[0])
    t_i = jnp.tanh(pre[1] + bg_ref[1])
    a = jnp.exp2(coef + coef * t_r)
    one_m_a2 = (1.0 - a) * (1.0 + a)
    mult = one_m_a2 * lax.rsqrt(jnp.maximum(one_m_a2, TINY))
    hu = hu_ref[...]
    iu = hu + hu * t_i
    bx = mult * iu
    a_ref[...] = a
    bx_ref[...] = bx
    fix = _slab_rows(g_first)
    bx_ref[fix, :] = jnp.where(first, iu[fix], bx[fix])


def _chunk_scan(a_ref, bx_ref, hl_ref, pl_ref, g_range):
    h = jnp.zeros((N_CHUNK, W_LRU), _f32)
    p = jnp.ones((N_CHUNK, W_LRU), _f32)
    for g in g_range:
        rows = _slab_rows(g)
        ag = a_ref[rows, :]
        h = ag * h + bx_ref[rows, :]
        p = ag * p
        hl_ref[rows, :] = h
        pl_ref[rows, :] = p
    return h, p


def _front_kernel(x_ref, xn_ref, g_ref, b_ref, w_ref, bm_ref, cw_ref, cb_ref, wg_ref, bg_ref, lam_ref,
                  *refs, apply_ln, blocks_per_seq):
    n_out = 11 if apply_ln else 10
    out_refs = refs[:n_out]
    (uprev_ref, carry_ref, xb_ref, hu_ref, ub_ref, a_ref, bx_ref, hl_ref, pl_ref,
     ut_ref, ot_f, ot_h, ot_p, pre_ref) = refs[n_out:]
    if apply_ln:
        h_ref, out_refs = out_refs[0], out_refs[1:]
    q_ref, k_ref, v_ref, za_ref, zb_ref, ga_ref, gb_ref, hf_ref, hb0_ref, pb_ref = out_refs
    j = pl.program_id(0)
    is_first_blk = j % blocks_per_seq == 0
    is_last_blk = j % blocks_per_seq == blocks_per_seq - 1
    u_off = 4 * W_ATTN
    g_off = u_off + 2 * W_LRU
    sub = lax.broadcasted_iota(jnp.int32, (N_CHUNK, W_LRU), 0)

    def edge(row):
        return jnp.broadcast_to(row, (N_CHUNK, W_LRU))

    def proj(o_ref, off, post):
        for c0 in range(0, o_ref.shape[-1], PROJ_COLS):
            y = _dot(xb_ref[...], w_ref[:, off + c0:off + c0 + PROJ_COLS])
            o_ref[:, c0:c0 + PROJ_COLS] = post(y, slice(c0, c0 + PROJ_COLS)).astype(o_ref.dtype)

    @pl.when(is_first_blk)
    def _():
        uprev_ref[...] = jnp.zeros_like(uprev_ref)
        carry_ref[...] = jnp.zeros_like(carry_ref)

    x = x_ref[...]
    xn = xn_ref[...]
    if apply_ln:
        x = _layer_norm(x, g_ref[...], b_ref[...])
        xn = _layer_norm(xn, g_ref[...], b_ref[...])
        h_ref[...] = x
    xb_ref[...] = x.astype(_bf16)
    u = _dot(xb_ref[...], w_ref[:, u_off:u_off + W_LRU])
    u_next = _dot(xn.astype(_bf16), w_ref[:, u_off:u_off + W_LRU])
    _stage_time_major(u, ut_ref)

    slabs = {}
    slabs[-2] = jnp.where(sub == 0, edge(uprev_ref[SUBLANES - 2:SUBLANES - 1, :]),
                          pltpu.roll(_load_slab(ut_ref, G_LEN - 2), 1, axis=0))
    slabs[-1] = jnp.where(sub == 0, edge(uprev_ref[SUBLANES - 1:SUBLANES, :]),
                          pltpu.roll(_load_slab(ut_ref, G_LEN - 1), 1, axis=0))
    nxt = jnp.where(is_last_blk, 0.0, u_next[0:1, :])
    slabs[G_LEN] = jnp.where(sub == N_CHUNK - 1, edge(nxt), pltpu.roll(_load_slab(ut_ref, 0), N_CHUNK - 1, axis=0))
    uprev_ref[...] = u[TM - SUBLANES:TM, :]

    def slab(g):
        if g not in slabs:
            slabs[g] = _load_slab(ut_ref, g)
        return slabs[g]

    cw = [cw_ref[i:i + 1, :] for i in range(CONV_W)]
    cb = cb_ref[...]
    for g in range(G_LEN):
        hu_ref[_slab_rows(g), :] = (slab(g - 2) * cw[0] + slab(g - 1) * cw[1] + slab(g) * cw[2]
                                    + slab(g + 1) * cw[3] + cb)
    ub_ref[...] = hu_ref[...].astype(_bf16)
    proj(q_ref, 0, lambda y, cols: y * (HEAD_DIM ** -0.5 * LOG2_E))
    proj(k_ref, W_ATTN, lambda y, cols: y)

    for di in range(2):
        pre = _gate_dots(ub_ref, wg_ref.at[di])
        pre_ref[2 * di] = pre[0]
        pre_ref[2 * di + 1] = pre[1]

    neg_lam = -lam_ref[...]
    softplus = jnp.maximum(neg_lam, 0.0) + jnp.log(1.0 + jnp.exp(-jnp.abs(neg_lam)))
    coef = (-0.5 * C_LRU * LOG2_E) * softplus

    first = jnp.logical_and(is_first_blk, sub == 0)
    _gates((pre_ref[0], pre_ref[1]), hu_ref, bg_ref.at[0], coef[0:1], first, 0, a_ref, bx_ref)
    h_end, p_end = _chunk_scan(a_ref, bx_ref, hl_ref, pl_ref, range(G_LEN))
    c = jnp.where(sub == 0, edge(carry_ref[N_CHUNK - 1:N_CHUNK, :]), 0.0)
    for kk in range(1, N_CHUNK):
        c = jnp.where(sub == kk, pltpu.roll(p_end * c + h_end, 1, axis=0), c)
    carry_ref[...] = p_end * c + h_end
    _unstage_slabs(lambda g: pl_ref[_slab_rows(g), :] * c + hl_ref[_slab_rows(g), :], ot_f, range(G_LEN))
    _copy_time_major(ot_f, hf_ref, range(N_CHUNK))
    proj(v_ref, 2 * W_ATTN, lambda y, cols: y)
    proj(za_ref, 3 * W_ATTN, lambda y, cols: _silu(y))
    proj(zb_ref, u_off + W_LRU, lambda y, cols: _silu(y))

    last = jnp.logical_and(is_last_blk, sub == N_CHUNK - 1)
    _gates((pre_ref[2], pre_ref[3]), hu_ref, bg_ref.at[1], coef[1:2], last, G_LEN - 1, a_ref, bx_ref)
    h_end, p_end = _chunk_scan(a_ref, bx_ref, hl_ref, pl_ref, range(G_LEN - 1, -1, -1))
    cz = jnp.zeros((N_CHUNK, W_LRU), _f32)
    d = jnp.where(sub == N_CHUNK - 1, 1.0, 0.0)
    for kk in range(N_CHUNK - 2, -1, -1):
        cz = jnp.where(sub == kk, pltpu.roll(p_end * cz + h_end, N_CHUNK - 1, axis=0), cz)
        d = jnp.where(sub == kk, pltpu.roll(p_end * d, N_CHUNK - 1, axis=0), d)
    _unstage_slabs(lambda g: pl_ref[_slab_rows(g), :] * cz + hl_ref[_slab_rows(g), :], ot_h, range(G_LEN))
    _unstage_slabs(lambda g: pl_ref[_slab_rows(g), :] * d, ot_p, range(G_LEN))
    _copy_time_major(ot_h, hb0_ref, range(N_CHUNK))
    _copy_time_major(ot_p, pb_ref, range(N_CHUNK))
    proj(ga_ref, g_off, lambda y, cols: _sigmoid(y + bm_ref[0:1, cols]))
    proj(gb_ref, g_off + D_MODEL, lambda y, cols: _sigmoid(y + bm_ref[1:2, cols]))


def _block_diag_halves(w):
    per_half = N_BLOCKS // 2
    eye = jnp.eye(per_half, dtype=w.dtype)
    w = w.reshape(w.shape[:-3] + (2, per_half, BLOCK_W, BLOCK_W))
    full = jnp.einsum('...hbde,bc->...hbdce', w, eye)
    return full.reshape(w.shape[:-3] + (per_half * BLOCK_W, per_half * BLOCK_W)).astype(_bf16)


def _front(x2, seq_len, ln_g, ln_b, w_in_bf16, b_merge, conv_w, conv_b, gate_w, gate_b, lam, apply_ln):
    t = x2.shape[0]
    n = t // TM
    per = TM // SUBLANES
    row = lambda i: (i, 0)
    full = lambda a: pl.BlockSpec(a.shape, lambda i: (0,) * a.ndim)
    wg = _block_diag_halves(gate_w)
    bg = (0.5 * gate_b).reshape(2, 2, 1, W_LRU)
    conv_w = 0.5 * conv_w
    cb2 = (0.5 * conv_b).reshape(1, W_LRU)
    widths = (W_ATTN, W_ATTN, W_ATTN, W_ATTN, W_LRU, D_MODEL, D_MODEL, W_LRU, W_LRU, W_LRU)
    out_shape = [jax.ShapeDtypeStruct((t, w), _bf16) for w in widths]
    out_specs = [pl.BlockSpec((TM, w), row) for w in widths]
    if apply_ln:
        out_shape = [jax.ShapeDtypeStruct((t, D_MODEL), _f32)] + out_shape
        out_specs = [pl.BlockSpec((TM, D_MODEL), row)] + out_specs
    stage = pltpu.VMEM((N_LG, N_CHUNK * PITCH, LANES), _f32)
    block_f32 = pltpu.VMEM((TM, W_LRU), _f32)
    return pl.pallas_call(
        functools.partial(_front_kernel, apply_ln=apply_ln, blocks_per_seq=seq_len // TM),
        grid=(n,),
        in_specs=[pl.BlockSpec((TM, D_MODEL), row),
                  pl.BlockSpec((SUBLANES, D_MODEL), lambda i: (jnp.minimum((i + 1) * per, n * per - 1), 0)),
                  full(ln_g), full(ln_b), full(w_in_bf16), full(b_merge), full(conv_w), full(cb2),
                  full(wg), full(bg), full(lam)],
        out_specs=out_specs,
        out_shape=out_shape,
        scratch_shapes=[pltpu.VMEM((SUBLANES, W_LRU), _f32), pltpu.VMEM((N_CHUNK, W_LRU), _f32),
                        pltpu.VMEM((TM, D_MODEL), _bf16), block_f32, pltpu.VMEM((TM, W_LRU), _bf16),
                        block_f32, block_f32, block_f32, block_f32, stage, stage, stage, stage,
                        pltpu.VMEM((4, TM, W_LRU), _f32)],
        compiler_params=pltpu.CompilerParams(dimension_semantics=("arbitrary",),
                                             vmem_limit_bytes=VMEM_LIMIT),
        name="front",
    )(x2, x2, ln_g, ln_b, w_in_bf16, b_merge, conv_w, cb2, wg, bg, lam)


def _attn_value_table(rpb):
    qc = np.arange(GRID_W)[:, None]
    kc = np.arange(GRID_W)[None, :]
    cs = np.clip(qc - WIN_W // 2, 0, GRID_W - WIN_W)
    ok = (kc >= cs) & (kc < cs + WIN_W)
    pad = GRID_W - WIN_W
    padded = jnp.pad(rpb.astype(_f32), ((0, 0), (0, 0), (pad, pad)))
    skew = jnp.tile(padded, (1, 1, GRID_W + 1))[:, :, :GRID_W * 2 * GRID_W]
    skew = skew.reshape(skew.shape[:2] + (GRID_W, 2 * GRID_W))
    t = skew[:, :, ::-1, :GRID_W]
    t = jnp.where(ok, t * LOG2_E, NEG)
    return jnp.concatenate([t[:, :-1], t[:, 1:]], axis=-1)


def _attn_row_mask(rows):
    r_first = (0, Q_ROWS, rows - Q_ROWS)
    m = np.full((3, Q_ROWS // SUB_ROWS, SUB_ROWS, SUBLANES, KEY_ROWS * GRID_W), NEG, np.float32)
    for v, r0 in enumerate(r_first):
        for half in range(Q_ROWS // SUB_ROWS):
            for qr in range(SUB_ROWS):
                r = r0 + half * SUB_ROWS + qr
                rs = min(max(r - WIN_H // 2, 0), rows - WIN_H)
                for kr in range(KEY_ROWS):
                    key_r = r0 + half * SUB_ROWS - WIN_H // 2 + kr
                    if rs <= key_r < rs + WIN_H:
                        m[v, half, qr, :, kr * GRID_W:(kr + 1) * GRID_W] = 0.0
    return m


def _attn_kernel(q_ref, *refs):
    k_refs, v_refs = refs[:N_KV_BLKS], refs[N_KV_BLKS:2 * N_KV_BLKS]
    z_ref, val_ref, rowm_ref, o_ref = refs[2 * N_KV_BLKS:]
    m_q = SUB_ROWS * GRID_W
    n_kb = KEY_ROWS // KV_BLK_ROWS
    blk_keys = KV_BLK_ROWS * GRID_W
    lane = lax.broadcasted_iota(jnp.int32, (m_q, LANES), 1)
    even = lane < HEAD_DIM
    for half in range(Q_ROWS // SUB_ROWS):
        rows = slice(half * m_q, (half + 1) * m_q)
        for pair in range(N_HEADS // 2):
            cols = slice(pair * LANES, (pair + 1) * LANES)
            qp = q_ref[rows, cols]
            zero = jnp.zeros_like(qp)
            q2 = jnp.concatenate([jnp.where(even, qp, zero), jnp.where(even, zero, qp)], axis=0)
            sk = [lax.dot_general(q2, k_refs[half + j][:, cols], (((1,), (1,)), ((), ())),
                                  preferred_element_type=_f32) for j in range(n_kb)]
            s = jnp.concatenate(sk, axis=1)
            blocks = []
            for parity in range(2):
                for qr in range(SUB_ROWS):
                    val = jnp.concatenate(
                        [val_ref[2 * pair + parity, 2 * j - qr + WIN_H // 2 - 1]
                         for j in range(KEY_ROWS // 2)], axis=1)
                    r0 = (parity * SUB_ROWS + qr) * GRID_W
                    blk = s[r0:r0 + GRID_W] + val
                    blk = blk.reshape(GRID_W // SUBLANES, SUBLANES, -1) + rowm_ref[half, qr][None]
                    blocks.append(blk.reshape(GRID_W, -1))
            s = jnp.concatenate(blocks, axis=0)
            m = jnp.max(s, axis=-1, keepdims=True)
            e = jnp.exp2(s - m)
            l = jnp.sum(e, axis=-1, keepdims=True)
            eb = e.astype(_bf16)
            o2 = sum(_dot(eb[:, j * blk_keys:(j + 1) * blk_keys], v_refs[half + j][:, cols])
                     for j in range(n_kb))
            o = jnp.where(even, o2[:m_q] / l[:m_q], o2[m_q:] / l[m_q:])
            o_ref[rows, cols] = (o * z_ref[rows, cols].astype(_f32)).astype(o_ref.dtype)


def _attention(q, k, v, z_a, val, rowm):
    b, s, _ = q.shape
    rows = s // GRID_W
    n_blk = rows // Q_ROWS
    n_kv = rows // KV_BLK_ROWS
    tq = Q_ROWS * GRID_W
    tkv = KV_BLK_ROWS * GRID_W

    def kv_spec(j):
        first = -(WIN_H // 2) // KV_BLK_ROWS
        return pl.BlockSpec(
            (None, tkv, W_ATTN),
            lambda bi, i: (bi, jnp.clip(i * (Q_ROWS // KV_BLK_ROWS) + first + j, 0, n_kv - 1), 0))

    def case(i):
        return jnp.where(i == 0, 0, jnp.where(i == n_blk - 1, 2, 1))

    blk = pl.BlockSpec((None, tq, W_ATTN), lambda bi, i: (bi, i, 0))
    return pl.pallas_call(
        _attn_kernel,
        grid=(b, n_blk),
        in_specs=[blk] + [kv_spec(j) for j in range(N_KV_BLKS)] * 2 + [
            blk,
            pl.BlockSpec(val.shape, lambda bi, i: (0, 0, 0, 0)),
            pl.BlockSpec((None,) + rowm.shape[1:], lambda bi, i: (case(i), 0, 0, 0, 0))],
        out_specs=blk,
        out_shape=jax.ShapeDtypeStruct((b, s, W_ATTN), _bf16),
        compiler_params=pltpu.CompilerParams(dimension_semantics=("arbitrary", "arbitrary"),
                                             vmem_limit_bytes=VMEM_LIMIT),
        name="attn",
    )(q, *([k] * N_KV_BLKS), *([v] * N_KV_BLKS), z_a, val, rowm)


def _back_kernel(x_ref, ya_ref, hf_ref, hb0_ref, pb_ref, zb_ref, ga_ref, gb_ref, wba_ref, wbb_ref,
                 wo_ref, g_ref, b_ref, o_ref, carry_ref, *, alpha, blocks_per_seq):
    i = pl.program_id(0)

    @pl.when(i % blocks_per_seq == 0)
    def _():
        carry_ref[...] = jnp.zeros_like(carry_ref)

    carry = carry_ref[0:1, :]
    parts = []
    for sb in range(TB // TM - 1, -1, -1):
        rows = slice(sb * TM, (sb + 1) * TM)
        parts.insert(0, (hf_ref[rows, :] + (hb0_ref[rows, :] + pb_ref[rows, :] * carry.astype(_bf16)))
                     * zb_ref[rows, :])
        r0 = sb * TM
        carry = hb0_ref[r0:r0 + 1, :].astype(_f32) + pb_ref[r0:r0 + 1, :].astype(_f32) * carry
    carry_ref[0:1, :] = carry
    yb = jnp.concatenate(parts, axis=0)
    pa = _dot(ya_ref[...], wba_ref[...])
    pb = _dot(yb, wbb_ref[...])
    m = ga_ref[...] * pa.astype(_bf16) + gb_ref[...] * pb.astype(_bf16)
    out = _dot(m, wo_ref[...])
    o_ref[...] = _layer_norm(alpha * x_ref[...] + out, g_ref[...], b_ref[...])


def _back(x2, seq_len, y_a, h_f, h_b0, p_b, z_b, g_a, g_b, w_ba, w_bb, w_out, ln_g, ln_b, alpha):
    t = x2.shape[0]
    n = t // TB
    rev = lambda i: (n - 1 - i, 0)
    const = lambda i: (0, 0)
    rows = lambda w: pl.BlockSpec((TB, w), rev)
    full = lambda a: pl.BlockSpec(a.shape, const)
    return pl.pallas_call(
        functools.partial(_back_kernel, alpha=alpha, blocks_per_seq=seq_len // TB),
        grid=(n,),
        in_specs=[rows(D_MODEL), rows(W_ATTN), rows(W_LRU), rows(W_LRU), rows(W_LRU), rows(W_LRU),
                  rows(D_MODEL), rows(D_MODEL), full(w_ba), full(w_bb), full(w_out), full(ln_g), full(ln_b)],
        out_specs=rows(D_MODEL),
        out_shape=jax.ShapeDtypeStruct((t, D_MODEL), _f32),
        scratch_shapes=[pltpu.VMEM((SUBLANES, W_LRU), _f32)],
        compiler_params=pltpu.CompilerParams(dimension_semantics=("arbitrary",),
                                             vmem_limit_bytes=VMEM_LIMIT),
        name="back",
    )(x2, y_a, h_f, h_b0, p_b, z_b, g_a, g_b, w_ba, w_bb, w_out, ln_g, ln_b)


def kernel(x, emb_ln_g, emb_ln_b, w_in, rpb, conv_w, conv_b, lru_gate_w, lru_gate_b, lru_lambda,
           w_branch_attn, w_branch_lru, b_merge, w_out, ln_g, ln_b):
    b, s, d = x.shape
    depth = w_in.shape[0]
    assert d == D_MODEL and s % (Q_ROWS * GRID_W) == 0 and s % TM == 0 and s % TB == 0 and TB % TM == 0
    alpha = (2 * depth) ** 0.25
    t = b * s
    h = x.reshape(t, d)
    row2 = lambda a: a.reshape(1, -1)
    rowm = _attn_row_mask(s // GRID_W)
    for l in range(depth):
        outs = _front(h, s, row2(emb_ln_g), row2(emb_ln_b), w_in[l].astype(_bf16), b_merge[l], conv_w[l],
                      conv_b[l], lru_gate_w[l], lru_gate_b[l], lru_lambda[l], apply_ln=(l == 0))
        if l == 0:
            h, outs = outs[0], outs[1:]
        q, k, v, z_a, z_b, g_a, g_b, h_f, h_b0, p_b = outs
        seq = lambda a: a.reshape(b, s, a.shape[-1])
        y_a = _attention(seq(q), seq(k), seq(v), seq(z_a), _attn_value_table(rpb[l]), rowm)
        h = _back(h, s, y_a.reshape(t, W_ATTN), h_f, h_b0, p_b, z_b, g_a, g_b,
                  w_branch_attn[l].astype(_bf16), w_branch_lru[l].astype(_bf16),
                  w_out[l].astype(_bf16), row2(ln_g[l]), row2(ln_b[l]), alpha)
    return h.reshape(b, s, d)
```

```python
import functools

import numpy as np
import jax
import jax.numpy as jnp
from jax import lax
from jax.experimental import pallas as pl
from jax.experimental.pallas import tpu as pltpu

D_MODEL = 1024
GRID_W = 64
N_HEADS = 8
HEAD_DIM = 64
W_ATTN = N_HEADS * HEAD_DIM
WIN_H = 8
WIN_W = 16
W_LRU = 512
N_BLOCKS = 8
BLOCK_W = 64
CONV_W = 4
C_LRU = 8.0
LN_EPS = 1e-5
NEG = -1e30
TINY = 1e-30
LOG2_E = 1.4426950408889634

LANES = 128
SUBLANES = 8
VMEM_LIMIT = 56 * 1024 * 1024

TM = 512
TB = 1024
Q_ROWS = 16
SUB_ROWS = 4
KEY_ROWS = SUB_ROWS + WIN_H
KV_BLK_ROWS = 4
N_KV_BLKS = (Q_ROWS + WIN_H) // KV_BLK_ROWS
QC = 32
KC = 40
KC_START = (0, GRID_W - KC)
KEYS_PAD = 512
N_CHUNK = SUBLANES
G_LEN = TM // N_CHUNK
PITCH = G_LEN + 8
N_LG = W_LRU // LANES
PROJ_COLS = 256

_f32 = jnp.float32
_bf16 = jnp.bfloat16


def _dot(a, b):
    return jnp.dot(a, b, preferred_element_type=_f32)


def _layer_norm(x, g, b):
    mu = jnp.mean(x, axis=-1, keepdims=True)
    xc = x - mu
    var = jnp.mean(xc * xc, axis=-1, keepdims=True)
    return xc * lax.rsqrt(var + LN_EPS) * g + b


def _sigmoid(x):
    return 0.5 + 0.5 * jnp.tanh(0.5 * x)


def _silu(x):
    hx = 0.5 * x
    return hx + hx * jnp.tanh(hx)


def _stage_time_major(u, ut_ref):
    for ch in range(N_CHUNK):
        for lg in range(N_LG):
            ut_ref[lg, ch * PITCH:ch * PITCH + G_LEN, :] = u[ch * G_LEN:(ch + 1) * G_LEN,
                                                            lg * LANES:(lg + 1) * LANES]


def _load_slab(ut_ref, g):
    return jnp.concatenate([ut_ref[lg, pl.ds(g, N_CHUNK, stride=PITCH), :] for lg in range(N_LG)], axis=1)


def _slab_rows(g):
    return slice(g * N_CHUNK, (g + 1) * N_CHUNK)


def _unstage_slabs(slab_fn, ot_ref, g_range):
    for g in g_range:
        slab = slab_fn(g)
        for lg in range(N_LG):
            ot_ref[lg, pl.ds(g, N_CHUNK, stride=PITCH), :] = slab[:, lg * LANES:(lg + 1) * LANES]


def _copy_time_major(ot_ref, o_ref, ch_range):
    for ch in ch_range:
        for lg in range(N_LG):
            o_ref[ch * G_LEN:(ch + 1) * G_LEN, lg * LANES:(lg + 1) * LANES] = (
                ot_ref[lg, ch * PITCH:ch * PITCH + G_LEN, :].astype(o_ref.dtype))


def _gate_dots(ub_ref, wg_ref):
    half = W_LRU // 2
    pre = []
    for gi in range(2):
        cols = [_dot(ub_ref[:, hf * half:(hf + 1) * half], wg_ref[gi, hf]) for hf in range(2)]
        pre.append(jnp.concatenate(cols, axis=1))
    return pre


def _gates(pre, hu_ref, bg_ref, coef, first, g_first, a_ref, bx_ref):
    t_r = jnp.tanh(pre[0] + bg_ref[0])
    t_i = jnp.tanh(pre[1] + bg_ref[1])
    a = jnp.exp2(coef + coef * t_r)
    one_m_a2 = (1.0 - a) * (1.0 + a)
    mult = one_m_a2 * lax.rsqrt(jnp.maximum(one_m_a2, TINY))
    hu = hu_ref[...]
    iu = hu + hu * t_i
    bx = mult * iu
    a_ref[...] = a
    bx_ref[...] = bx
    fix = _slab_rows(g_first)
    bx_ref[fix, :] = jnp.where(first, iu[fix], bx[fix])


def _chunk_scan(a_ref, bx_ref, hl_ref, pl_ref, g_range):
    h = jnp.zeros((N_CHUNK, W_LRU), _f32)
    p = jnp.ones((N_CHUNK, W_LRU), _f32)
    for g in g_range:
        rows = _slab_rows(g)
        ag = a_ref[rows, :]
        h = ag * h + bx_ref[rows, :]
        p = ag * p
        hl_ref[rows, :] = h
        pl_ref[rows, :] = p
    return h, p


def _front_kernel(x_ref, xn_ref, g_ref, b_ref, w_ref, bm_ref, cw_ref, cb_ref, wg_ref, bg_ref, lam_ref,
                  *refs, apply_ln, blocks_per_seq):
    n_out = 11 if apply_ln else 10
    out_refs = refs[:n_out]
    (uprev_ref, carry_ref, xb_ref, hu_ref, ub_ref, a_ref, bx_ref, hl_ref, pl_ref,
     ut_ref, ot_f, ot_h, ot_p, pre_ref) = refs[n_out:]
    if apply_ln:
        h_ref, out_refs = out_refs[0], out_refs[1:]
    q_ref, k_ref, v_ref, za_ref, zb_ref, ga_ref, gb_ref, hf_ref, hb0_ref, pb_ref = out_refs
    j = pl.program_id(0)
    is_first_blk = j % blocks_per_seq == 0
    is_last_blk = j % blocks_per_seq == blocks_per_seq - 1
    u_off = 4 * W_ATTN
    g_off = u_off + 2 * W_LRU
    sub = lax.broadcasted_iota(jnp.int32, (N_CHUNK, W_LRU), 0)

    def edge(row):
        return jnp.broadcast_to(row, (N_CHUNK, W_LRU))

    def proj(o_ref, off, post):
        for c0 in range(0, o_ref.shape[-1], PROJ_COLS):
            y = _dot(xb_ref[...], w_ref[:, off + c0:off + c0 + PROJ_COLS])
            o_ref[:, c0:c0 + PROJ_COLS] = post(y, slice(c0, c0 + PROJ_COLS)).astype(o_ref.dtype)

    @pl.when(is_first_blk)
    def _():
        uprev_ref[...] = jnp.zeros_like(uprev_ref)
        carry_ref[...] = jnp.zeros_like(carry_ref)

    x = x_ref[...]
    xn = xn_ref[...]
    if apply_ln:
        x = _layer_norm(x, g_ref[...], b_ref[...])
        xn = _layer_norm(xn, g_ref[...], b_ref[...])
        h_ref[...] = x
    xb_ref[...] = x.astype(_bf16)
    u = _dot(xb_ref[...], w_ref[:, u_off:u_off + W_LRU])
    u_next = _dot(xn.astype(_bf16), w_ref[:, u_off:u_off + W_LRU])
    _stage_time_major(u, ut_ref)

    slabs = {}
    slabs[-2] = jnp.where(sub == 0, edge(uprev_ref[SUBLANES - 2:SUBLANES - 1, :]),
                          pltpu.roll(_load_slab(ut_ref, G_LEN - 2), 1, axis=0))
    slabs[-1] = jnp.where(sub == 0, edge(uprev_ref[SUBLANES - 1:SUBLANES, :]),
                          pltpu.roll(_load_slab(ut_ref, G_LEN - 1), 1, axis=0))
    nxt = jnp.where(is_last_blk, 0.0, u_next[0:1, :])
    slabs[G_LEN] = jnp.where(sub == N_CHUNK - 1, edge(nxt), pltpu.roll(_load_slab(ut_ref, 0), N_CHUNK - 1, axis=0))
    uprev_ref[...] = u[TM - SUBLANES:TM, :]

    def slab(g):
        if g not in slabs:
            slabs[g] = _load_slab(ut_ref, g)
        return slabs[g]

    cw = [cw_ref[i:i + 1, :] for i in range(CONV_W)]
    cb = cb_ref[...]
    for g in range(G_LEN):
        hu_ref[_slab_rows(g), :] = (slab(g - 2) * cw[0] + slab(g - 1) * cw[1] + slab(g) * cw[2]
                                    + slab(g + 1) * cw[3] + cb)
    ub_ref[...] = hu_ref[...].astype(_bf16)
    proj(q_ref, 0, lambda y, cols: y * (HEAD_DIM ** -0.5 * LOG2_E))
    proj(k_ref, W_ATTN, lambda y, cols: y)

    for di in range(2):
        pre = _gate_dots(ub_ref, wg_ref.at[di])
        pre_ref[2 * di] = pre[0]
        pre_ref[2 * di + 1] = pre[1]

    neg_lam = -lam_ref[...]
    softplus = jnp.maximum(neg_lam, 0.0) + jnp.log(1.0 + jnp.exp(-jnp.abs(neg_lam)))
    coef = (-0.5 * C_LRU * LOG2_E) * softplus

    first = jnp.logical_and(is_first_blk, sub == 0)
    _gates((pre_ref[0], pre_ref[1]), hu_ref, bg_ref.at[0], coef[0:1], first, 0, a_ref, bx_ref)
    h_end, p_end = _chunk_scan(a_ref, bx_ref, hl_ref, pl_ref, range(G_LEN))
    c = jnp.where(sub == 0, edge(carry_ref[N_CHUNK - 1:N_CHUNK, :]), 0.0)
    for kk in range(1, N_CHUNK):
        c = jnp.where(sub == kk, pltpu.roll(p_end * c + h_end, 1, axis=0), c)
    carry_ref[...] = p_end * c + h_end
    _unstage_slabs(lambda g: pl_ref[_slab_rows(g), :] * c + hl_ref[_slab_rows(g), :], ot_f, range(G_LEN))
    _copy_time_major(ot_f, hf_ref, range(N_CHUNK))
    proj(v_ref, 2 * W_ATTN, lambda y, cols: y)
    proj(za_ref, 3 * W_ATTN, lambda y, cols: _silu(y))
    proj(zb_ref, u_off + W_LRU, lambda y, cols: _silu(y))

    last = jnp.logical_and(is_last_blk, sub == N_CHUNK - 1)
    _gates((pre_ref[2], pre_ref[3]), hu_ref, bg_ref.at[1], coef[1:2], last, G_LEN - 1, a_ref, bx_ref)
    h_end, p_end = _chunk_scan(a_ref, bx_ref, hl_ref, pl_ref, range(G_LEN - 1, -1, -1))
    cz = jnp.zeros((N_CHUNK, W_LRU), _f32)
    d = jnp.where(sub == N_CHUNK - 1, 1.0, 0.0)
    for kk in range(N_CHUNK - 2, -1, -1):
        cz = jnp.where(sub == kk, pltpu.roll(p_end * cz + h_end, N_CHUNK - 1, axis=0), cz)
        d = jnp.where(sub == kk, pltpu.roll(p_end * d, N_CHUNK - 1, axis=0), d)
    _unstage_slabs(lambda g: pl_ref[_slab_rows(g), :] * cz + hl_ref[_slab_rows(g), :], ot_h, range(G_LEN))
    _unstage_slabs(lambda g: pl_ref[_slab_rows(g), :] * d, ot_p, range(G_LEN))
    _copy_time_major(ot_h, hb0_ref, range(N_CHUNK))
    _copy_time_major(ot_p, pb_ref, range(N_CHUNK))
    proj(ga_ref, g_off, lambda y, cols: _sigmoid(y + bm_ref[0:1, cols]))
    proj(gb_ref, g_off + D_MODEL, lambda y, cols: _sigmoid(y + bm_ref[1:2, cols]))


def _block_diag_halves(w):
    per_half = N_BLOCKS // 2
    eye = jnp.eye(per_half, dtype=w.dtype)
    w = w.reshape(w.shape[:-3] + (2, per_half, BLOCK_W, BLOCK_W))
    full = jnp.einsum('...hbde,bc->...hbdce', w, eye)
    return full.reshape(w.shape[:-3] + (per_half * BLOCK_W, per_half * BLOCK_W)).astype(_bf16)


def _front(x2, seq_len, ln_g, ln_b, w_in_bf16, b_merge, conv_w, conv_b, gate_w, gate_b, lam, apply_ln):
    t = x2.shape[0]
    n = t // TM
    per = TM // SUBLANES
    row = lambda i: (i, 0)
    full = lambda a: pl.BlockSpec(a.shape, lambda i: (0,) * a.ndim)
    wg = _block_diag_halves(gate_w)
    bg = (0.5 * gate_b).reshape(2, 2, 1, W_LRU)
    conv_w = 0.5 * conv_w
    cb2 = (0.5 * conv_b).reshape(1, W_LRU)
    widths = (W_ATTN, W_ATTN, W_ATTN, W_ATTN, W_LRU, D_MODEL, D_MODEL, W_LRU, W_LRU, W_LRU)
    out_shape = [jax.ShapeDtypeStruct((t, w), _bf16) for w in widths]
    out_specs = [pl.BlockSpec((TM, w), row) for w in widths]
    if apply_ln:
        out_shape = [jax.ShapeDtypeStruct((t, D_MODEL), _f32)] + out_shape
        out_specs = [pl.BlockSpec((TM, D_MODEL), row)] + out_specs
    stage = pltpu.VMEM((N_LG, N_CHUNK * PITCH, LANES), _f32)
    block_f32 = pltpu.VMEM((TM, W_LRU), _f32)
    return pl.pallas_call(
        functools.partial(_front_kernel, apply_ln=apply_ln, blocks_per_seq=seq_len // TM),
        grid=(n,),
        in_specs=[pl.BlockSpec((TM, D_MODEL), row),
                  pl.BlockSpec((SUBLANES, D_MODEL), lambda i: (jnp.minimum((i + 1) * per, n * per - 1), 0)),
                  full(ln_g), full(ln_b), full(w_in_bf16), full(b_merge), full(conv_w), full(cb2),
                  full(wg), full(bg), full(lam)],
        out_specs=out_specs,
        out_shape=out_shape,
        scratch_shapes=[pltpu.VMEM((SUBLANES, W_LRU), _f32), pltpu.VMEM((N_CHUNK, W_LRU), _f32),
                        pltpu.VMEM((TM, D_MODEL), _bf16), block_f32, pltpu.VMEM((TM, W_LRU), _bf16),
                        block_f32, block_f32, block_f32, block_f32, stage, stage, stage, stage,
                        pltpu.VMEM((4, TM, W_LRU), _f32)],
        compiler_params=pltpu.CompilerParams(dimension_semantics=("arbitrary",),
                                             vmem_limit_bytes=VMEM_LIMIT),
        name="front",
    )(x2, x2, ln_g, ln_b, w_in_bf16, b_merge, conv_w, cb2, wg, bg, lam)


def _attn_value_table(rpb):
    qc = np.arange(GRID_W)[:, None]
    kc = np.arange(GRID_W)[None, :]
    cs = np.clip(qc - WIN_W // 2, 0, GRID_W - WIN_W)
    ok = (kc >= cs) & (kc < cs + WIN_W)
    pad = GRID_W - WIN_W
    padded = jnp.pad(rpb.astype(_f32), ((0, 0), (0, 0), (pad, pad)))
    skew = jnp.tile(padded, (1, 1, GRID_W + 1))[:, :, :GRID_W * 2 * GRID_W]
    skew = skew.reshape(skew.shape[:2] + (GRID_W, 2 * GRID_W))
    t = skew[:, :, ::-1, :GRID_W]
    t = jnp.where(ok, t * LOG2_E, NEG)
    tiles = []
    for ct, k0 in enumerate(KC_START):
        for qr in range(SUB_ROWS):
            a0 = WIN_H // 2 - 1 - qr
            sub = t[:, a0:a0 + KEY_ROWS, ct * QC:(ct + 1) * QC, k0:k0 + KC]
            sub = sub.transpose(0, 2, 1, 3).reshape(t.shape[0], QC, KEY_ROWS * KC)
            tiles.append(jnp.pad(sub, ((0, 0), (0, 0), (0, KEYS_PAD - KEY_ROWS * KC)), constant_values=NEG))
    return jnp.stack(tiles, axis=1).reshape(t.shape[0], len(KC_START), SUB_ROWS, QC, KEYS_PAD)


def _attn_row_mask(rows):
    r_first = (0, Q_ROWS, rows - Q_ROWS)
    m = np.full((3, Q_ROWS // SUB_ROWS, SUB_ROWS, SUBLANES, KEYS_PAD), NEG, np.float32)
    m[..., KEY_ROWS * KC:] = 0.0
    for v, r0 in enumerate(r_first):
        for half in range(Q_ROWS // SUB_ROWS):
            for qr in range(SUB_ROWS):
                r = r0 + half * SUB_ROWS + qr
                rs = min(max(r - WIN_H // 2, 0), rows - WIN_H)
                for kr in range(KEY_ROWS):
                    key_r = r0 + half * SUB_ROWS - WIN_H // 2 + kr
                    if rs <= key_r < rs + WIN_H:
                        m[v, half, qr, :, kr * KC:(kr + 1) * KC] = 0.0
    return m


def _attn_kernel(q_ref, *refs):
    k_refs, v_refs = refs[:N_KV_BLKS], refs[N_KV_BLKS:2 * N_KV_BLKS]
    z_ref, val_ref, rowm_ref, o_ref, k32_ref, v32_ref = refs[2 * N_KV_BLKS:]
    m_q = SUB_ROWS * QC
    blk_tok = KV_BLK_ROWS * GRID_W
    lane = lax.broadcasted_iota(jnp.int32, (m_q, LANES), 1)
    even = lane < HEAD_DIM
    for j in range(N_KV_BLKS):
        k32_ref[j * blk_tok:(j + 1) * blk_tok, :] = k_refs[j][...].astype(_f32)
        v32_ref[j * blk_tok:(j + 1) * blk_tok, :] = v_refs[j][...].astype(_f32)
    pad = jnp.zeros((KEYS_PAD - KEY_ROWS * KC, LANES), _f32)

    def key_tile(src_ref, half, k0, cols):
        parts = [src_ref[(half * SUB_ROWS + kr) * GRID_W + k0:(half * SUB_ROWS + kr) * GRID_W + k0 + KC, cols]
                 for kr in range(KEY_ROWS)]
        return jnp.concatenate(parts + [pad], axis=0).astype(_bf16)

    def q_rows(half, ct):
        return [slice((half * SUB_ROWS + qr) * GRID_W + ct * QC, (half * SUB_ROWS + qr) * GRID_W + (ct + 1) * QC)
                for qr in range(SUB_ROWS)]

    def scores(tile):
        half, ct, pair = tile
        cols = slice(pair * LANES, (pair + 1) * LANES)
        kt = key_tile(k32_ref, half, KC_START[ct], cols)
        qp = jnp.concatenate([q_ref[r, cols] for r in q_rows(half, ct)], axis=0)
        zero = jnp.zeros_like(qp)
        q2 = jnp.concatenate([jnp.where(even, qp, zero), jnp.where(even, zero, qp)], axis=0)
        return lax.dot_general(q2, kt, (((1,), (1,)), ((), ())), preferred_element_type=_f32)

    def finish(tile, s):
        half, ct, pair = tile
        cols = slice(pair * LANES, (pair + 1) * LANES)
        blocks = []
        for parity in range(2):
            for qr in range(SUB_ROWS):
                r0 = (parity * SUB_ROWS + qr) * QC
                blk = s[r0:r0 + QC] + val_ref[2 * pair + parity, ct, qr]
                blk = blk.reshape(QC // SUBLANES, SUBLANES, -1) + rowm_ref[half, qr][None]
                blocks.append(blk.reshape(QC, -1))
        s = jnp.concatenate(blocks, axis=0)
        m = jnp.max(s, axis=-1, keepdims=True)
        e = jnp.exp2(s - m)
        l = jnp.sum(e, axis=-1, keepdims=True)
        o2 = _dot(e.astype(_bf16), key_tile(v32_ref, half, KC_START[ct], cols))
        o = jnp.where(even, o2[:m_q] / l[:m_q], o2[m_q:] / l[m_q:])
        z = jnp.concatenate([z_ref[r, cols] for r in q_rows(half, ct)], axis=0).astype(_f32)
        y = (o * z).astype(o_ref.dtype)
        for qr, r in enumerate(q_rows(half, ct)):
            o_ref[r, cols] = y[qr * QC:(qr + 1) * QC]

    tiles = [(half, ct, pair) for half in range(Q_ROWS // SUB_ROWS) for ct in range(len(KC_START))
             for pair in range(N_HEADS // 2)]
    s_next = scores(tiles[0])
    for t, tile in enumerate(tiles):
        s_cur = s_next
        if t + 1 < len(tiles):
            s_next = scores(tiles[t + 1])
        finish(tile, s_cur)


def _attention(q, k, v, z_a, val, rowm):
    b, s, _ = q.shape
    rows = s // GRID_W
    n_blk = rows // Q_ROWS
    n_kv = rows // KV_BLK_ROWS
    tq = Q_ROWS * GRID_W
    tkv = KV_BLK_ROWS * GRID_W

    def kv_spec(j):
        first = -(WIN_H // 2) // KV_BLK_ROWS
        return pl.BlockSpec(
            (None, tkv, W_ATTN),
            lambda bi, i: (bi, jnp.clip(i * (Q_ROWS // KV_BLK_ROWS) + first + j, 0, n_kv - 1), 0))

    def case(i):
        return jnp.where(i == 0, 0, jnp.where(i == n_blk - 1, 2, 1))

    blk = pl.BlockSpec((None, tq, W_ATTN), lambda bi, i: (bi, i, 0))
    return pl.pallas_call(
        _attn_kernel,
        grid=(b, n_blk),
        in_specs=[blk] + [kv_spec(j) for j in range(N_KV_BLKS)] * 2 + [
            blk,
            pl.BlockSpec(val.shape, lambda bi, i: (0, 0, 0, 0, 0)),
            pl.BlockSpec((None,) + rowm.shape[1:], lambda bi, i: (case(i), 0, 0, 0, 0))],
        out_specs=blk,
        out_shape=jax.ShapeDtypeStruct((b, s, W_ATTN), _bf16),
        scratch_shapes=[pltpu.VMEM((N_KV_BLKS * tkv, W_ATTN), _f32)] * 2,
        compiler_params=pltpu.CompilerParams(dimension_semantics=("arbitrary", "arbitrary"),
                                             vmem_limit_bytes=VMEM_LIMIT),
        name="attn",
    )(q, *([k] * N_KV_BLKS), *([v] * N_KV_BLKS), z_a, val, rowm)


def _back_kernel(x_ref, ya_ref, hf_ref, hb0_ref, pb_ref, zb_ref, ga_ref, gb_ref, wba_ref, wbb_ref,
                 wo_ref, g_ref, b_ref, o_ref, carry_ref, *, alpha, blocks_per_seq):
    i = pl.program_id(0)

    @pl.when(i % blocks_per_seq == 0)
    def _():
        carry_ref[...] = jnp.zeros_like(carry_ref)

    carry = carry_ref[0:1, :]
    parts = []
    for sb in range(TB // TM - 1, -1, -1):
        rows = slice(sb * TM, (sb + 1) * TM)
        part = hb0_ref[rows, :].astype(_f32) + pb_ref[rows, :].astype(_f32) * carry
        carry = part[0:1, :]
        parts.insert(0, part)
    h_b = jnp.concatenate(parts, axis=0)
    carry_ref[...] = h_b[0:SUBLANES, :]
    yb = ((hf_ref[...].astype(_f32) + h_b) * zb_ref[...].astype(_f32)).astype(_bf16)
    pa = _dot(ya_ref[...], wba_ref[...])
    pb = _dot(yb, wbb_ref[...])
    m = ga_ref[...].astype(_f32) * pa + gb_ref[...].astype(_f32) * pb
    out = _dot(m.astype(_bf16), wo_ref[...])
    o_ref[...] = _layer_norm(alpha * x_ref[...] + out, g_ref[...], b_ref[...])


def _back(x2, seq_len, y_a, h_f, h_b0, p_b, z_b, g_a, g_b, w_ba, w_bb, w_out, ln_g, ln_b, alpha):
    t = x2.shape[0]
    n = t // TB
    rev = lambda i: (n - 1 - i, 0)
    const = lambda i: (0, 0)
    rows = lambda w: pl.BlockSpec((TB, w), rev)
    full = lambda a: pl.BlockSpec(a.shape, const)
    return pl.pallas_call(
        functools.partial(_back_kernel, alpha=alpha, blocks_per_seq=seq_len // TB),
        grid=(n,),
        in_specs=[rows(D_MODEL), rows(W_ATTN), rows(W_LRU), rows(W_LRU), rows(W_LRU), rows(W_LRU),
                  rows(D_MODEL), rows(D_MODEL), full(w_ba), full(w_bb), full(w_out), full(ln_g), full(ln_b)],
        out_specs=rows(D_MODEL),
        out_shape=jax.ShapeDtypeStruct((t, D_MODEL), _f32),
        scratch_shapes=[pltpu.VMEM((SUBLANES, W_LRU), _f32)],
        compiler_params=pltpu.CompilerParams(dimension_semantics=("arbitrary",),
                                             vmem_limit_bytes=VMEM_LIMIT),
        name="back",
    )(x2, y_a, h_f, h_b0, p_b, z_b, g_a, g_b, w_ba, w_bb, w_out, ln_g, ln_b)


def kernel(x, emb_ln_g, emb_ln_b, w_in, rpb, conv_w, conv_b, lru_gate_w, lru_gate_b, lru_lambda,
           w_branch_attn, w_branch_lru, b_merge, w_out, ln_g, ln_b):
    b, s, d = x.shape
    depth = w_in.shape[0]
    assert d == D_MODEL and s % (Q_ROWS * GRID_W) == 0 and s % TM == 0 and s % TB == 0 and TB % TM == 0
    alpha = (2 * depth) ** 0.25
    t = b * s
    h = x.reshape(t, d)
    row2 = lambda a: a.reshape(1, -1)
    rowm = _attn_row_mask(s // GRID_W)
    for l in range(depth):
        outs = _front(h, s, row2(emb_ln_g), row2(emb_ln_b), w_in[l].astype(_bf16), b_merge[l], conv_w[l],
                      conv_b[l], lru_gate_w[l], lru_gate_b[l], lru_lambda[l], apply_ln=(l == 0))
        if l == 0:
            h, outs = outs[0], outs[1:]
        q, k, v, z_a, z_b, g_a, g_b, h_f, h_b0, p_b = outs
        seq = lambda a: a.reshape(b, s, a.shape[-1])
        y_a = _attention(seq(q), seq(k), seq(v), seq(z_a), _attn_value_table(rpb[l]), rowm)
        h = _back(h, s, y_a.reshape(t, W_ATTN), h_f, h_b0, p_b, z_b, g_a, g_b,
                  w_branch_attn[l].astype(_bf16), w_branch_lru[l].astype(_bf16),
                  w_out[l].astype(_bf16), row2(ln_g[l]), row2(ln_b[l]), alpha)
    return h.reshape(b, s, d)
```

```python
import functools

import numpy as np
import jax
import jax.numpy as jnp
from jax import lax
from jax.experimental import pallas as pl
from jax.experimental.pallas import tpu as pltpu

D_MODEL = 1024
GRID_W = 64
N_HEADS = 8
HEAD_DIM = 64
W_ATTN = N_HEADS * HEAD_DIM
WIN_H = 8
WIN_W = 16
W_LRU = 512
N_BLOCKS = 8
BLOCK_W = 64
CONV_W = 4
C_LRU = 8.0
LN_EPS = 1e-5
NEG = -1e30
TINY = 1e-30
LOG2_E = 1.4426950408889634

LANES = 128
SUBLANES = 8
VMEM_LIMIT = 56 * 1024 * 1024

TM = 512
TB = 1024
BACK_ROWS = 256
Q_ROWS = 16
SUB_ROWS = 4
KEY_ROWS = SUB_ROWS + WIN_H
KV_BLK_ROWS = 4
N_KV_BLKS = (Q_ROWS + WIN_H) // KV_BLK_ROWS
QC = 32
KC = 40
KC_START = (0, GRID_W - KC)
KEYS_PAD = 512
SCORE_LOOKAHEAD = 1
N_CHUNK = SUBLANES
G_LEN = TM // N_CHUNK
PITCH = G_LEN + 8
N_LG = W_LRU // LANES
PROJ_COLS = 256

_f32 = jnp.float32
_bf16 = jnp.bfloat16


def _dot(a, b):
    return jnp.dot(a, b, preferred_element_type=_f32)


def _layer_norm(x, g, b):
    mu = jnp.mean(x, axis=-1, keepdims=True)
    xc = x - mu
    var = jnp.mean(xc * xc, axis=-1, keepdims=True)
    return xc * lax.rsqrt(var + LN_EPS) * g + b


def _sigmoid(x):
    return 0.5 + 0.5 * jnp.tanh(0.5 * x)


def _silu(x):
    hx = 0.5 * x
    return hx + hx * jnp.tanh(hx)


def _stage_time_major(u, ut_ref):
    for ch in range(N_CHUNK):
        for lg in range(N_LG):
            ut_ref[lg, ch * PITCH:ch * PITCH + G_LEN, :] = u[ch * G_LEN:(ch + 1) * G_LEN,
                                                            lg * LANES:(lg + 1) * LANES]


def _load_slab(ut_ref, g):
    return jnp.concatenate([ut_ref[lg, pl.ds(g, N_CHUNK, stride=PITCH), :] for lg in range(N_LG)], axis=1)


def _slab_rows(g):
    return slice(g * N_CHUNK, (g + 1) * N_CHUNK)


def _unstage_slabs(slab_fn, ot_ref, g_range):
    for g in g_range:
        slab = slab_fn(g)
        for lg in range(N_LG):
            ot_ref[lg, pl.ds(g, N_CHUNK, stride=PITCH), :] = slab[:, lg * LANES:(lg + 1) * LANES]


def _copy_time_major(ot_ref, o_ref, ch_range):
    for ch in ch_range:
        for lg in range(N_LG):
            o_ref[ch * G_LEN:(ch + 1) * G_LEN, lg * LANES:(lg + 1) * LANES] = (
                ot_ref[lg, ch * PITCH:ch * PITCH + G_LEN, :].astype(o_ref.dtype))


def _gate_dots(ub_ref, wg_ref):
    half = W_LRU // 2
    pre = []
    for gi in range(2):
        cols = [_dot(ub_ref[:, hf * half:(hf + 1) * half], wg_ref[gi, hf]) for hf in range(2)]
        pre.append(jnp.concatenate(cols, axis=1))
    return pre


def _gates(pre, hu_ref, bg_ref, coef, first, g_first, a_ref, bx_ref):
    t_r = jnp.tanh(pre[0] + bg_ref[0])
    t_i = jnp.tanh(pre[1] + bg_ref[1])
    a = jnp.exp2(coef + coef * t_r)
    one_m_a2 = (1.0 - a) * (1.0 + a)
    mult = one_m_a2 * lax.rsqrt(jnp.maximum(one_m_a2, TINY))
    hu = hu_ref[...]
    iu = hu + hu * t_i
    bx = mult * iu
    a_ref[...] = a
    bx_ref[...] = bx
    fix = _slab_rows(g_first)
    bx_ref[fix, :] = jnp.where(first, iu[fix], bx[fix])


def _chunk_scan(a_ref, bx_ref, hl_ref, pl_ref, g_range):
    h = jnp.zeros((N_CHUNK, W_LRU), _f32)
    p = jnp.ones((N_CHUNK, W_LRU), _f32)
    for g in g_range:
        rows = _slab_rows(g)
        ag = a_ref[rows, :]
        h = ag * h + bx_ref[rows, :]
        p = ag * p
        hl_ref[rows, :] = h
        pl_ref[rows, :] = p
    return h, p


def _front_kernel(x_ref, xn_ref, g_ref, b_ref, w_ref, bm_ref, cw_ref, cb_ref, wg_ref, bg_ref, lam_ref,
                  *refs, apply_ln, blocks_per_seq):
    n_out = 11 if apply_ln else 10
    out_refs = refs[:n_out]
    (uprev_ref, carry_ref, xb_ref, hu_ref, ub_ref, a_ref, bx_ref, hl_ref, pl_ref,
     ut_ref, ot_f, ot_h, ot_p, pre_ref) = refs[n_out:]
    if apply_ln:
        h_ref, out_refs = out_refs[0], out_refs[1:]
    q_ref, k_ref, v_ref, za_ref, zb_ref, ga_ref, gb_ref, hf_ref, hb0_ref, pb_ref = out_refs
    j = pl.program_id(0)
    is_first_blk = j % blocks_per_seq == 0
    is_last_blk = j % blocks_per_seq == blocks_per_seq - 1
    u_off = 4 * W_ATTN
    g_off = u_off + 2 * W_LRU
    sub = lax.broadcasted_iota(jnp.int32, (N_CHUNK, W_LRU), 0)

    def edge(row):
        return jnp.broadcast_to(row, (N_CHUNK, W_LRU))

    def proj(o_ref, off, post):
        for c0 in range(0, o_ref.shape[-1], PROJ_COLS):
            y = _dot(xb_ref[...], w_ref[:, off + c0:off + c0 + PROJ_COLS])
            o_ref[:, c0:c0 + PROJ_COLS] = post(y, slice(c0, c0 + PROJ_COLS)).astype(o_ref.dtype)

    @pl.when(is_first_blk)
    def _():
        uprev_ref[...] = jnp.zeros_like(uprev_ref)
        carry_ref[...] = jnp.zeros_like(carry_ref)

    x = x_ref[...]
    xn = xn_ref[...]
    if apply_ln:
        x = _layer_norm(x, g_ref[...], b_ref[...])
        xn = _layer_norm(xn, g_ref[...], b_ref[...])
        h_ref[...] = x
    xb_ref[...] = x.astype(_bf16)
    u = _dot(xb_ref[...], w_ref[:, u_off:u_off + W_LRU])
    u_next = _dot(xn.astype(_bf16), w_ref[:, u_off:u_off + W_LRU])
    _stage_time_major(u, ut_ref)

    slabs = {}
    slabs[-2] = jnp.where(sub == 0, edge(uprev_ref[SUBLANES - 2:SUBLANES - 1, :]),
                          pltpu.roll(_load_slab(ut_ref, G_LEN - 2), 1, axis=0))
    slabs[-1] = jnp.where(sub == 0, edge(uprev_ref[SUBLANES - 1:SUBLANES, :]),
                          pltpu.roll(_load_slab(ut_ref, G_LEN - 1), 1, axis=0))
    nxt = jnp.where(is_last_blk, 0.0, u_next[0:1, :])
    slabs[G_LEN] = jnp.where(sub == N_CHUNK - 1, edge(nxt), pltpu.roll(_load_slab(ut_ref, 0), N_CHUNK - 1, axis=0))
    uprev_ref[...] = u[TM - SUBLANES:TM, :]

    def slab(g):
        if g not in slabs:
            slabs[g] = _load_slab(ut_ref, g)
        return slabs[g]

    cw = [cw_ref[i:i + 1, :] for i in range(CONV_W)]
    cb = cb_ref[...]
    for g in range(G_LEN):
        hu_ref[_slab_rows(g), :] = (slab(g - 2) * cw[0] + slab(g - 1) * cw[1] + slab(g) * cw[2]
                                    + slab(g + 1) * cw[3] + cb)
    ub_ref[...] = hu_ref[...].astype(_bf16)
    proj(q_ref, 0, lambda y, cols: y * (HEAD_DIM ** -0.5 * LOG2_E))
    proj(k_ref, W_ATTN, lambda y, cols: y)

    for di in range(2):
        pre = _gate_dots(ub_ref, wg_ref.at[di])
        pre_ref[2 * di] = pre[0]
        pre_ref[2 * di + 1] = pre[1]

    neg_lam = -lam_ref[...]
    softplus = jnp.maximum(neg_lam, 0.0) + jnp.log(1.0 + jnp.exp(-jnp.abs(neg_lam)))
    coef = (-0.5 * C_LRU * LOG2_E) * softplus

    first = jnp.logical_and(is_first_blk, sub == 0)
    _gates((pre_ref[0], pre_ref[1]), hu_ref, bg_ref.at[0], coef[0:1], first, 0, a_ref, bx_ref)
    h_end, p_end = _chunk_scan(a_ref, bx_ref, hl_ref, pl_ref, range(G_LEN))
    c = jnp.where(sub == 0, edge(carry_ref[N_CHUNK - 1:N_CHUNK, :]), 0.0)
    for kk in range(1, N_CHUNK):
        c = jnp.where(sub == kk, pltpu.roll(p_end * c + h_end, 1, axis=0), c)
    carry_ref[...] = p_end * c + h_end
    _unstage_slabs(lambda g: pl_ref[_slab_rows(g), :] * c + hl_ref[_slab_rows(g), :], ot_f, range(G_LEN))
    _copy_time_major(ot_f, hf_ref, range(N_CHUNK))
    proj(v_ref, 2 * W_ATTN, lambda y, cols: y)
    proj(za_ref, 3 * W_ATTN, lambda y, cols: _silu(y))
    proj(zb_ref, u_off + W_LRU, lambda y, cols: _silu(y))

    last = jnp.logical_and(is_last_blk, sub == N_CHUNK - 1)
    _gates((pre_ref[2], pre_ref[3]), hu_ref, bg_ref.at[1], coef[1:2], last, G_LEN - 1, a_ref, bx_ref)
    h_end, p_end = _chunk_scan(a_ref, bx_ref, hl_ref, pl_ref, range(G_LEN - 1, -1, -1))
    cz = jnp.zeros((N_CHUNK, W_LRU), _f32)
    d = jnp.where(sub == N_CHUNK - 1, 1.0, 0.0)
    for kk in range(N_CHUNK - 2, -1, -1):
        cz = jnp.where(sub == kk, pltpu.roll(p_end * cz + h_end, N_CHUNK - 1, axis=0), cz)
        d = jnp.where(sub == kk, pltpu.roll(p_end * d, N_CHUNK - 1, axis=0), d)
    _unstage_slabs(lambda g: pl_ref[_slab_rows(g), :] * cz + hl_ref[_slab_rows(g), :], ot_h, range(G_LEN))
    _unstage_slabs(lambda g: pl_ref[_slab_rows(g), :] * d, ot_p, range(G_LEN))
    _copy_time_major(ot_h, hb0_ref, range(N_CHUNK))
    _copy_time_major(ot_p, pb_ref, range(N_CHUNK))
    proj(ga_ref, g_off, lambda y, cols: _sigmoid(y + bm_ref[0:1, cols]))
    proj(gb_ref, g_off + D_MODEL, lambda y, cols: _sigmoid(y + bm_ref[1:2, cols]))


def _block_diag_halves(w):
    per_half = N_BLOCKS // 2
    eye = jnp.eye(per_half, dtype=w.dtype)
    w = w.reshape(w.shape[:-3] + (2, per_half, BLOCK_W, BLOCK_W))
    full = jnp.einsum('...hbde,bc->...hbdce', w, eye)
    return full.reshape(w.shape[:-3] + (per_half * BLOCK_W, per_half * BLOCK_W)).astype(_bf16)


def _front(x2, seq_len, ln_g, ln_b, w_in_bf16, b_merge, conv_w, conv_b, gate_w, gate_b, lam, apply_ln):
    t = x2.shape[0]
    n = t // TM
    per = TM // SUBLANES
    row = lambda i: (i, 0)
    full = lambda a: pl.BlockSpec(a.shape, lambda i: (0,) * a.ndim)
    wg = _block_diag_halves(gate_w)
    bg = (0.5 * gate_b).reshape(2, 2, 1, W_LRU)
    conv_w = 0.5 * conv_w
    cb2 = (0.5 * conv_b).reshape(1, W_LRU)
    widths = (W_ATTN, W_ATTN, W_ATTN, W_ATTN, W_LRU, D_MODEL, D_MODEL, W_LRU, W_LRU, W_LRU)
    out_shape = [jax.ShapeDtypeStruct((t, w), _bf16) for w in widths]
    out_specs = [pl.BlockSpec((TM, w), row) for w in widths]
    if apply_ln:
        out_shape = [jax.ShapeDtypeStruct((t, D_MODEL), _f32)] + out_shape
        out_specs = [pl.BlockSpec((TM, D_MODEL), row)] + out_specs
    stage = pltpu.VMEM((N_LG, N_CHUNK * PITCH, LANES), _f32)
    block_f32 = pltpu.VMEM((TM, W_LRU), _f32)
    return pl.pallas_call(
        functools.partial(_front_kernel, apply_ln=apply_ln, blocks_per_seq=seq_len // TM),
        grid=(n,),
        in_specs=[pl.BlockSpec((TM, D_MODEL), row),
                  pl.BlockSpec((SUBLANES, D_MODEL), lambda i: (jnp.minimum((i + 1) * per, n * per - 1), 0)),
                  full(ln_g), full(ln_b), full(w_in_bf16), full(b_merge), full(conv_w), full(cb2),
                  full(wg), full(bg), full(lam)],
        out_specs=out_specs,
        out_shape=out_shape,
        scratch_shapes=[pltpu.VMEM((SUBLANES, W_LRU), _f32), pltpu.VMEM((N_CHUNK, W_LRU), _f32),
                        pltpu.VMEM((TM, D_MODEL), _bf16), block_f32, pltpu.VMEM((TM, W_LRU), _bf16),
                        block_f32, block_f32, block_f32, block_f32, stage, stage, stage, stage,
                        pltpu.VMEM((4, TM, W_LRU), _f32)],
        compiler_params=pltpu.CompilerParams(dimension_semantics=("arbitrary",),
                                             vmem_limit_bytes=VMEM_LIMIT),
        name="front",
    )(x2, x2, ln_g, ln_b, w_in_bf16, b_merge, conv_w, cb2, wg, bg, lam)


def _attn_value_table(rpb):
    qc = np.arange(GRID_W)[:, None]
    kc = np.arange(GRID_W)[None, :]
    cs = np.clip(qc - WIN_W // 2, 0, GRID_W - WIN_W)
    ok = (kc >= cs) & (kc < cs + WIN_W)
    pad = GRID_W - WIN_W
    padded = jnp.pad(rpb.astype(_f32), ((0, 0), (0, 0), (pad, pad)))
    skew = jnp.tile(padded, (1, 1, GRID_W + 1))[:, :, :GRID_W * 2 * GRID_W]
    skew = skew.reshape(skew.shape[:2] + (GRID_W, 2 * GRID_W))
    t = skew[:, :, ::-1, :GRID_W]
    t = jnp.where(ok, t * LOG2_E, NEG)
    tiles = []
    for ct, k0 in enumerate(KC_START):
        for qr in range(SUB_ROWS):
            a0 = WIN_H // 2 - 1 - qr
            sub = t[:, a0:a0 + KEY_ROWS, ct * QC:(ct + 1) * QC, k0:k0 + KC]
            sub = sub.transpose(0, 2, 1, 3).reshape(t.shape[0], QC, KEY_ROWS * KC)
            tiles.append(jnp.pad(sub, ((0, 0), (0, 0), (0, KEYS_PAD - KEY_ROWS * KC)), constant_values=NEG))
    return jnp.stack(tiles, axis=1).reshape(t.shape[0], len(KC_START), SUB_ROWS, QC, KEYS_PAD)


def _attn_row_mask(rows):
    r_first = (0, Q_ROWS, rows - Q_ROWS)
    m = np.full((3, Q_ROWS // SUB_ROWS, SUB_ROWS, SUBLANES, KEYS_PAD), NEG, np.float32)
    m[..., KEY_ROWS * KC:] = 0.0
    for v, r0 in enumerate(r_first):
        for half in range(Q_ROWS // SUB_ROWS):
            for qr in range(SUB_ROWS):
                r = r0 + half * SUB_ROWS + qr
                rs = min(max(r - WIN_H // 2, 0), rows - WIN_H)
                for kr in range(KEY_ROWS):
                    key_r = r0 + half * SUB_ROWS - WIN_H // 2 + kr
                    if rs <= key_r < rs + WIN_H:
                        m[v, half, qr, :, kr * KC:(kr + 1) * KC] = 0.0
    return m


def _attn_kernel(q_ref, *refs):
    k_refs, v_refs = refs[:N_KV_BLKS], refs[N_KV_BLKS:2 * N_KV_BLKS]
    z_ref, val_ref, rowm_ref, o_ref, k32_ref, v32_ref = refs[2 * N_KV_BLKS:]
    m_q = SUB_ROWS * QC
    blk_tok = KV_BLK_ROWS * GRID_W
    lane = lax.broadcasted_iota(jnp.int32, (m_q, LANES), 1)
    even = lane < HEAD_DIM
    for j in range(N_KV_BLKS):
        k32_ref[j * blk_tok:(j + 1) * blk_tok, :] = k_refs[j][...].astype(_f32)
        v32_ref[j * blk_tok:(j + 1) * blk_tok, :] = v_refs[j][...].astype(_f32)
    pad = jnp.zeros((KEYS_PAD - KEY_ROWS * KC, LANES), _f32)

    def key_tile(src_ref, half, k0, cols):
        parts = [src_ref[(half * SUB_ROWS + kr) * GRID_W + k0:(half * SUB_ROWS + kr) * GRID_W + k0 + KC, cols]
                 for kr in range(KEY_ROWS)]
        return jnp.concatenate(parts + [pad], axis=0).astype(_bf16)

    def q_rows(half, ct):
        return [slice((half * SUB_ROWS + qr) * GRID_W + ct * QC, (half * SUB_ROWS + qr) * GRID_W + (ct + 1) * QC)
                for qr in range(SUB_ROWS)]

    def scores(tile):
        half, ct, pair = tile
        cols = slice(pair * LANES, (pair + 1) * LANES)
        kt = key_tile(k32_ref, half, KC_START[ct], cols)
        qp = jnp.concatenate([q_ref[r, cols] for r in q_rows(half, ct)], axis=0)
        zero = jnp.zeros_like(qp)
        q2 = jnp.concatenate([jnp.where(even, qp, zero), jnp.where(even, zero, qp)], axis=0)
        return lax.dot_general(q2, kt, (((1,), (1,)), ((), ())), preferred_element_type=_f32)

    def finish(tile, s):
        half, ct, pair = tile
        cols = slice(pair * LANES, (pair + 1) * LANES)
        blocks = []
        for parity in range(2):
            for qr in range(SUB_ROWS):
                r0 = (parity * SUB_ROWS + qr) * QC
                blk = s[r0:r0 + QC] + val_ref[2 * pair + parity, ct, qr]
                blk = blk.reshape(QC // SUBLANES, SUBLANES, -1) + rowm_ref[half, qr][None]
                blocks.append(blk.reshape(QC, -1))
        s = jnp.concatenate(blocks, axis=0)
        m = jnp.max(s, axis=-1, keepdims=True)
        e = jnp.exp2(s - m)
        l = jnp.sum(e, axis=-1, keepdims=True)
        o2 = _dot(e.astype(_bf16), key_tile(v32_ref, half, KC_START[ct], cols))
        o = jnp.where(even, o2[:m_q] / l[:m_q], o2[m_q:] / l[m_q:])
        z = jnp.concatenate([z_ref[r, cols] for r in q_rows(half, ct)], axis=0).astype(_f32)
        y = (o * z).astype(o_ref.dtype)
        for qr, r in enumerate(q_rows(half, ct)):
            o_ref[r, cols] = y[qr * QC:(qr + 1) * QC]

    tiles = [(half, ct, pair) for half in range(Q_ROWS // SUB_ROWS) for ct in range(len(KC_START))
             for pair in range(N_HEADS // 2)]
    ahead = [scores(tile) for tile in tiles[:SCORE_LOOKAHEAD]]
    for t, tile in enumerate(tiles):
        if t + SCORE_LOOKAHEAD < len(tiles):
            ahead.append(scores(tiles[t + SCORE_LOOKAHEAD]))
        finish(tile, ahead.pop(0))


def _attention(q, k, v, z_a, val, rowm):
    b, s, _ = q.shape
    rows = s // GRID_W
    n_blk = rows // Q_ROWS
    n_kv = rows // KV_BLK_ROWS
    tq = Q_ROWS * GRID_W
    tkv = KV_BLK_ROWS * GRID_W

    def kv_spec(j):
        first = -(WIN_H // 2) // KV_BLK_ROWS
        return pl.BlockSpec(
            (None, tkv, W_ATTN),
            lambda bi, i: (bi, jnp.clip(i * (Q_ROWS // KV_BLK_ROWS) + first + j, 0, n_kv - 1), 0))

    def case(i):
        return jnp.where(i == 0, 0, jnp.where(i == n_blk - 1, 2, 1))

    blk = pl.BlockSpec((None, tq, W_ATTN), lambda bi, i: (bi, i, 0))
    return pl.pallas_call(
        _attn_kernel,
        grid=(b, n_blk),
        in_specs=[blk] + [kv_spec(j) for j in range(N_KV_BLKS)] * 2 + [
            blk,
            pl.BlockSpec(val.shape, lambda bi, i: (0, 0, 0, 0, 0)),
            pl.BlockSpec((None,) + rowm.shape[1:], lambda bi, i: (case(i), 0, 0, 0, 0))],
        out_specs=blk,
        out_shape=jax.ShapeDtypeStruct((b, s, W_ATTN), _bf16),
        scratch_shapes=[pltpu.VMEM((N_KV_BLKS * tkv, W_ATTN), _f32)] * 2,
        compiler_params=pltpu.CompilerParams(dimension_semantics=("arbitrary", "arbitrary"),
                                             vmem_limit_bytes=VMEM_LIMIT),
        name="attn",
    )(q, *([k] * N_KV_BLKS), *([v] * N_KV_BLKS), z_a, val, rowm)


def _back_kernel(x_ref, ya_ref, hf_ref, hb0_ref, pb_ref, zb_ref, ga_ref, gb_ref, wba_ref, wbb_ref,
                 wo_ref, g_ref, b_ref, o_ref, carry_ref, *, alpha, blocks_per_seq):
    i = pl.program_id(0)

    @pl.when(i % blocks_per_seq == 0)
    def _():
        carry_ref[...] = jnp.zeros_like(carry_ref)

    carry = carry_ref[0:1, :]
    cin = [None] * (TB // TM)
    for sb in range(TB // TM - 1, -1, -1):
        cin[sb] = carry
        r0 = sb * TM
        carry = hb0_ref[r0:r0 + 1, :].astype(_f32) + pb_ref[r0:r0 + 1, :].astype(_f32) * carry
    carry_ref[0:1, :] = carry

    def rows_of(c):
        return slice(c * BACK_ROWS, (c + 1) * BACK_ROWS)

    def branch_dots(c):
        rows = rows_of(c)
        h_b = hb0_ref[rows, :].astype(_f32) + pb_ref[rows, :].astype(_f32) * cin[c * BACK_ROWS // TM]
        yb = ((hf_ref[rows, :].astype(_f32) + h_b) * zb_ref[rows, :].astype(_f32)).astype(_bf16)
        return _dot(ya_ref[rows, :], wba_ref[...]), _dot(yb, wbb_ref[...])

    def out_dot(c, pa, pb):
        rows = rows_of(c)
        m = ga_ref[rows, :].astype(_f32) * pa + gb_ref[rows, :].astype(_f32) * pb
        return _dot(m.astype(_bf16), wo_ref[...])

    def finish(c, out):
        rows = rows_of(c)
        o_ref[rows, :] = _layer_norm(alpha * x_ref[rows, :] + out, g_ref[...], b_ref[...])

    n_c = TB // BACK_ROWS
    branch = {0: branch_dots(0)}
    outs = {}
    for c in range(n_c):
        if c + 1 < n_c:
            branch[c + 1] = branch_dots(c + 1)
        outs[c] = out_dot(c, *branch.pop(c))
        if c >= 1:
            finish(c - 1, outs.pop(c - 1))
    finish(n_c - 1, outs.pop(n_c - 1))


def _back(x2, seq_len, y_a, h_f, h_b0, p_b, z_b, g_a, g_b, w_ba, w_bb, w_out, ln_g, ln_b, alpha):
    t = x2.shape[0]
    n = t // TB
    rev = lambda i: (n - 1 - i, 0)
    const = lambda i: (0, 0)
    rows = lambda w: pl.BlockSpec((TB, w), rev)
    full = lambda a: pl.BlockSpec(a.shape, const)
    return pl.pallas_call(
        functools.partial(_back_kernel, alpha=alpha, blocks_per_seq=seq_len // TB),
        grid=(n,),
        in_specs=[rows(D_MODEL), rows(W_ATTN), rows(W_LRU), rows(W_LRU), rows(W_LRU), rows(W_LRU),
                  rows(D_MODEL), rows(D_MODEL), full(w_ba), full(w_bb), full(w_out), full(ln_g), full(ln_b)],
        out_specs=rows(D_MODEL),
        out_shape=jax.ShapeDtypeStruct((t, D_MODEL), _f32),
        scratch_shapes=[pltpu.VMEM((SUBLANES, W_LRU), _f32)],
        compiler_params=pltpu.CompilerParams(dimension_semantics=("arbitrary",),
                                             vmem_limit_bytes=VMEM_LIMIT),
        name="back",
    )(x2, y_a, h_f, h_b0, p_b, z_b, g_a, g_b, w_ba, w_bb, w_out, ln_g, ln_b)


def kernel(x, emb_ln_g, emb_ln_b, w_in, rpb, conv_w, conv_b, lru_gate_w, lru_gate_b, lru_lambda,
           w_branch_attn, w_branch_lru, b_merge, w_out, ln_g, ln_b):
    b, s, d = x.shape
    depth = w_in.shape[0]
    assert d == D_MODEL and s % (Q_ROWS * GRID_W) == 0 and s % TM == 0 and s % TB == 0 and TB % TM == 0
    alpha = (2 * depth) ** 0.25
    t = b * s
    h = x.reshape(t, d)
    row2 = lambda a: a.reshape(1, -1)
    rowm = _attn_row_mask(s // GRID_W)
    for l in range(depth):
        outs = _front(h, s, row2(emb_ln_g), row2(emb_ln_b), w_in[l].astype(_bf16), b_merge[l], conv_w[l],
                      conv_b[l], lru_gate_w[l], lru_gate_b[l], lru_lambda[l], apply_ln=(l == 0))
        if l == 0:
            h, outs = outs[0], outs[1:]
        q, k, v, z_a, z_b, g_a, g_b, h_f, h_b0, p_b = outs
        seq = lambda a: a.reshape(b, s, a.shape[-1])
        y_a = _attention(seq(q), seq(k), seq(v), seq(z_a), _attn_value_table(rpb[l]), rowm)
        h = _back(h, s, y_a.reshape(t, W_ATTN), h_f, h_b0, p_b, z_b, g_a, g_b,
                  w_branch_attn[l].astype(_bf16), w_branch_lru[l].astype(_bf16),
                  w_out[l].astype(_bf16), row2(ln_g[l]), row2(ln_b[l]), alpha)
    return h.reshape(b, s, d)
```

```python
import functools

import numpy as np
import jax
import jax.numpy as jnp
from jax import lax
from jax.experimental import pallas as pl
from jax.experimental.pallas import tpu as pltpu

D_MODEL = 1024
GRID_W = 64
N_HEADS = 8
HEAD_DIM = 64
W_ATTN = N_HEADS * HEAD_DIM
WIN_H = 8
WIN_W = 16
W_LRU = 512
N_BLOCKS = 8
BLOCK_W = 64
CONV_W = 4
C_LRU = 8.0
LN_EPS = 1e-5
NEG = -1e30
TINY = 1e-30
LOG2_E = 1.4426950408889634

LANES = 128
SUBLANES = 8
VMEM_LIMIT = 56 * 1024 * 1024

TM = 512
TB = 1024
BACK_ROWS = 256
Q_ROWS = 16
SUB_ROWS = 4
KEY_ROWS = SUB_ROWS + WIN_H
KV_BLK_ROWS = 4
N_KV_BLKS = (Q_ROWS + WIN_H) // KV_BLK_ROWS
QC = 32
KC = 40
KC_START = (0, GRID_W - KC)
KEYS_PAD = 512
SCORE_LOOKAHEAD = 1
N_CHUNK = SUBLANES
G_LEN = TM // N_CHUNK
PITCH = G_LEN + 8
N_LG = W_LRU // LANES
PROJ_COLS = 256
GATE_SLABS = 16

_f32 = jnp.float32
_bf16 = jnp.bfloat16


def _dot(a, b):
    return jnp.dot(a, b, preferred_element_type=_f32)


def _layer_norm(x, g, b):
    mu = jnp.mean(x, axis=-1, keepdims=True)
    xc = x - mu
    var = jnp.mean(xc * xc, axis=-1, keepdims=True)
    return xc * lax.rsqrt(var + LN_EPS) * g + b


def _sigmoid(x):
    return 0.5 + 0.5 * jnp.tanh(0.5 * x)


def _silu(x):
    hx = 0.5 * x
    return hx + hx * jnp.tanh(hx)


def _stage_time_major(u, ut_ref):
    for ch in range(N_CHUNK):
        for lg in range(N_LG):
            ut_ref[lg, ch * PITCH:ch * PITCH + G_LEN, :] = u[ch * G_LEN:(ch + 1) * G_LEN,
                                                            lg * LANES:(lg + 1) * LANES]


def _load_slab(ut_ref, g):
    return jnp.concatenate([ut_ref[lg, pl.ds(g, N_CHUNK, stride=PITCH), :] for lg in range(N_LG)], axis=1)


def _slab_rows(g):
    return slice(g * N_CHUNK, (g + 1) * N_CHUNK)


def _unstage_slabs(slab_fn, ot_ref, g_range):
    for g in g_range:
        slab = slab_fn(g)
        for lg in range(N_LG):
            ot_ref[lg, pl.ds(g, N_CHUNK, stride=PITCH), :] = slab[:, lg * LANES:(lg + 1) * LANES]


def _copy_time_major(ot_ref, o_ref, ch_range):
    for ch in ch_range:
        for lg in range(N_LG):
            o_ref[ch * G_LEN:(ch + 1) * G_LEN, lg * LANES:(lg + 1) * LANES] = (
                ot_ref[lg, ch * PITCH:ch * PITCH + G_LEN, :].astype(o_ref.dtype))


def _gate_dots(ub_ref, wg_ref):
    half = W_LRU // 2
    pre = []
    for gi in range(2):
        cols = [_dot(ub_ref[:, hf * half:(hf + 1) * half], wg_ref[gi, hf]) for hf in range(2)]
        pre.append(jnp.concatenate(cols, axis=1))
    return pre


def _gates(pre, hu_ref, bg_ref, coef, first, g_first, a_ref, bx_ref):
    t_r = jnp.tanh(pre[0] + bg_ref[0])
    t_i = jnp.tanh(pre[1] + bg_ref[1])
    a = jnp.exp2(coef + coef * t_r)
    one_m_a2 = (1.0 - a) * (1.0 + a)
    mult = one_m_a2 * lax.rsqrt(jnp.maximum(one_m_a2, TINY))
    hu = hu_ref[...]
    iu = hu + hu * t_i
    bx = mult * iu
    a_ref[...] = a
    bx_ref[...] = bx
    fix = _slab_rows(g_first)
    bx_ref[fix, :] = jnp.where(first, iu[fix], bx[fix])


def _chunk_scan(a_ref, bx_ref, hl_ref, pl_ref, g_range):
    h = jnp.zeros((N_CHUNK, W_LRU), _f32)
    p = jnp.ones((N_CHUNK, W_LRU), _f32)
    for g in g_range:
        rows = _slab_rows(g)
        ag = a_ref[rows, :]
        h = ag * h + bx_ref[rows, :]
        p = ag * p
        hl_ref[rows, :] = h
        pl_ref[rows, :] = p
    return h, p


def _front_kernel(x_ref, xn_ref, g_ref, b_ref, w_ref, bm_ref, cw_ref, cb_ref, wg_ref, bg_ref, lam_ref,
                  *refs, apply_ln, blocks_per_seq):
    n_out = 11 if apply_ln else 10
    out_refs = refs[:n_out]
    (uprev_ref, carry_ref, xb_ref, hu_ref, ub_ref, a_ref, bx_ref, hl_ref, pl_ref,
     ut_ref, ot_f, ot_h, ot_p) = refs[n_out:]
    if apply_ln:
        h_ref, out_refs = out_refs[0], out_refs[1:]
    q_ref, k_ref, v_ref, za_ref, zb_ref, ga_ref, gb_ref, hf_ref, hb0_ref, pb_ref = out_refs
    j = pl.program_id(0)
    is_first_blk = j % blocks_per_seq == 0
    is_last_blk = j % blocks_per_seq == blocks_per_seq - 1
    u_off = 4 * W_ATTN
    g_off = u_off + 2 * W_LRU
    sub = lax.broadcasted_iota(jnp.int32, (N_CHUNK, W_LRU), 0)

    def edge(row):
        return jnp.broadcast_to(row, (N_CHUNK, W_LRU))

    def proj_chunks():
        def chunk(o_ref, off, post, c0):
            def run():
                y = _dot(xb_ref[...], w_ref[:, off + c0:off + c0 + PROJ_COLS])
                o_ref[:, c0:c0 + PROJ_COLS] = post(y, slice(c0, c0 + PROJ_COLS)).astype(o_ref.dtype)
            return run
        plan = ((q_ref, 0, lambda y, cols: y * (HEAD_DIM ** -0.5 * LOG2_E)),
                (k_ref, W_ATTN, lambda y, cols: y),
                (v_ref, 2 * W_ATTN, lambda y, cols: y),
                (za_ref, 3 * W_ATTN, lambda y, cols: _silu(y)),
                (zb_ref, u_off + W_LRU, lambda y, cols: _silu(y)),
                (ga_ref, g_off, lambda y, cols: _sigmoid(y + bm_ref[0:1, cols])),
                (gb_ref, g_off + D_MODEL, lambda y, cols: _sigmoid(y + bm_ref[1:2, cols])))
        return [chunk(o_ref, off, post, c0) for o_ref, off, post in plan
                for c0 in range(0, o_ref.shape[-1], PROJ_COLS)]

    @pl.when(is_first_blk)
    def _():
        uprev_ref[...] = jnp.zeros_like(uprev_ref)
        carry_ref[...] = jnp.zeros_like(carry_ref)

    x = x_ref[...]
    xn = xn_ref[...]
    if apply_ln:
        x = _layer_norm(x, g_ref[...], b_ref[...])
        xn = _layer_norm(xn, g_ref[...], b_ref[...])
        h_ref[...] = x
    xb_ref[...] = x.astype(_bf16)
    u = _dot(xb_ref[...], w_ref[:, u_off:u_off + W_LRU])
    u_next = _dot(xn.astype(_bf16), w_ref[:, u_off:u_off + W_LRU])
    _stage_time_major(u, ut_ref)

    slabs = {}
    slabs[-2] = jnp.where(sub == 0, edge(uprev_ref[SUBLANES - 2:SUBLANES - 1, :]),
                          pltpu.roll(_load_slab(ut_ref, G_LEN - 2), 1, axis=0))
    slabs[-1] = jnp.where(sub == 0, edge(uprev_ref[SUBLANES - 1:SUBLANES, :]),
                          pltpu.roll(_load_slab(ut_ref, G_LEN - 1), 1, axis=0))
    nxt = jnp.where(is_last_blk, 0.0, u_next[0:1, :])
    slabs[G_LEN] = jnp.where(sub == N_CHUNK - 1, edge(nxt), pltpu.roll(_load_slab(ut_ref, 0), N_CHUNK - 1, axis=0))
    uprev_ref[...] = u[TM - SUBLANES:TM, :]

    def slab(g):
        if g not in slabs:
            slabs[g] = _load_slab(ut_ref, g)
        return slabs[g]

    cw = [cw_ref[i:i + 1, :] for i in range(CONV_W)]
    cb = cb_ref[...]
    neg_lam = -lam_ref[...]
    softplus = jnp.maximum(neg_lam, 0.0) + jnp.log(1.0 + jnp.exp(-jnp.abs(neg_lam)))
    coef = (-0.5 * C_LRU * LOG2_E) * softplus
    first = jnp.logical_and(is_first_blk, sub == 0)
    last = jnp.logical_and(is_last_blk, sub == N_CHUNK - 1)

    n_grp = G_LEN // GATE_SLABS
    grp_rows = lambda r: slice(r * GATE_SLABS * N_CHUNK, (r + 1) * GATE_SLABS * N_CHUNK)

    def conv_group(r):
        for g in range(r * GATE_SLABS, (r + 1) * GATE_SLABS):
            hu_ref[_slab_rows(g), :] = (slab(g - 2) * cw[0] + slab(g - 1) * cw[1] + slab(g) * cw[2]
                                        + slab(g + 1) * cw[3] + cb)
        ub_ref[grp_rows(r), :] = hu_ref[grp_rows(r), :].astype(_bf16)

    def gate_group(r):
        rows = grp_rows(r)
        half = W_LRU // 2
        hu = hu_ref[rows, :]
        for di in range(2):
            pre = [jnp.concatenate([_dot(ub_ref[rows, hf * half:(hf + 1) * half], wg_ref[di, gi, hf])
                                    for hf in range(2)], axis=1) for gi in range(2)]
            t_r = jnp.tanh(pre[0] + bg_ref[di, 0])
            t_i = jnp.tanh(pre[1] + bg_ref[di, 1])
            a = jnp.exp2(coef[di:di + 1] + coef[di:di + 1] * t_r)
            one_m_a2 = (1.0 - a) * (1.0 + a)
            mult = one_m_a2 * lax.rsqrt(jnp.maximum(one_m_a2, TINY))
            iu = hu + hu * t_i
            bx = mult * iu
            a_ref[di, rows, :] = a
            bx_ref[di, rows, :] = bx
            g_fix = (G_LEN - 1) if di else 0
            if r * GATE_SLABS <= g_fix < (r + 1) * GATE_SLABS:
                rel = slice((g_fix - r * GATE_SLABS) * N_CHUNK, (g_fix - r * GATE_SLABS + 1) * N_CHUNK)
                bx_ref[di, _slab_rows(g_fix), :] = jnp.where(last if di else first, iu[rel], bx[rel])

    def scan_slabs(di, state, g_range):
        h, p = state
        for g in g_range:
            rows = _slab_rows(g)
            ag = a_ref[di, rows, :]
            h = ag * h + bx_ref[di, rows, :]
            p = ag * p
            hl_ref[rows, :] = h
            pl_ref[rows, :] = p
        return h, p

    chunks = proj_chunks()

    def emit(n):
        for _ in range(min(n, len(chunks))):
            chunks.pop(0)()
    zero_state = (jnp.zeros((N_CHUNK, W_LRU), _f32), jnp.ones((N_CHUNK, W_LRU), _f32))

    emit(2)
    conv_group(0)
    state = zero_state
    for r in range(n_grp):
        emit(2)
        if r + 1 < n_grp:
            conv_group(r + 1)
        gate_group(r)
        if r >= 1:
            state = scan_slabs(0, state, range((r - 1) * GATE_SLABS, r * GATE_SLABS))
    emit(2)
    h_end, p_end = scan_slabs(0, state, range((n_grp - 1) * GATE_SLABS, G_LEN))
    c = jnp.where(sub == 0, edge(carry_ref[N_CHUNK - 1:N_CHUNK, :]), 0.0)
    for kk in range(1, N_CHUNK):
        c = jnp.where(sub == kk, pltpu.roll(p_end * c + h_end, 1, axis=0), c)
    carry_ref[...] = p_end * c + h_end
    emit(2)
    _unstage_slabs(lambda g: pl_ref[_slab_rows(g), :] * c + hl_ref[_slab_rows(g), :], ot_f, range(G_LEN))
    _copy_time_major(ot_f, hf_ref, range(N_CHUNK))

    emit(2)
    h_end, p_end = scan_slabs(1, zero_state, range(G_LEN - 1, -1, -1))
    cz = jnp.zeros((N_CHUNK, W_LRU), _f32)
    d = jnp.where(sub == N_CHUNK - 1, 1.0, 0.0)
    for kk in range(N_CHUNK - 2, -1, -1):
        cz = jnp.where(sub == kk, pltpu.roll(p_end * cz + h_end, N_CHUNK - 1, axis=0), cz)
        d = jnp.where(sub == kk, pltpu.roll(p_end * d, N_CHUNK - 1, axis=0), d)
    emit(2)
    _unstage_slabs(lambda g: pl_ref[_slab_rows(g), :] * cz + hl_ref[_slab_rows(g), :], ot_h, range(G_LEN))
    emit(2)
    _unstage_slabs(lambda g: pl_ref[_slab_rows(g), :] * d, ot_p, range(G_LEN))
    _copy_time_major(ot_h, hb0_ref, range(N_CHUNK))
    _copy_time_major(ot_p, pb_ref, range(N_CHUNK))
    emit(len(chunks))


def _block_diag_halves(w):
    per_half = N_BLOCKS // 2
    eye = jnp.eye(per_half, dtype=w.dtype)
    w = w.reshape(w.shape[:-3] + (2, per_half, BLOCK_W, BLOCK_W))
    full = jnp.einsum('...hbde,bc->...hbdce', w, eye)
    return full.reshape(w.shape[:-3] + (per_half * BLOCK_W, per_half * BLOCK_W)).astype(_bf16)


def _front(x2, seq_len, ln_g, ln_b, w_in_bf16, b_merge, conv_w, conv_b, gate_w, gate_b, lam, apply_ln):
    t = x2.shape[0]
    n = t // TM
    per = TM // SUBLANES
    row = lambda i: (i, 0)
    full = lambda a: pl.BlockSpec(a.shape, lambda i: (0,) * a.ndim)
    wg = _block_diag_halves(gate_w)
    bg = (0.5 * gate_b).reshape(2, 2, 1, W_LRU)
    conv_w = 0.5 * conv_w
    cb2 = (0.5 * conv_b).reshape(1, W_LRU)
    widths = (W_ATTN, W_ATTN, W_ATTN, W_ATTN, W_LRU, D_MODEL, D_MODEL, W_LRU, W_LRU, W_LRU)
    out_shape = [jax.ShapeDtypeStruct((t, w), _bf16) for w in widths]
    out_specs = [pl.BlockSpec((TM, w), row) for w in widths]
    if apply_ln:
        out_shape = [jax.ShapeDtypeStruct((t, D_MODEL), _f32)] + out_shape
        out_specs = [pl.BlockSpec((TM, D_MODEL), row)] + out_specs
    stage = pltpu.VMEM((N_LG, N_CHUNK * PITCH, LANES), _f32)
    block_f32 = pltpu.VMEM((TM, W_LRU), _f32)
    return pl.pallas_call(
        functools.partial(_front_kernel, apply_ln=apply_ln, blocks_per_seq=seq_len // TM),
        grid=(n,),
        in_specs=[pl.BlockSpec((TM, D_MODEL), row),
                  pl.BlockSpec((SUBLANES, D_MODEL), lambda i: (jnp.minimum((i + 1) * per, n * per - 1), 0)),
                  full(ln_g), full(ln_b), full(w_in_bf16), full(b_merge), full(conv_w), full(cb2),
                  full(wg), full(bg), full(lam)],
        out_specs=out_specs,
        out_shape=out_shape,
        scratch_shapes=[pltpu.VMEM((SUBLANES, W_LRU), _f32), pltpu.VMEM((N_CHUNK, W_LRU), _f32),
                        pltpu.VMEM((TM, D_MODEL), _bf16), block_f32, pltpu.VMEM((TM, W_LRU), _bf16),
                        pltpu.VMEM((2, TM, W_LRU), _f32), pltpu.VMEM((2, TM, W_LRU), _f32),
                        block_f32, block_f32, stage, stage, stage, stage],
        compiler_params=pltpu.CompilerParams(dimension_semantics=("arbitrary",),
                                             vmem_limit_bytes=VMEM_LIMIT),
        name="front",
    )(x2, x2, ln_g, ln_b, w_in_bf16, b_merge, conv_w, cb2, wg, bg, lam)


def _attn_value_table(rpb):
    qc = np.arange(GRID_W)[:, None]
    kc = np.arange(GRID_W)[None, :]
    cs = np.clip(qc - WIN_W // 2, 0, GRID_W - WIN_W)
    ok = (kc >= cs) & (kc < cs + WIN_W)
    pad = GRID_W - WIN_W
    padded = jnp.pad(rpb.astype(_f32), ((0, 0), (0, 0), (pad, pad)))
    skew = jnp.tile(padded, (1, 1, GRID_W + 1))[:, :, :GRID_W * 2 * GRID_W]
    skew = skew.reshape(skew.shape[:2] + (GRID_W, 2 * GRID_W))
    t = skew[:, :, ::-1, :GRID_W]
    t = jnp.where(ok, t * LOG2_E, NEG)
    tiles = []
    for ct, k0 in enumerate(KC_START):
        for qr in range(SUB_ROWS):
            a0 = WIN_H // 2 - 1 - qr
            sub = t[:, a0:a0 + KEY_ROWS, ct * QC:(ct + 1) * QC, k0:k0 + KC]
            sub = sub.transpose(0, 2, 1, 3).reshape(t.shape[0], QC, KEY_ROWS * KC)
            tiles.append(jnp.pad(sub, ((0, 0), (0, 0), (0, KEYS_PAD - KEY_ROWS * KC)), constant_values=NEG))
    return jnp.stack(tiles, axis=1).reshape(t.shape[0], len(KC_START), SUB_ROWS, QC, KEYS_PAD)


def _attn_row_mask(rows):
    r_first = (0, Q_ROWS, rows - Q_ROWS)
    m = np.full((3, Q_ROWS // SUB_ROWS, SUB_ROWS, SUBLANES, KEYS_PAD), NEG, np.float32)
    m[..., KEY_ROWS * KC:] = 0.0
    for v, r0 in enumerate(r_first):
        for half in range(Q_ROWS // SUB_ROWS):
            for qr in range(SUB_ROWS):
                r = r0 + half * SUB_ROWS + qr
                rs = min(max(r - WIN_H // 2, 0), rows - WIN_H)
                for kr in range(KEY_ROWS):
                    key_r = r0 + half * SUB_ROWS - WIN_H // 2 + kr
                    if rs <= key_r < rs + WIN_H:
                        m[v, half, qr, :, kr * KC:(kr + 1) * KC] = 0.0
    return m


def _attn_kernel(q_ref, *refs):
    k_refs, v_refs = refs[:N_KV_BLKS], refs[N_KV_BLKS:2 * N_KV_BLKS]
    z_ref, val_ref, rowm_ref, o_ref, k32_ref, v32_ref = refs[2 * N_KV_BLKS:]
    m_q = SUB_ROWS * QC
    blk_tok = KV_BLK_ROWS * GRID_W
    lane = lax.broadcasted_iota(jnp.int32, (m_q, LANES), 1)
    even = lane < HEAD_DIM
    for j in range(N_KV_BLKS):
        k32_ref[j * blk_tok:(j + 1) * blk_tok, :] = k_refs[j][...].astype(_f32)
        v32_ref[j * blk_tok:(j + 1) * blk_tok, :] = v_refs[j][...].astype(_f32)
    pad = jnp.zeros((KEYS_PAD - KEY_ROWS * KC, LANES), _f32)

    def key_tile(src_ref, half, k0, cols):
        parts = [src_ref[(half * SUB_ROWS + kr) * GRID_W + k0:(half * SUB_ROWS + kr) * GRID_W + k0 + KC, cols]
                 for kr in range(KEY_ROWS)]
        return jnp.concatenate(parts + [pad], axis=0).astype(_bf16)

    def q_rows(half, ct):
        return [slice((half * SUB_ROWS + qr) * GRID_W + ct * QC, (half * SUB_ROWS + qr) * GRID_W + (ct + 1) * QC)
                for qr in range(SUB_ROWS)]

    def scores(tile):
        half, ct, pair = tile
        cols = slice(pair * LANES, (pair + 1) * LANES)
        kt = key_tile(k32_ref, half, KC_START[ct], cols)
        qp = jnp.concatenate([q_ref[r, cols] for r in q_rows(half, ct)], axis=0)
        zero = jnp.zeros_like(qp)
        q2 = jnp.concatenate([jnp.where(even, qp, zero), jnp.where(even, zero, qp)], axis=0)
        return lax.dot_general(q2, kt, (((1,), (1,)), ((), ())), preferred_element_type=_f32)

    def finish(tile, s):
        half, ct, pair = tile
        cols = slice(pair * LANES, (pair + 1) * LANES)
        blocks = []
        for parity in range(2):
            for qr in range(SUB_ROWS):
                r0 = (parity * SUB_ROWS + qr) * QC
                blk = s[r0:r0 + QC] + val_ref[2 * pair + parity, ct, qr]
                blk = blk.reshape(QC // SUBLANES, SUBLANES, -1) + rowm_ref[half, qr][None]
                blocks.append(blk.reshape(QC, -1))
        s = jnp.concatenate(blocks, axis=0)
        m = jnp.max(s, axis=-1, keepdims=True)
        e = jnp.exp2(s - m)
        l = jnp.sum(e, axis=-1, keepdims=True)
        o2 = _dot(e.astype(_bf16), key_tile(v32_ref, half, KC_START[ct], cols))
        o = jnp.where(even, o2[:m_q] / l[:m_q], o2[m_q:] / l[m_q:])
        z = jnp.concatenate([z_ref[r, cols] for r in q_rows(half, ct)], axis=0).astype(_f32)
        y = (o * z).astype(o_ref.dtype)
        for qr, r in enumerate(q_rows(half, ct)):
            o_ref[r, cols] = y[qr * QC:(qr + 1) * QC]

    tiles = [(half, ct, pair) for half in range(Q_ROWS // SUB_ROWS) for ct in range(len(KC_START))
             for pair in range(N_HEADS // 2)]
    ahead = [scores(tile) for tile in tiles[:SCORE_LOOKAHEAD]]
    for t, tile in enumerate(tiles):
        if t + SCORE_LOOKAHEAD < len(tiles):
            ahead.append(scores(tiles[t + SCORE_LOOKAHEAD]))
        finish(tile, ahead.pop(0))


def _attention(q, k, v, z_a, val, rowm):
    b, s, _ = q.shape
    rows = s // GRID_W
    n_blk = rows // Q_ROWS
    n_kv = rows // KV_BLK_ROWS
    tq = Q_ROWS * GRID_W
    tkv = KV_BLK_ROWS * GRID_W

    def kv_spec(j):
        first = -(WIN_H // 2) // KV_BLK_ROWS
        return pl.BlockSpec(
            (None, tkv, W_ATTN),
            lambda bi, i: (bi, jnp.clip(i * (Q_ROWS // KV_BLK_ROWS) + first + j, 0, n_kv - 1), 0))

    def case(i):
        return jnp.where(i == 0, 0, jnp.where(i == n_blk - 1, 2, 1))

    blk = pl.BlockSpec((None, tq, W_ATTN), lambda bi, i: (bi, i, 0))
    return pl.pallas_call(
        _attn_kernel,
        grid=(b, n_blk),
        in_specs=[blk] + [kv_spec(j) for j in range(N_KV_BLKS)] * 2 + [
            blk,
            pl.BlockSpec(val.shape, lambda bi, i: (0, 0, 0, 0, 0)),
            pl.BlockSpec((None,) + rowm.shape[1:], lambda bi, i: (case(i), 0, 0, 0, 0))],
        out_specs=blk,
        out_shape=jax.ShapeDtypeStruct((b, s, W_ATTN), _bf16),
        scratch_shapes=[pltpu.VMEM((N_KV_BLKS * tkv, W_ATTN), _f32)] * 2,
        compiler_params=pltpu.CompilerParams(dimension_semantics=("arbitrary", "arbitrary"),
                                             vmem_limit_bytes=VMEM_LIMIT),
        name="attn",
    )(q, *([k] * N_KV_BLKS), *([v] * N_KV_BLKS), z_a, val, rowm)


def _back_kernel(x_ref, ya_ref, hf_ref, hb0_ref, pb_ref, zb_ref, ga_ref, gb_ref, wba_ref, wbb_ref,
                 wo_ref, g_ref, b_ref, o_ref, carry_ref, *, alpha, blocks_per_seq):
    i = pl.program_id(0)

    @pl.when(i % blocks_per_seq == 0)
    def _():
        carry_ref[...] = jnp.zeros_like(carry_ref)

    carry = carry_ref[0:1, :]
    cin = [None] * (TB // TM)
    for sb in range(TB // TM - 1, -1, -1):
        cin[sb] = carry
        r0 = sb * TM
        carry = hb0_ref[r0:r0 + 1, :].astype(_f32) + pb_ref[r0:r0 + 1, :].astype(_f32) * carry
    carry_ref[0:1, :] = carry

    def rows_of(c):
        return slice(c * BACK_ROWS, (c + 1) * BACK_ROWS)

    def branch_dots(c):
        rows = rows_of(c)
        h_b = hb0_ref[rows, :].astype(_f32) + pb_ref[rows, :].astype(_f32) * cin[c * BACK_ROWS // TM]
        yb = ((hf_ref[rows, :].astype(_f32) + h_b) * zb_ref[rows, :].astype(_f32)).astype(_bf16)
        return _dot(ya_ref[rows, :], wba_ref[...]), _dot(yb, wbb_ref[...])

    def out_dot(c, pa, pb):
        rows = rows_of(c)
        m = ga_ref[rows, :].astype(_f32) * pa + gb_ref[rows, :].astype(_f32) * pb
        return _dot(m.astype(_bf16), wo_ref[...])

    def finish(c, out):
        rows = rows_of(c)
        o_ref[rows, :] = _layer_norm(alpha * x_ref[rows, :] + out, g_ref[...], b_ref[...])

    n_c = TB // BACK_ROWS
    branch = {0: branch_dots(0)}
    outs = {}
    for c in range(n_c):
        if c + 1 < n_c:
            branch[c + 1] = branch_dots(c + 1)
        outs[c] = out_dot(c, *branch.pop(c))
        if c >= 1:
            finish(c - 1, outs.pop(c - 1))
    finish(n_c - 1, outs.pop(n_c - 1))


def _back(x2, seq_len, y_a, h_f, h_b0, p_b, z_b, g_a, g_b, w_ba, w_bb, w_out, ln_g, ln_b, alpha):
    t = x2.shape[0]
    n = t // TB
    rev = lambda i: (n - 1 - i, 0)
    const = lambda i: (0, 0)
    rows = lambda w: pl.BlockSpec((TB, w), rev)
    full = lambda a: pl.BlockSpec(a.shape, const)
    return pl.pallas_call(
        functools.partial(_back_kernel, alpha=alpha, blocks_per_seq=seq_len // TB),
        grid=(n,),
        in_specs=[rows(D_MODEL), rows(W_ATTN), rows(W_LRU), rows(W_LRU), rows(W_LRU), rows(W_LRU),
                  rows(D_MODEL), rows(D_MODEL), full(w_ba), full(w_bb), full(w_out), full(ln_g), full(ln_b)],
        out_specs=rows(D_MODEL),
        out_shape=jax.ShapeDtypeStruct((t, D_MODEL), _f32),
        scratch_shapes=[pltpu.VMEM((SUBLANES, W_LRU), _f32)],
        compiler_params=pltpu.CompilerParams(dimension_semantics=("arbitrary",),
                                             vmem_limit_bytes=VMEM_LIMIT),
        name="back",
    )(x2, y_a, h_f, h_b0, p_b, z_b, g_a, g_b, w_ba, w_bb, w_out, ln_g, ln_b)


def kernel(x, emb_ln_g, emb_ln_b, w_in, rpb, conv_w, conv_b, lru_gate_w, lru_gate_b, lru_lambda,
           w_branch_attn, w_branch_lru, b_merge, w_out, ln_g, ln_b):
    b, s, d = x.shape
    depth = w_in.shape[0]
    assert d == D_MODEL and s % (Q_ROWS * GRID_W) == 0 and s % TM == 0 and s % TB == 0 and TB % TM == 0
    alpha = (2 * depth) ** 0.25
    t = b * s
    h = x.reshape(t, d)
    row2 = lambda a: a.reshape(1, -1)
    rowm = _attn_row_mask(s // GRID_W)
    for l in range(depth):
        outs = _front(h, s, row2(emb_ln_g), row2(emb_ln_b), w_in[l].astype(_bf16), b_merge[l], conv_w[l],
                      conv_b[l], lru_gate_w[l], lru_gate_b[l], lru_lambda[l], apply_ln=(l == 0))
        if l == 0:
            h, outs = outs[0], outs[1:]
        q, k, v, z_a, z_b, g_a, g_b, h_f, h_b0, p_b = outs
        seq = lambda a: a.reshape(b, s, a.shape[-1])
        y_a = _attention(seq(q), seq(k), seq(v), seq(z_a), _attn_value_table(rpb[l]), rowm)
        h = _back(h, s, y_a.reshape(t, W_ATTN), h_f, h_b0, p_b, z_b, g_a, g_b,
                  w_branch_attn[l].astype(_bf16), w_branch_lru[l].astype(_bf16),
                  w_out[l].astype(_bf16), row2(ln_g[l]), row2(ln_b[l]), alpha)
    return h.reshape(b, s, d)
```

```python
import functools

import numpy as np
import jax
import jax.numpy as jnp
from jax import lax
from jax.experimental import pallas as pl
from jax.experimental.pallas import tpu as pltpu

D_MODEL = 1024
GRID_W = 64
N_HEADS = 8
HEAD_DIM = 64
W_ATTN = N_HEADS * HEAD_DIM
WIN_H = 8
WIN_W = 16
W_LRU = 512
N_BLOCKS = 8
BLOCK_W = 64
CONV_W = 4
C_LRU = 8.0
LN_EPS = 1e-5
NEG = -1e30
TINY = 1e-30
LOG2_E = 1.4426950408889634

LANES = 128
SUBLANES = 8
VMEM_LIMIT = 56 * 1024 * 1024

TM = 512
TB = 1024
BACK_ROWS = 256
Q_ROWS = 16
SUB_ROWS = 4
KEY_ROWS = SUB_ROWS + WIN_H
KV_BLK_ROWS = 4
N_KV_BLKS = (Q_ROWS + WIN_H) // KV_BLK_ROWS
QC = 32
KC = 40
KC_START = (0, GRID_W - KC)
KEYS_PAD = 512
SCORE_LOOKAHEAD = 1
N_CHUNK = SUBLANES
G_LEN = TM // N_CHUNK
PITCH = G_LEN + 8
N_LG = W_LRU // LANES
PROJ_COLS = 256
GATE_SLABS = 16

_f32 = jnp.float32
_bf16 = jnp.bfloat16


def _dot(a, b):
    return jnp.dot(a, b, preferred_element_type=_f32)


def _layer_norm(x, g, b):
    mu = jnp.mean(x, axis=-1, keepdims=True)
    xc = x - mu
    var = jnp.mean(xc * xc, axis=-1, keepdims=True)
    return xc * lax.rsqrt(var + LN_EPS) * g + b


def _sigmoid(x):
    return 0.5 + 0.5 * jnp.tanh(0.5 * x)


def _silu(x):
    hx = 0.5 * x
    return hx + hx * jnp.tanh(hx)


def _stage_time_major(u, ut_ref):
    for ch in range(N_CHUNK):
        for lg in range(N_LG):
            ut_ref[lg, ch * PITCH:ch * PITCH + G_LEN, :] = u[ch * G_LEN:(ch + 1) * G_LEN,
                                                            lg * LANES:(lg + 1) * LANES]


def _load_slab(ut_ref, g):
    return jnp.concatenate([ut_ref[lg, pl.ds(g, N_CHUNK, stride=PITCH), :] for lg in range(N_LG)], axis=1)


def _slab_rows(g):
    return slice(g * N_CHUNK, (g + 1) * N_CHUNK)


def _unstage_slabs(slab_fn, ot_ref, g_range):
    for g in g_range:
        slab = slab_fn(g)
        for lg in range(N_LG):
            ot_ref[lg, pl.ds(g, N_CHUNK, stride=PITCH), :] = slab[:, lg * LANES:(lg + 1) * LANES]


def _copy_time_major(ot_ref, o_ref, ch_range):
    for ch in ch_range:
        for lg in range(N_LG):
            o_ref[ch * G_LEN:(ch + 1) * G_LEN, lg * LANES:(lg + 1) * LANES] = (
                ot_ref[lg, ch * PITCH:ch * PITCH + G_LEN, :].astype(o_ref.dtype))


def _gate_dots(ub_ref, wg_ref):
    half = W_LRU // 2
    pre = []
    for gi in range(2):
        cols = [_dot(ub_ref[:, hf * half:(hf + 1) * half], wg_ref[gi, hf]) for hf in range(2)]
        pre.append(jnp.concatenate(cols, axis=1))
    return pre


def _gates(pre, hu_ref, bg_ref, coef, first, g_first, a_ref, bx_ref):
    t_r = jnp.tanh(pre[0] + bg_ref[0])
    t_i = jnp.tanh(pre[1] + bg_ref[1])
    a = jnp.exp2(coef + coef * t_r)
    one_m_a2 = (1.0 - a) * (1.0 + a)
    mult = one_m_a2 * lax.rsqrt(jnp.maximum(one_m_a2, TINY))
    hu = hu_ref[...]
    iu = hu + hu * t_i
    bx = mult * iu
    a_ref[...] = a
    bx_ref[...] = bx
    fix = _slab_rows(g_first)
    bx_ref[fix, :] = jnp.where(first, iu[fix], bx[fix])


def _chunk_scan(a_ref, bx_ref, hl_ref, pl_ref, g_range):
    h = jnp.zeros((N_CHUNK, W_LRU), _f32)
    p = jnp.ones((N_CHUNK, W_LRU), _f32)
    for g in g_range:
        rows = _slab_rows(g)
        ag = a_ref[rows, :]
        h = ag * h + bx_ref[rows, :]
        p = ag * p
        hl_ref[rows, :] = h
        pl_ref[rows, :] = p
    return h, p


def _front_kernel(x_ref, xn_ref, g_ref, b_ref, w_ref, bm_ref, cw_ref, cb_ref, wg_ref, bg_ref, lam_ref,
                  *refs, apply_ln, blocks_per_seq):
    n_out = 11 if apply_ln else 10
    out_refs = refs[:n_out]
    (uprev_ref, carry_ref, xb2_ref, un_ref, hn_ref, hu_ref, ub_ref, a_ref, bx_ref, hl_ref, pl_ref,
     ut_ref, ot_f, ot_h, ot_p) = refs[n_out:]
    if apply_ln:
        h_ref, out_refs = out_refs[0], out_refs[1:]
    q_ref, k_ref, v_ref, za_ref, zb_ref, ga_ref, gb_ref, hf_ref, hb0_ref, pb_ref = out_refs
    j = pl.program_id(0)
    is_first_blk = j % blocks_per_seq == 0
    is_last_blk = j % blocks_per_seq == blocks_per_seq - 1
    u_off = 4 * W_ATTN
    g_off = u_off + 2 * W_LRU
    sub = lax.broadcasted_iota(jnp.int32, (N_CHUNK, W_LRU), 0)

    def edge(row):
        return jnp.broadcast_to(row, (N_CHUNK, W_LRU))

    def proj_chunks():
        def chunk(o_ref, off, post, c0):
            def run():
                y = _dot(xb2_ref[slot], w_ref[:, off + c0:off + c0 + PROJ_COLS])
                o_ref[:, c0:c0 + PROJ_COLS] = post(y, slice(c0, c0 + PROJ_COLS)).astype(o_ref.dtype)
            return run
        plan = ((q_ref, 0, lambda y, cols: y * (HEAD_DIM ** -0.5 * LOG2_E)),
                (k_ref, W_ATTN, lambda y, cols: y),
                (v_ref, 2 * W_ATTN, lambda y, cols: y),
                (za_ref, 3 * W_ATTN, lambda y, cols: _silu(y)),
                (zb_ref, u_off + W_LRU, lambda y, cols: _silu(y)),
                (ga_ref, g_off, lambda y, cols: _sigmoid(y + bm_ref[0:1, cols])),
                (gb_ref, g_off + D_MODEL, lambda y, cols: _sigmoid(y + bm_ref[1:2, cols])))
        return [chunk(o_ref, off, post, c0) for o_ref, off, post in plan
                for c0 in range(0, o_ref.shape[-1], PROJ_COLS)]

    @pl.when(is_first_blk)
    def _():
        uprev_ref[...] = jnp.zeros_like(uprev_ref)
        carry_ref[...] = jnp.zeros_like(carry_ref)

    slot = j % 2

    def prepare(src_ref, dst_slot):
        x = src_ref[...]
        if apply_ln:
            x = _layer_norm(x, g_ref[...], b_ref[...])
            hn_ref[...] = x
        xb2_ref[dst_slot] = x.astype(_bf16)
        return _dot(xb2_ref[dst_slot], w_ref[:, u_off:u_off + W_LRU])

    @pl.when(j == 0)
    def _():
        un_ref[...] = prepare(x_ref, 0)

    if apply_ln:
        h_ref[...] = hn_ref[...]
    u = un_ref[...]
    _stage_time_major(u, ut_ref)
    u_next = prepare(xn_ref, 1 - slot)
    un_ref[...] = u_next

    slabs = {}
    slabs[-2] = jnp.where(sub == 0, edge(uprev_ref[SUBLANES - 2:SUBLANES - 1, :]),
                          pltpu.roll(_load_slab(ut_ref, G_LEN - 2), 1, axis=0))
    slabs[-1] = jnp.where(sub == 0, edge(uprev_ref[SUBLANES - 1:SUBLANES, :]),
                          pltpu.roll(_load_slab(ut_ref, G_LEN - 1), 1, axis=0))
    nxt = jnp.where(is_last_blk, 0.0, u_next[0:1, :])
    slabs[G_LEN] = jnp.where(sub == N_CHUNK - 1, edge(nxt), pltpu.roll(_load_slab(ut_ref, 0), N_CHUNK - 1, axis=0))
    uprev_ref[...] = u[TM - SUBLANES:TM, :]

    def slab(g):
        if g not in slabs:
            slabs[g] = _load_slab(ut_ref, g)
        return slabs[g]

    cw = [cw_ref[i:i + 1, :] for i in range(CONV_W)]
    cb = cb_ref[...]
    neg_lam = -lam_ref[...]
    softplus = jnp.maximum(neg_lam, 0.0) + jnp.log(1.0 + jnp.exp(-jnp.abs(neg_lam)))
    coef = (-0.5 * C_LRU * LOG2_E) * softplus
    first = jnp.logical_and(is_first_blk, sub == 0)
    last = jnp.logical_and(is_last_blk, sub == N_CHUNK - 1)

    n_grp = G_LEN // GATE_SLABS
    grp_rows = lambda r: slice(r * GATE_SLABS * N_CHUNK, (r + 1) * GATE_SLABS * N_CHUNK)

    def conv_group(r):
        for g in range(r * GATE_SLABS, (r + 1) * GATE_SLABS):
            hu_ref[_slab_rows(g), :] = (slab(g - 2) * cw[0] + slab(g - 1) * cw[1] + slab(g) * cw[2]
                                        + slab(g + 1) * cw[3] + cb)
        ub_ref[grp_rows(r), :] = hu_ref[grp_rows(r), :].astype(_bf16)

    def gate_group(r, dirs):
        rows = grp_rows(r)
        half = W_LRU // 2
        hu = hu_ref[rows, :]
        for di in dirs:
            pre = [jnp.concatenate([_dot(ub_ref[rows, hf * half:(hf + 1) * half], wg_ref[di, gi, hf])
                                    for hf in range(2)], axis=1) for gi in range(2)]
            t_r = jnp.tanh(pre[0] + bg_ref[di, 0])
            t_i = jnp.tanh(pre[1] + bg_ref[di, 1])
            a = jnp.exp2(coef[di:di + 1] + coef[di:di + 1] * t_r)
            one_m_a2 = (1.0 - a) * (1.0 + a)
            mult = one_m_a2 * lax.rsqrt(jnp.maximum(one_m_a2, TINY))
            iu = hu + hu * t_i
            bx = mult * iu
            a_ref[di, rows, :] = a
            bx_ref[di, rows, :] = bx
            g_fix = (G_LEN - 1) if di else 0
            if r * GATE_SLABS <= g_fix < (r + 1) * GATE_SLABS:
                rel = slice((g_fix - r * GATE_SLABS) * N_CHUNK, (g_fix - r * GATE_SLABS + 1) * N_CHUNK)
                bx_ref[di, _slab_rows(g_fix), :] = jnp.where(last if di else first, iu[rel], bx[rel])

    def scan_slabs(di, state, g_range):
        h, p = state
        for g in g_range:
            rows = _slab_rows(g)
            ag = a_ref[di, rows, :]
            h = ag * h + bx_ref[di, rows, :]
            p = ag * p
            hl_ref[di, rows, :] = h
            pl_ref[di, rows, :] = p
        return h, p

    chunks = proj_chunks()

    def emit(n):
        for _ in range(min(n, len(chunks))):
            chunks.pop(0)()
    zero_state = (jnp.zeros((N_CHUNK, W_LRU), _f32), jnp.ones((N_CHUNK, W_LRU), _f32))

    grp_slabs = lambda r: range(r * GATE_SLABS, (r + 1) * GATE_SLABS)

    emit(2)
    conv_group(0)
    state = zero_state
    for r in range(n_grp):
        emit(2)
        if r + 1 < n_grp:
            conv_group(r + 1)
        gate_group(r, (0,))
        if r >= 1:
            state = scan_slabs(0, state, grp_slabs(r - 1))
    emit(1)
    h_end, p_end = scan_slabs(0, state, grp_slabs(n_grp - 1))
    c = jnp.where(sub == 0, edge(carry_ref[N_CHUNK - 1:N_CHUNK, :]), 0.0)
    for kk in range(1, N_CHUNK):
        c = jnp.where(sub == kk, pltpu.roll(p_end * c + h_end, 1, axis=0), c)
    carry_ref[...] = p_end * c + h_end

    state = zero_state
    for r in range(n_grp - 1, -1, -1):
        emit(1)
        gate_group(r, (1,))
        _unstage_slabs(lambda g: pl_ref[0, _slab_rows(g), :] * c + hl_ref[0, _slab_rows(g), :], ot_f, grp_slabs(r))
        if r + 1 < n_grp:
            state = scan_slabs(1, state, reversed(grp_slabs(r + 1)))
    _copy_time_major(ot_f, hf_ref, range(N_CHUNK))
    emit(1)
    h_end, p_end = scan_slabs(1, state, reversed(grp_slabs(0)))
    cz = jnp.zeros((N_CHUNK, W_LRU), _f32)
    d = jnp.where(sub == N_CHUNK - 1, 1.0, 0.0)
    for kk in range(N_CHUNK - 2, -1, -1):
        cz = jnp.where(sub == kk, pltpu.roll(p_end * cz + h_end, N_CHUNK - 1, axis=0), cz)
        d = jnp.where(sub == kk, pltpu.roll(p_end * d, N_CHUNK - 1, axis=0), d)
    emit(1)
    _unstage_slabs(lambda g: pl_ref[1, _slab_rows(g), :] * cz + hl_ref[1, _slab_rows(g), :], ot_h, range(G_LEN))
    emit(1)
    _unstage_slabs(lambda g: pl_ref[1, _slab_rows(g), :] * d, ot_p, range(G_LEN))
    _copy_time_major(ot_h, hb0_ref, range(N_CHUNK))
    _copy_time_major(ot_p, pb_ref, range(N_CHUNK))
    emit(len(chunks))


def _block_diag_halves(w):
    per_half = N_BLOCKS // 2
    eye = jnp.eye(per_half, dtype=w.dtype)
    w = w.reshape(w.shape[:-3] + (2, per_half, BLOCK_W, BLOCK_W))
    full = jnp.einsum('...hbde,bc->...hbdce', w, eye)
    return full.reshape(w.shape[:-3] + (per_half * BLOCK_W, per_half * BLOCK_W)).astype(_bf16)


def _front(x2, seq_len, ln_g, ln_b, w_in_bf16, b_merge, conv_w, conv_b, gate_w, gate_b, lam, apply_ln):
    t = x2.shape[0]
    n = t // TM
    per = TM // SUBLANES
    row = lambda i: (i, 0)
    full = lambda a: pl.BlockSpec(a.shape, lambda i: (0,) * a.ndim)
    wg = _block_diag_halves(gate_w)
    bg = (0.5 * gate_b).reshape(2, 2, 1, W_LRU)
    conv_w = 0.5 * conv_w
    cb2 = (0.5 * conv_b).reshape(1, W_LRU)
    widths = (W_ATTN, W_ATTN, W_ATTN, W_ATTN, W_LRU, D_MODEL, D_MODEL, W_LRU, W_LRU, W_LRU)
    out_shape = [jax.ShapeDtypeStruct((t, w), _bf16) for w in widths]
    out_specs = [pl.BlockSpec((TM, w), row) for w in widths]
    if apply_ln:
        out_shape = [jax.ShapeDtypeStruct((t, D_MODEL), _f32)] + out_shape
        out_specs = [pl.BlockSpec((TM, D_MODEL), row)] + out_specs
    stage = pltpu.VMEM((N_LG, N_CHUNK * PITCH, LANES), _f32)
    block_f32 = pltpu.VMEM((TM, W_LRU), _f32)
    return pl.pallas_call(
        functools.partial(_front_kernel, apply_ln=apply_ln, blocks_per_seq=seq_len // TM),
        grid=(n,),
        in_specs=[pl.BlockSpec((TM, D_MODEL), row),
                  pl.BlockSpec((TM, D_MODEL), lambda i: (jnp.minimum(i + 1, n - 1), 0)),
                  full(ln_g), full(ln_b), full(w_in_bf16), full(b_merge), full(conv_w), full(cb2),
                  full(wg), full(bg), full(lam)],
        out_specs=out_specs,
        out_shape=out_shape,
        scratch_shapes=[pltpu.VMEM((SUBLANES, W_LRU), _f32), pltpu.VMEM((N_CHUNK, W_LRU), _f32),
                        pltpu.VMEM((2, TM, D_MODEL), _bf16), block_f32,
                        pltpu.VMEM((TM, D_MODEL) if apply_ln else (SUBLANES, LANES), _f32),
                        block_f32, pltpu.VMEM((TM, W_LRU), _bf16),
                        pltpu.VMEM((2, TM, W_LRU), _f32), pltpu.VMEM((2, TM, W_LRU), _f32),
                        pltpu.VMEM((2, TM, W_LRU), _f32), pltpu.VMEM((2, TM, W_LRU), _f32),
                        stage, stage, stage, stage],
        compiler_params=pltpu.CompilerParams(dimension_semantics=("arbitrary",),
                                             vmem_limit_bytes=VMEM_LIMIT),
        name="front",
    )(x2, x2, ln_g, ln_b, w_in_bf16, b_merge, conv_w, cb2, wg, bg, lam)


def _attn_value_table(rpb):
    qc = np.arange(GRID_W)[:, None]
    kc = np.arange(GRID_W)[None, :]
    cs = np.clip(qc - WIN_W // 2, 0, GRID_W - WIN_W)
    ok = (kc >= cs) & (kc < cs + WIN_W)
    pad = GRID_W - WIN_W
    padded = jnp.pad(rpb.astype(_f32), ((0, 0), (0, 0), (pad, pad)))
    skew = jnp.tile(padded, (1, 1, GRID_W + 1))[:, :, :GRID_W * 2 * GRID_W]
    skew = skew.reshape(skew.shape[:2] + (GRID_W, 2 * GRID_W))
    t = skew[:, :, ::-1, :GRID_W]
    t = jnp.where(ok, t * LOG2_E, NEG)
    tiles = []
    for ct, k0 in enumerate(KC_START):
        for qr in range(SUB_ROWS):
            a0 = WIN_H // 2 - 1 - qr
            sub = t[:, a0:a0 + KEY_ROWS, ct * QC:(ct + 1) * QC, k0:k0 + KC]
            sub = sub.transpose(0, 2, 1, 3).reshape(t.shape[0], QC, KEY_ROWS * KC)
            tiles.append(jnp.pad(sub, ((0, 0), (0, 0), (0, KEYS_PAD - KEY_ROWS * KC)), constant_values=NEG))
    return jnp.stack(tiles, axis=1).reshape(t.shape[0], len(KC_START), SUB_ROWS, QC, KEYS_PAD)


def _attn_row_mask(rows):
    r_first = (0, Q_ROWS, rows - Q_ROWS)
    m = np.full((3, Q_ROWS // SUB_ROWS, SUB_ROWS, SUBLANES, KEYS_PAD), NEG, np.float32)
    m[..., KEY_ROWS * KC:] = 0.0
    for v, r0 in enumerate(r_first):
        for half in range(Q_ROWS // SUB_ROWS):
            for qr in range(SUB_ROWS):
                r = r0 + half * SUB_ROWS + qr
                rs = min(max(r - WIN_H // 2, 0), rows - WIN_H)
                for kr in range(KEY_ROWS):
                    key_r = r0 + half * SUB_ROWS - WIN_H // 2 + kr
                    if rs <= key_r < rs + WIN_H:
                        m[v, half, qr, :, kr * KC:(kr + 1) * KC] = 0.0
    return m


def _attn_kernel(q_ref, *refs):
    k_refs, v_refs = refs[:N_KV_BLKS], refs[N_KV_BLKS:2 * N_KV_BLKS]
    z_ref, val_ref, rowm_ref, o_ref, k32_ref, v32_ref = refs[2 * N_KV_BLKS:]
    m_q = SUB_ROWS * QC
    blk_tok = KV_BLK_ROWS * GRID_W
    lane = lax.broadcasted_iota(jnp.int32, (m_q, LANES), 1)
    even = lane < HEAD_DIM
    for j in range(N_KV_BLKS):
        k32_ref[j * blk_tok:(j + 1) * blk_tok, :] = k_refs[j][...].astype(_f32)
        v32_ref[j * blk_tok:(j + 1) * blk_tok, :] = v_refs[j][...].astype(_f32)
    pad = jnp.zeros((KEYS_PAD - KEY_ROWS * KC, LANES), _f32)

    def key_tile(src_ref, half, k0, cols):
        parts = [src_ref[(half * SUB_ROWS + kr) * GRID_W + k0:(half * SUB_ROWS + kr) * GRID_W + k0 + KC, cols]
                 for kr in range(KEY_ROWS)]
        return jnp.concatenate(parts + [pad], axis=0).astype(_bf16)

    def q_rows(half, ct):
        return [slice((half * SUB_ROWS + qr) * GRID_W + ct * QC, (half * SUB_ROWS + qr) * GRID_W + (ct + 1) * QC)
                for qr in range(SUB_ROWS)]

    def scores(tile):
        half, ct, pair = tile
        cols = slice(pair * LANES, (pair + 1) * LANES)
        kt = key_tile(k32_ref, half, KC_START[ct], cols)
        qp = jnp.concatenate([q_ref[r, cols] for r in q_rows(half, ct)], axis=0)
        zero = jnp.zeros_like(qp)
        q2 = jnp.concatenate([jnp.where(even, qp, zero), jnp.where(even, zero, qp)], axis=0)
        return lax.dot_general(q2, kt, (((1,), (1,)), ((), ())), preferred_element_type=_f32)

    def finish(tile, s):
        half, ct, pair = tile
        cols = slice(pair * LANES, (pair + 1) * LANES)
        blocks = []
        for parity in range(2):
            for qr in range(SUB_ROWS):
                r0 = (parity * SUB_ROWS + qr) * QC
                blk = s[r0:r0 + QC] + val_ref[2 * pair + parity, ct, qr]
                blk = blk.reshape(QC // SUBLANES, SUBLANES, -1) + rowm_ref[half, qr][None]
                blocks.append(blk.reshape(QC, -1))
        s = jnp.concatenate(blocks, axis=0)
        m = jnp.max(s, axis=-1, keepdims=True)
        e = jnp.exp2(s - m)
        l = jnp.sum(e, axis=-1, keepdims=True)
        o2 = _dot(e.astype(_bf16), key_tile(v32_ref, half, KC_START[ct], cols))
        o = jnp.where(even, o2[:m_q] / l[:m_q], o2[m_q:] / l[m_q:])
        z = jnp.concatenate([z_ref[r, cols] for r in q_rows(half, ct)], axis=0).astype(_f32)
        y = (o * z).astype(o_ref.dtype)
        for qr, r in enumerate(q_rows(half, ct)):
            o_ref[r, cols] = y[qr * QC:(qr + 1) * QC]

    tiles = [(half, ct, pair) for half in range(Q_ROWS // SUB_ROWS) for ct in range(len(KC_START))
             for pair in range(N_HEADS // 2)]
    ahead = [scores(tile) for tile in tiles[:SCORE_LOOKAHEAD]]
    for t, tile in enumerate(tiles):
        if t + SCORE_LOOKAHEAD < len(tiles):
            ahead.append(scores(tiles[t + SCORE_LOOKAHEAD]))
        finish(tile, ahead.pop(0))


def _attention(q, k, v, z_a, val, rowm):
    b, s, _ = q.shape
    rows = s // GRID_W
    n_blk = rows // Q_ROWS
    n_kv = rows // KV_BLK_ROWS
    tq = Q_ROWS * GRID_W
    tkv = KV_BLK_ROWS * GRID_W

    def kv_spec(j):
        first = -(WIN_H // 2) // KV_BLK_ROWS
        return pl.BlockSpec(
            (None, tkv, W_ATTN),
            lambda bi, i: (bi, jnp.clip(i * (Q_ROWS // KV_BLK_ROWS) + first + j, 0, n_kv - 1), 0))

    def case(i):
        return jnp.where(i == 0, 0, jnp.where(i == n_blk - 1, 2, 1))

    blk = pl.BlockSpec((None, tq, W_ATTN), lambda bi, i: (bi, i, 0))
    return pl.pallas_call(
        _attn_kernel,
        grid=(b, n_blk),
        in_specs=[blk] + [kv_spec(j) for j in range(N_KV_BLKS)] * 2 + [
            blk,
            pl.BlockSpec(val.shape, lambda bi, i: (0, 0, 0, 0, 0)),
            pl.BlockSpec((None,) + rowm.shape[1:], lambda bi, i: (case(i), 0, 0, 0, 0))],
        out_specs=blk,
        out_shape=jax.ShapeDtypeStruct((b, s, W_ATTN), _bf16),
        scratch_shapes=[pltpu.VMEM((N_KV_BLKS * tkv, W_ATTN), _f32)] * 2,
        compiler_params=pltpu.CompilerParams(dimension_semantics=("arbitrary", "arbitrary"),
                                             vmem_limit_bytes=VMEM_LIMIT),
        name="attn",
    )(q, *([k] * N_KV_BLKS), *([v] * N_KV_BLKS), z_a, val, rowm)


def _back_kernel(x_ref, ya_ref, hf_ref, hb0_ref, pb_ref, zb_ref, ga_ref, gb_ref, wba_ref, wbb_ref,
                 wo_ref, g_ref, b_ref, o_ref, carry_ref, *, alpha, blocks_per_seq):
    i = pl.program_id(0)

    @pl.when(i % blocks_per_seq == 0)
    def _():
        carry_ref[...] = jnp.zeros_like(carry_ref)

    carry = carry_ref[0:1, :]
    cin = [None] * (TB // TM)
    for sb in range(TB // TM - 1, -1, -1):
        cin[sb] = carry
        r0 = sb * TM
        carry = hb0_ref[r0:r0 + 1, :].astype(_f32) + pb_ref[r0:r0 + 1, :].astype(_f32) * carry
    carry_ref[0:1, :] = carry

    def rows_of(c):
        return slice(c * BACK_ROWS, (c + 1) * BACK_ROWS)

    def branch_dots(c):
        rows = rows_of(c)
        h_b = hb0_ref[rows, :].astype(_f32) + pb_ref[rows, :].astype(_f32) * cin[c * BACK_ROWS // TM]
        yb = ((hf_ref[rows, :].astype(_f32) + h_b) * zb_ref[rows, :].astype(_f32)).astype(_bf16)
        return _dot(ya_ref[rows, :], wba_ref[...]), _dot(yb, wbb_ref[...])

    def out_dot(c, pa, pb):
        rows = rows_of(c)
        m = ga_ref[rows, :].astype(_f32) * pa + gb_ref[rows, :].astype(_f32) * pb
        return _dot(m.astype(_bf16), wo_ref[...])

    def finish(c, out):
        rows = rows_of(c)
        o_ref[rows, :] = _layer_norm(alpha * x_ref[rows, :] + out, g_ref[...], b_ref[...])

    n_c = TB // BACK_ROWS
    branch = {0: branch_dots(0)}
    outs = {}
    for c in range(n_c):
        if c + 1 < n_c:
            branch[c + 1] = branch_dots(c + 1)
        outs[c] = out_dot(c, *branch.pop(c))
        if c >= 1:
            finish(c - 1, outs.pop(c - 1))
    finish(n_c - 1, outs.pop(n_c - 1))


def _back(x2, seq_len, y_a, h_f, h_b0, p_b, z_b, g_a, g_b, w_ba, w_bb, w_out, ln_g, ln_b, alpha):
    t = x2.shape[0]
    n = t // TB
    rev = lambda i: (n - 1 - i, 0)
    const = lambda i: (0, 0)
    rows = lambda w: pl.BlockSpec((TB, w), rev)
    full = lambda a: pl.BlockSpec(a.shape, const)
    return pl.pallas_call(
        functools.partial(_back_kernel, alpha=alpha, blocks_per_seq=seq_len // TB),
        grid=(n,),
        in_specs=[rows(D_MODEL), rows(W_ATTN), rows(W_LRU), rows(W_LRU), rows(W_LRU), rows(W_LRU),
                  rows(D_MODEL), rows(D_MODEL), full(w_ba), full(w_bb), full(w_out), full(ln_g), full(ln_b)],
        out_specs=rows(D_MODEL),
        out_shape=jax.ShapeDtypeStruct((t, D_MODEL), _f32),
        scratch_shapes=[pltpu.VMEM((SUBLANES, W_LRU), _f32)],
        compiler_params=pltpu.CompilerParams(dimension_semantics=("arbitrary",),
                                             vmem_limit_bytes=VMEM_LIMIT),
        name="back",
    )(x2, y_a, h_f, h_b0, p_b, z_b, g_a, g_b, w_ba, w_bb, w_out, ln_g, ln_b)


def kernel(x, emb_ln_g, emb_ln_b, w_in, rpb, conv_w, conv_b, lru_gate_w, lru_gate_b, lru_lambda,
           w_branch_attn, w_branch_lru, b_merge, w_out, ln_g, ln_b):
    b, s, d = x.shape
    depth = w_in.shape[0]
    assert d == D_MODEL and s % (Q_ROWS * GRID_W) == 0 and s % TM == 0 and s % TB == 0 and TB % TM == 0
    alpha = (2 * depth) ** 0.25
    t = b * s
    h = x.reshape(t, d)
    row2 = lambda a: a.reshape(1, -1)
    rowm = _attn_row_mask(s // GRID_W)
    for l in range(depth):
        outs = _front(h, s, row2(emb_ln_g), row2(emb_ln_b), w_in[l].astype(_bf16), b_merge[l], conv_w[l],
                      conv_b[l], lru_gate_w[l], lru_gate_b[l], lru_lambda[l], apply_ln=(l == 0))
        if l == 0:
            h, outs = outs[0], outs[1:]
        q, k, v, z_a, z_b, g_a, g_b, h_f, h_b0, p_b = outs
        seq = lambda a: a.reshape(b, s, a.shape[-1])
        y_a = _attention(seq(q), seq(k), seq(v), seq(z_a), _attn_value_table(rpb[l]), rowm)
        h = _back(h, s, y_a.reshape(t, W_ATTN), h_f, h_b0, p_b, z_b, g_a, g_b,
                  w_branch_attn[l].astype(_bf16), w_branch_lru[l].astype(_bf16),
                  w_out[l].astype(_bf16), row2(ln_g[l]), row2(ln_b[l]), alpha)
    return h.reshape(b, s, d)
```

```python
import functools

import numpy as np
import jax
import jax.numpy as jnp
from jax import lax
from jax.experimental import pallas as pl
from jax.experimental.pallas import tpu as pltpu

D_MODEL = 1024
GRID_W = 64
N_HEADS = 8
HEAD_DIM = 64
W_ATTN = N_HEADS * HEAD_DIM
WIN_H = 8
WIN_W = 16
W_LRU = 512
N_BLOCKS = 8
BLOCK_W = 64
CONV_W = 4
C_LRU = 8.0
LN_EPS = 1e-5
NEG = -1e30
TINY = 1e-30
LOG2_E = 1.4426950408889634

LANES = 128
SUBLANES = 8
VMEM_LIMIT = 56 * 1024 * 1024

TM = 512
TB = 1024
BACK_ROWS = 256
Q_ROWS = 16
SUB_ROWS = 4
KEY_ROWS = SUB_ROWS + WIN_H
KV_BLK_ROWS = 4
N_KV_BLKS = (Q_ROWS + WIN_H) // KV_BLK_ROWS
QC = 32
KC = 40
KC_START = (0, GRID_W - KC)
KEYS_PAD = 512
SCORE_LOOKAHEAD = 1
N_CHUNK = SUBLANES
G_LEN = TM // N_CHUNK
PITCH = G_LEN + 8
N_LG = W_LRU // LANES
PROJ_COLS = 256
GATE_SLABS = 16

_f32 = jnp.float32
_bf16 = jnp.bfloat16


def _dot(a, b):
    return jnp.dot(a, b, preferred_element_type=_f32)


def _layer_norm(x, g, b):
    mu = jnp.mean(x, axis=-1, keepdims=True)
    xc = x - mu
    var = jnp.mean(xc * xc, axis=-1, keepdims=True)
    return xc * lax.rsqrt(var + LN_EPS) * g + b


def _sigmoid(x):
    return 0.5 + 0.5 * jnp.tanh(0.5 * x)


def _silu(x):
    hx = 0.5 * x
    return hx + hx * jnp.tanh(hx)


def _stage_time_major(u, ut_ref):
    for ch in range(N_CHUNK):
        for lg in range(N_LG):
            ut_ref[lg, ch * PITCH:ch * PITCH + G_LEN, :] = u[ch * G_LEN:(ch + 1) * G_LEN,
                                                            lg * LANES:(lg + 1) * LANES]


def _load_slab(ut_ref, g):
    return jnp.concatenate([ut_ref[lg, pl.ds(g, N_CHUNK, stride=PITCH), :] for lg in range(N_LG)], axis=1)


def _slab_rows(g):
    return slice(g * N_CHUNK, (g + 1) * N_CHUNK)


def _unstage_slabs(slab_fn, ot_ref, g_range):
    for g in g_range:
        slab = slab_fn(g)
        for lg in range(N_LG):
            ot_ref[lg, pl.ds(g, N_CHUNK, stride=PITCH), :] = slab[:, lg * LANES:(lg + 1) * LANES]


def _copy_time_major(ot_ref, o_ref, ch_range):
    for ch in ch_range:
        for lg in range(N_LG):
            o_ref[ch * G_LEN:(ch + 1) * G_LEN, lg * LANES:(lg + 1) * LANES] = (
                ot_ref[lg, ch * PITCH:ch * PITCH + G_LEN, :].astype(o_ref.dtype))


def _gate_dots(ub_ref, wg_ref):
    half = W_LRU // 2
    pre = []
    for gi in range(2):
        cols = [_dot(ub_ref[:, hf * half:(hf + 1) * half], wg_ref[gi, hf]) for hf in range(2)]
        pre.append(jnp.concatenate(cols, axis=1))
    return pre


def _gates(pre, hu_ref, bg_ref, coef, first, g_first, a_ref, bx_ref):
    t_r = jnp.tanh(pre[0] + bg_ref[0])
    t_i = jnp.tanh(pre[1] + bg_ref[1])
    a = jnp.exp2(coef + coef * t_r)
    one_m_a2 = (1.0 - a) * (1.0 + a)
    mult = one_m_a2 * lax.rsqrt(jnp.maximum(one_m_a2, TINY))
    hu = hu_ref[...]
    iu = hu + hu * t_i
    bx = mult * iu
    a_ref[...] = a
    bx_ref[...] = bx
    fix = _slab_rows(g_first)
    bx_ref[fix, :] = jnp.where(first, iu[fix], bx[fix])


def _chunk_scan(a_ref, bx_ref, hl_ref, pl_ref, g_range):
    h = jnp.zeros((N_CHUNK, W_LRU), _f32)
    p = jnp.ones((N_CHUNK, W_LRU), _f32)
    for g in g_range:
        rows = _slab_rows(g)
        ag = a_ref[rows, :]
        h = ag * h + bx_ref[rows, :]
        p = ag * p
        hl_ref[rows, :] = h
        pl_ref[rows, :] = p
    return h, p


def _front_kernel(x_ref, xn_ref, g_ref, b_ref, w_ref, bm_ref, cw_ref, cb_ref, wg_ref, bg_ref, lam_ref,
                  *refs, apply_ln, blocks_per_seq):
    n_out = 11 if apply_ln else 10
    out_refs = refs[:n_out]
    (uprev_ref, carry_ref, xb_ref, hu_ref, ub_ref, a_ref, bx_ref, hl_ref, pl_ref,
     ut_ref, ot_s, ot_p) = refs[n_out:]
    if apply_ln:
        h_ref, out_refs = out_refs[0], out_refs[1:]
    q_ref, k_ref, v_ref, za_ref, zb_ref, ga_ref, gb_ref, hs_ref, pb_ref, hb0f_ref = out_refs
    j = pl.program_id(0)
    is_first_blk = j % blocks_per_seq == 0
    is_last_blk = j % blocks_per_seq == blocks_per_seq - 1
    u_off = 4 * W_ATTN
    g_off = u_off + 2 * W_LRU
    sub = lax.broadcasted_iota(jnp.int32, (N_CHUNK, W_LRU), 0)

    def edge(row):
        return jnp.broadcast_to(row, (N_CHUNK, W_LRU))

    def proj_chunks():
        def chunk(o_ref, off, post, c0):
            def run():
                y = _dot(xb_ref[...], w_ref[:, off + c0:off + c0 + PROJ_COLS])
                o_ref[:, c0:c0 + PROJ_COLS] = post(y, slice(c0, c0 + PROJ_COLS)).astype(o_ref.dtype)
            return run
        plan = ((q_ref, 0, lambda y, cols: y * (HEAD_DIM ** -0.5 * LOG2_E)),
                (k_ref, W_ATTN, lambda y, cols: y),
                (v_ref, 2 * W_ATTN, lambda y, cols: y),
                (za_ref, 3 * W_ATTN, lambda y, cols: _silu(y)),
                (zb_ref, u_off + W_LRU, lambda y, cols: _silu(y)),
                (ga_ref, g_off, lambda y, cols: _sigmoid(y + bm_ref[0:1, cols])),
                (gb_ref, g_off + D_MODEL, lambda y, cols: _sigmoid(y + bm_ref[1:2, cols])))
        return [chunk(o_ref, off, post, c0) for o_ref, off, post in plan
                for c0 in range(0, o_ref.shape[-1], PROJ_COLS)]

    @pl.when(is_first_blk)
    def _():
        uprev_ref[...] = jnp.zeros_like(uprev_ref)
        carry_ref[...] = jnp.zeros_like(carry_ref)

    x = x_ref[...]
    xn = xn_ref[...]
    if apply_ln:
        x = _layer_norm(x, g_ref[...], b_ref[...])
        xn = _layer_norm(xn, g_ref[...], b_ref[...])
        h_ref[...] = x
    xb_ref[...] = x.astype(_bf16)
    u = _dot(xb_ref[...], w_ref[:, u_off:u_off + W_LRU])
    u_next = _dot(xn.astype(_bf16), w_ref[:, u_off:u_off + W_LRU])
    _stage_time_major(u, ut_ref)

    slabs = {}
    slabs[-2] = jnp.where(sub == 0, edge(uprev_ref[SUBLANES - 2:SUBLANES - 1, :]),
                          pltpu.roll(_load_slab(ut_ref, G_LEN - 2), 1, axis=0))
    slabs[-1] = jnp.where(sub == 0, edge(uprev_ref[SUBLANES - 1:SUBLANES, :]),
                          pltpu.roll(_load_slab(ut_ref, G_LEN - 1), 1, axis=0))
    nxt = jnp.where(is_last_blk, 0.0, u_next[0:1, :])
    slabs[G_LEN] = jnp.where(sub == N_CHUNK - 1, edge(nxt), pltpu.roll(_load_slab(ut_ref, 0), N_CHUNK - 1, axis=0))
    uprev_ref[...] = u[TM - SUBLANES:TM, :]

    def slab(g):
        if g not in slabs:
            slabs[g] = _load_slab(ut_ref, g)
        return slabs[g]

    cw = [cw_ref[i:i + 1, :] for i in range(CONV_W)]
    cb = cb_ref[...]
    neg_lam = -lam_ref[...]
    softplus = jnp.maximum(neg_lam, 0.0) + jnp.log(1.0 + jnp.exp(-jnp.abs(neg_lam)))
    coef = (-0.5 * C_LRU * LOG2_E) * softplus
    first = jnp.logical_and(is_first_blk, sub == 0)
    last = jnp.logical_and(is_last_blk, sub == N_CHUNK - 1)

    n_grp = G_LEN // GATE_SLABS
    grp_rows = lambda r: slice(r * GATE_SLABS * N_CHUNK, (r + 1) * GATE_SLABS * N_CHUNK)

    def conv_group(r):
        for g in range(r * GATE_SLABS, (r + 1) * GATE_SLABS):
            hu_ref[_slab_rows(g), :] = (slab(g - 2) * cw[0] + slab(g - 1) * cw[1] + slab(g) * cw[2]
                                        + slab(g + 1) * cw[3] + cb)
        ub_ref[grp_rows(r), :] = hu_ref[grp_rows(r), :].astype(_bf16)

    def gate_group(r):
        rows = grp_rows(r)
        half = W_LRU // 2
        hu = hu_ref[rows, :]
        for di in range(2):
            pre = [jnp.concatenate([_dot(ub_ref[rows, hf * half:(hf + 1) * half], wg_ref[di, gi, hf])
                                    for hf in range(2)], axis=1) for gi in range(2)]
            t_r = jnp.tanh(pre[0] + bg_ref[di, 0])
            t_i = jnp.tanh(pre[1] + bg_ref[di, 1])
            a = jnp.exp2(coef[di:di + 1] + coef[di:di + 1] * t_r)
            one_m_a2 = (1.0 - a) * (1.0 + a)
            mult = one_m_a2 * lax.rsqrt(jnp.maximum(one_m_a2, TINY))
            iu = hu + hu * t_i
            bx = mult * iu
            a_ref[di, rows, :] = a
            bx_ref[di, rows, :] = bx
            g_fix = (G_LEN - 1) if di else 0
            if r * GATE_SLABS <= g_fix < (r + 1) * GATE_SLABS:
                rel = slice((g_fix - r * GATE_SLABS) * N_CHUNK, (g_fix - r * GATE_SLABS + 1) * N_CHUNK)
                bx_ref[di, _slab_rows(g_fix), :] = jnp.where(last if di else first, iu[rel], bx[rel])

    def scan_slabs(di, state, g_range):
        h, p = state
        for g in g_range:
            rows = _slab_rows(g)
            ag = a_ref[di, rows, :]
            h = ag * h + bx_ref[di, rows, :]
            p = ag * p
            hl_ref[di, rows, :] = h
            pl_ref[di, rows, :] = p
        return h, p

    chunks = proj_chunks()

    def emit(n):
        for _ in range(min(n, len(chunks))):
            chunks.pop(0)()
    zero_state = (jnp.zeros((N_CHUNK, W_LRU), _f32), jnp.ones((N_CHUNK, W_LRU), _f32))

    emit(2)
    conv_group(0)
    state = zero_state
    for r in range(n_grp):
        emit(2)
        if r + 1 < n_grp:
            conv_group(r + 1)
        gate_group(r)
        if r >= 1:
            state = scan_slabs(0, state, range((r - 1) * GATE_SLABS, r * GATE_SLABS))
    emit(2)
    h_end, p_end = scan_slabs(0, state, range((n_grp - 1) * GATE_SLABS, G_LEN))
    c = jnp.where(sub == 0, edge(carry_ref[N_CHUNK - 1:N_CHUNK, :]), 0.0)
    for kk in range(1, N_CHUNK):
        c = jnp.where(sub == kk, pltpu.roll(p_end * c + h_end, 1, axis=0), c)
    carry_ref[...] = p_end * c + h_end

    emit(2)
    h_end, p_end = scan_slabs(1, zero_state, range(G_LEN - 1, -1, -1))
    cz = jnp.zeros((N_CHUNK, W_LRU), _f32)
    d = jnp.where(sub == N_CHUNK - 1, 1.0, 0.0)
    for kk in range(N_CHUNK - 2, -1, -1):
        cz = jnp.where(sub == kk, pltpu.roll(p_end * cz + h_end, N_CHUNK - 1, axis=0), cz)
        d = jnp.where(sub == kk, pltpu.roll(p_end * d, N_CHUNK - 1, axis=0), d)

    def h_bwd(g):
        return pl_ref[1, _slab_rows(g), :] * cz + hl_ref[1, _slab_rows(g), :]

    hb0f_ref[...] = h_bwd(0)
    emit(2)
    _unstage_slabs(lambda g: pl_ref[0, _slab_rows(g), :] * c + hl_ref[0, _slab_rows(g), :] + h_bwd(g),
                   ot_s, range(G_LEN))
    emit(2)
    _unstage_slabs(lambda g: pl_ref[1, _slab_rows(g), :] * d, ot_p, range(G_LEN))
    _copy_time_major(ot_s, hs_ref, range(N_CHUNK))
    _copy_time_major(ot_p, pb_ref, range(N_CHUNK))
    emit(len(chunks))


def _block_diag_halves(w):
    per_half = N_BLOCKS // 2
    eye = jnp.eye(per_half, dtype=w.dtype)
    w = w.reshape(w.shape[:-3] + (2, per_half, BLOCK_W, BLOCK_W))
    full = jnp.einsum('...hbde,bc->...hbdce', w, eye)
    return full.reshape(w.shape[:-3] + (per_half * BLOCK_W, per_half * BLOCK_W)).astype(_bf16)


def _front(x2, seq_len, ln_g, ln_b, w_in_bf16, b_merge, conv_w, conv_b, gate_w, gate_b, lam, apply_ln):
    t = x2.shape[0]
    n = t // TM
    per = TM // SUBLANES
    row = lambda i: (i, 0)
    full = lambda a: pl.BlockSpec(a.shape, lambda i: (0,) * a.ndim)
    wg = _block_diag_halves(gate_w)
    bg = (0.5 * gate_b).reshape(2, 2, 1, W_LRU)
    conv_w = 0.5 * conv_w
    cb2 = (0.5 * conv_b).reshape(1, W_LRU)
    widths = (W_ATTN, W_ATTN, W_ATTN, W_ATTN, W_LRU, D_MODEL, D_MODEL, W_LRU, W_LRU)
    out_shape = [jax.ShapeDtypeStruct((t, w), _bf16) for w in widths]
    out_specs = [pl.BlockSpec((TM, w), row) for w in widths]
    out_shape.append(jax.ShapeDtypeStruct((n * N_CHUNK, W_LRU), _f32))
    out_specs.append(pl.BlockSpec((N_CHUNK, W_LRU), row))
    if apply_ln:
        out_shape = [jax.ShapeDtypeStruct((t, D_MODEL), _f32)] + out_shape
        out_specs = [pl.BlockSpec((TM, D_MODEL), row)] + out_specs
    stage = pltpu.VMEM((N_LG, N_CHUNK * PITCH, LANES), _f32)
    block_f32 = pltpu.VMEM((TM, W_LRU), _f32)
    return pl.pallas_call(
        functools.partial(_front_kernel, apply_ln=apply_ln, blocks_per_seq=seq_len // TM),
        grid=(n,),
        in_specs=[pl.BlockSpec((TM, D_MODEL), row),
                  pl.BlockSpec((SUBLANES, D_MODEL), lambda i: (jnp.minimum((i + 1) * per, n * per - 1), 0)),
                  full(ln_g), full(ln_b), full(w_in_bf16), full(b_merge), full(conv_w), full(cb2),
                  full(wg), full(bg), full(lam)],
        out_specs=out_specs,
        out_shape=out_shape,
        scratch_shapes=[pltpu.VMEM((SUBLANES, W_LRU), _f32), pltpu.VMEM((N_CHUNK, W_LRU), _f32),
                        pltpu.VMEM((TM, D_MODEL), _bf16), block_f32, pltpu.VMEM((TM, W_LRU), _bf16),
                        pltpu.VMEM((2, TM, W_LRU), _f32), pltpu.VMEM((2, TM, W_LRU), _f32),
                        pltpu.VMEM((2, TM, W_LRU), _f32), pltpu.VMEM((2, TM, W_LRU), _f32),
                        stage, stage, stage],
        compiler_params=pltpu.CompilerParams(dimension_semantics=("arbitrary",),
                                             vmem_limit_bytes=VMEM_LIMIT),
        name="front",
    )(x2, x2, ln_g, ln_b, w_in_bf16, b_merge, conv_w, cb2, wg, bg, lam)


def _attn_value_table(rpb):
    qc = np.arange(GRID_W)[:, None]
    kc = np.arange(GRID_W)[None, :]
    cs = np.clip(qc - WIN_W // 2, 0, GRID_W - WIN_W)
    ok = (kc >= cs) & (kc < cs + WIN_W)
    pad = GRID_W - WIN_W
    padded = jnp.pad(rpb.astype(_f32), ((0, 0), (0, 0), (pad, pad)))
    skew = jnp.tile(padded, (1, 1, GRID_W + 1))[:, :, :GRID_W * 2 * GRID_W]
    skew = skew.reshape(skew.shape[:2] + (GRID_W, 2 * GRID_W))
    t = skew[:, :, ::-1, :GRID_W]
    t = jnp.where(ok, t * LOG2_E, NEG)
    tiles = []
    for ct, k0 in enumerate(KC_START):
        for qr in range(SUB_ROWS):
            a0 = WIN_H // 2 - 1 - qr
            sub = t[:, a0:a0 + KEY_ROWS, ct * QC:(ct + 1) * QC, k0:k0 + KC]
            sub = sub.transpose(0, 2, 1, 3).reshape(t.shape[0], QC, KEY_ROWS * KC)
            tiles.append(jnp.pad(sub, ((0, 0), (0, 0), (0, KEYS_PAD - KEY_ROWS * KC)), constant_values=NEG))
    return jnp.stack(tiles, axis=1).reshape(t.shape[0], len(KC_START), SUB_ROWS, QC, KEYS_PAD)


def _attn_row_mask(rows):
    r_first = (0, Q_ROWS, rows - Q_ROWS)
    m = np.full((3, Q_ROWS // SUB_ROWS, SUB_ROWS, SUBLANES, KEYS_PAD), NEG, np.float32)
    m[..., KEY_ROWS * KC:] = 0.0
    for v, r0 in enumerate(r_first):
        for half in range(Q_ROWS // SUB_ROWS):
            for qr in range(SUB_ROWS):
                r = r0 + half * SUB_ROWS + qr
                rs = min(max(r - WIN_H // 2, 0), rows - WIN_H)
                for kr in range(KEY_ROWS):
                    key_r = r0 + half * SUB_ROWS - WIN_H // 2 + kr
                    if rs <= key_r < rs + WIN_H:
                        m[v, half, qr, :, kr * KC:(kr + 1) * KC] = 0.0
    return m


def _attn_kernel(q_ref, *refs):
    k_refs, v_refs = refs[:N_KV_BLKS], refs[N_KV_BLKS:2 * N_KV_BLKS]
    z_ref, val_ref, rowm_ref, o_ref, k32_ref, v32_ref = refs[2 * N_KV_BLKS:]
    m_q = SUB_ROWS * QC
    blk_tok = KV_BLK_ROWS * GRID_W
    lane = lax.broadcasted_iota(jnp.int32, (m_q, LANES), 1)
    even = lane < HEAD_DIM
    for j in range(N_KV_BLKS):
        k32_ref[j * blk_tok:(j + 1) * blk_tok, :] = k_refs[j][...].astype(_f32)
        v32_ref[j * blk_tok:(j + 1) * blk_tok, :] = v_refs[j][...].astype(_f32)
    pad = jnp.zeros((KEYS_PAD - KEY_ROWS * KC, LANES), _f32)

    def key_tile(src_ref, half, k0, cols):
        parts = [src_ref[(half * SUB_ROWS + kr) * GRID_W + k0:(half * SUB_ROWS + kr) * GRID_W + k0 + KC, cols]
                 for kr in range(KEY_ROWS)]
        return jnp.concatenate(parts + [pad], axis=0).astype(_bf16)

    def q_rows(half, ct):
        return [slice((half * SUB_ROWS + qr) * GRID_W + ct * QC, (half * SUB_ROWS + qr) * GRID_W + (ct + 1) * QC)
                for qr in range(SUB_ROWS)]

    def scores(tile):
        half, ct, pair = tile
        cols = slice(pair * LANES, (pair + 1) * LANES)
        kt = key_tile(k32_ref, half, KC_START[ct], cols)
        qp = jnp.concatenate([q_ref[r, cols] for r in q_rows(half, ct)], axis=0)
        zero = jnp.zeros_like(qp)
        q2 = jnp.concatenate([jnp.where(even, qp, zero), jnp.where(even, zero, qp)], axis=0)
        return lax.dot_general(q2, kt, (((1,), (1,)), ((), ())), preferred_element_type=_f32)

    def finish(tile, s):
        half, ct, pair = tile
        cols = slice(pair * LANES, (pair + 1) * LANES)
        blocks = []
        for parity in range(2):
            for qr in range(SUB_ROWS):
                r0 = (parity * SUB_ROWS + qr) * QC
                blk = s[r0:r0 + QC] + val_ref[2 * pair + parity, ct, qr]
                blk = blk.reshape(QC // SUBLANES, SUBLANES, -1) + rowm_ref[half, qr][None]
                blocks.append(blk.reshape(QC, -1))
        s = jnp.concatenate(blocks, axis=0)
        m = jnp.max(s, axis=-1, keepdims=True)
        e = jnp.exp2(s - m)
        l = jnp.sum(e, axis=-1, keepdims=True)
        o2 = _dot(e.astype(_bf16), key_tile(v32_ref, half, KC_START[ct], cols))
        o = jnp.where(even, o2[:m_q] / l[:m_q], o2[m_q:] / l[m_q:])
        z = jnp.concatenate([z_ref[r, cols] for r in q_rows(half, ct)], axis=0).astype(_f32)
        y = (o * z).astype(o_ref.dtype)
        for qr, r in enumerate(q_rows(half, ct)):
            o_ref[r, cols] = y[qr * QC:(qr + 1) * QC]

    tiles = [(half, ct, pair) for half in range(Q_ROWS // SUB_ROWS) for ct in range(len(KC_START))
             for pair in range(N_HEADS // 2)]
    ahead = [scores(tile) for tile in tiles[:SCORE_LOOKAHEAD]]
    for t, tile in enumerate(tiles):
        if t + SCORE_LOOKAHEAD < len(tiles):
            ahead.append(scores(tiles[t + SCORE_LOOKAHEAD]))
        finish(tile, ahead.pop(0))


def _attention(q, k, v, z_a, val, rowm):
    b, s, _ = q.shape
    rows = s // GRID_W
    n_blk = rows // Q_ROWS
    n_kv = rows // KV_BLK_ROWS
    tq = Q_ROWS * GRID_W
    tkv = KV_BLK_ROWS * GRID_W

    def kv_spec(j):
        first = -(WIN_H // 2) // KV_BLK_ROWS
        return pl.BlockSpec(
            (None, tkv, W_ATTN),
            lambda bi, i: (bi, jnp.clip(i * (Q_ROWS // KV_BLK_ROWS) + first + j, 0, n_kv - 1), 0))

    def case(i):
        return jnp.where(i == 0, 0, jnp.where(i == n_blk - 1, 2, 1))

    blk = pl.BlockSpec((None, tq, W_ATTN), lambda bi, i: (bi, i, 0))
    return pl.pallas_call(
        _attn_kernel,
        grid=(b, n_blk),
        in_specs=[blk] + [kv_spec(j) for j in range(N_KV_BLKS)] * 2 + [
            blk,
            pl.BlockSpec(val.shape, lambda bi, i: (0, 0, 0, 0, 0)),
            pl.BlockSpec((None,) + rowm.shape[1:], lambda bi, i: (case(i), 0, 0, 0, 0))],
        out_specs=blk,
        out_shape=jax.ShapeDtypeStruct((b, s, W_ATTN), _bf16),
        scratch_shapes=[pltpu.VMEM((N_KV_BLKS * tkv, W_ATTN), _f32)] * 2,
        compiler_params=pltpu.CompilerParams(dimension_semantics=("arbitrary", "arbitrary"),
                                             vmem_limit_bytes=VMEM_LIMIT),
        name="attn",
    )(q, *([k] * N_KV_BLKS), *([v] * N_KV_BLKS), z_a, val, rowm)


def _back_kernel(x_ref, ya_ref, hs_ref, pb_ref, hb0f_ref, zb_ref, ga_ref, gb_ref, wba_ref, wbb_ref,
                 wo_ref, g_ref, b_ref, o_ref, carry_ref, *, alpha, blocks_per_seq):
    i = pl.program_id(0)

    @pl.when(i % blocks_per_seq == 0)
    def _():
        carry_ref[...] = jnp.zeros_like(carry_ref)

    carry = carry_ref[0:1, :]
    cin = [None] * (TB // TM)
    for sb in range(TB // TM - 1, -1, -1):
        cin[sb] = carry
        r0 = sb * TM
        carry = hb0f_ref[sb * N_CHUNK:sb * N_CHUNK + 1, :] + pb_ref[r0:r0 + 1, :].astype(_f32) * carry
    carry_ref[0:1, :] = carry

    def rows_of(c):
        return slice(c * BACK_ROWS, (c + 1) * BACK_ROWS)

    def branch_dots(c):
        rows = rows_of(c)
        h = hs_ref[rows, :].astype(_f32) + pb_ref[rows, :].astype(_f32) * cin[c * BACK_ROWS // TM]
        yb = (h * zb_ref[rows, :].astype(_f32)).astype(_bf16)
        return _dot(ya_ref[rows, :], wba_ref[...]), _dot(yb, wbb_ref[...])

    def out_dot(c, pa, pb):
        rows = rows_of(c)
        m = ga_ref[rows, :].astype(_f32) * pa + gb_ref[rows, :].astype(_f32) * pb
        return _dot(m.astype(_bf16), wo_ref[...])

    def finish(c, out):
        rows = rows_of(c)
        o_ref[rows, :] = _layer_norm(alpha * x_ref[rows, :] + out, g_ref[...], b_ref[...])

    n_c = TB // BACK_ROWS
    branch = {0: branch_dots(0)}
    outs = {}
    for c in range(n_c):
        if c + 1 < n_c:
            branch[c + 1] = branch_dots(c + 1)
        outs[c] = out_dot(c, *branch.pop(c))
        if c >= 1:
            finish(c - 1, outs.pop(c - 1))
    finish(n_c - 1, outs.pop(n_c - 1))


def _back(x2, seq_len, y_a, h_s, p_b, h_b0f, z_b, g_a, g_b, w_ba, w_bb, w_out, ln_g, ln_b, alpha):
    t = x2.shape[0]
    n = t // TB
    rev = lambda i: (n - 1 - i, 0)
    const = lambda i: (0, 0)
    rows = lambda w: pl.BlockSpec((TB, w), rev)
    full = lambda a: pl.BlockSpec(a.shape, const)
    return pl.pallas_call(
        functools.partial(_back_kernel, alpha=alpha, blocks_per_seq=seq_len // TB),
        grid=(n,),
        in_specs=[rows(D_MODEL), rows(W_ATTN), rows(W_LRU), rows(W_LRU),
                  pl.BlockSpec((TB // TM * N_CHUNK, W_LRU), rev), rows(W_LRU),
                  rows(D_MODEL), rows(D_MODEL), full(w_ba), full(w_bb), full(w_out), full(ln_g), full(ln_b)],
        out_specs=rows(D_MODEL),
        out_shape=jax.ShapeDtypeStruct((t, D_MODEL), _f32),
        scratch_shapes=[pltpu.VMEM((SUBLANES, W_LRU), _f32)],
        compiler_params=pltpu.CompilerParams(dimension_semantics=("arbitrary",),
                                             vmem_limit_bytes=VMEM_LIMIT),
        name="back",
    )(x2, y_a, h_s, p_b, h_b0f, z_b, g_a, g_b, w_ba, w_bb, w_out, ln_g, ln_b)


def kernel(x, emb_ln_g, emb_ln_b, w_in, rpb, conv_w, conv_b, lru_gate_w, lru_gate_b, lru_lambda,
           w_branch_attn, w_branch_lru, b_merge, w_out, ln_g, ln_b):
    b, s, d = x.shape
    depth = w_in.shape[0]
    assert d == D_MODEL and s % (Q_ROWS * GRID_W) == 0 and s % TM == 0 and s % TB == 0 and TB % TM == 0
    alpha = (2 * depth) ** 0.25
    t = b * s
    h = x.reshape(t, d)
    row2 = lambda a: a.reshape(1, -1)
    rowm = _attn_row_mask(s // GRID_W)
    for l in range(depth):
        outs = _front(h, s, row2(emb_ln_g), row2(emb_ln_b), w_in[l].astype(_bf16), b_merge[l], conv_w[l],
                      conv_b[l], lru_gate_w[l], lru_gate_b[l], lru_lambda[l], apply_ln=(l == 0))
        if l == 0:
            h, outs = outs[0], outs[1:]
        q, k, v, z_a, z_b, g_a, g_b, h_s, p_b, h_b0f = outs
        seq = lambda a: a.reshape(b, s, a.shape[-1])
        y_a = _attention(seq(q), seq(k), seq(v), seq(z_a), _attn_value_table(rpb[l]), rowm)
        h = _back(h, s, y_a.reshape(t, W_ATTN), h_s, p_b, h_b0f, z_b, g_a, g_b,
                  w_branch_attn[l].astype(_bf16), w_branch_lru[l].astype(_bf16),
                  w_out[l].astype(_bf16), row2(ln_g[l]), row2(ln_b[l]), alpha)
    return h.reshape(b, s, d)
```

```python
import functools

import numpy as np
import jax
import jax.numpy as jnp
from jax import lax
from jax.experimental import pallas as pl
from jax.experimental.pallas import tpu as pltpu

D_MODEL = 1024
GRID_W = 64
N_HEADS = 8
HEAD_DIM = 64
W_ATTN = N_HEADS * HEAD_DIM
WIN_H = 8
WIN_W = 16
W_LRU = 512
N_BLOCKS = 8
BLOCK_W = 64
CONV_W = 4
C_LRU = 8.0
LN_EPS = 1e-5
NEG = -1e30
TINY = 1e-30
LOG2_E = 1.4426950408889634

LANES = 128
SUBLANES = 8
VMEM_LIMIT = 56 * 1024 * 1024

TM = 512
TB = 1024
BACK_ROWS = 256
Q_ROWS = 16
SUB_ROWS = 4
KEY_ROWS = SUB_ROWS + WIN_H
KV_BLK_ROWS = 4
N_KV_BLKS = (Q_ROWS + WIN_H) // KV_BLK_ROWS
QC = 32
KC = 40
KC_START = (0, GRID_W - KC)
KEYS_PAD = 512
SCORE_LOOKAHEAD = 1
N_CHUNK = SUBLANES
G_LEN = TM // N_CHUNK
PITCH = G_LEN + 8
N_LG = W_LRU // LANES
PROJ_COLS = 256
GATE_SLABS = 16

_f32 = jnp.float32
_bf16 = jnp.bfloat16


def _dot(a, b):
    return jnp.dot(a, b, preferred_element_type=_f32)


def _layer_norm(x, g, b):
    mu = jnp.mean(x, axis=-1, keepdims=True)
    xc = x - mu
    var = jnp.mean(xc * xc, axis=-1, keepdims=True)
    return xc * lax.rsqrt(var + LN_EPS) * g + b


def _sigmoid(x):
    return 0.5 + 0.5 * jnp.tanh(0.5 * x)


def _silu(x):
    hx = 0.5 * x
    return hx + hx * jnp.tanh(hx)


def _stage_time_major(u, ut_ref):
    for ch in range(N_CHUNK):
        for lg in range(N_LG):
            ut_ref[lg, ch * PITCH:ch * PITCH + G_LEN, :] = u[ch * G_LEN:(ch + 1) * G_LEN,
                                                            lg * LANES:(lg + 1) * LANES]


def _load_slab(ut_ref, g):
    return jnp.concatenate([ut_ref[lg, pl.ds(g, N_CHUNK, stride=PITCH), :] for lg in range(N_LG)], axis=1)


def _slab_rows(g):
    return slice(g * N_CHUNK, (g + 1) * N_CHUNK)


def _unstage_slabs(slab_fn, ot_ref, g_range):
    for g in g_range:
        slab = slab_fn(g)
        for lg in range(N_LG):
            ot_ref[lg, pl.ds(g, N_CHUNK, stride=PITCH), :] = slab[:, lg * LANES:(lg + 1) * LANES]


def _copy_time_major(ot_ref, o_ref, ch_range):
    for ch in ch_range:
        for lg in range(N_LG):
            o_ref[ch * G_LEN:(ch + 1) * G_LEN, lg * LANES:(lg + 1) * LANES] = (
                ot_ref[lg, ch * PITCH:ch * PITCH + G_LEN, :].astype(o_ref.dtype))


def _gate_dots(ub_ref, wg_ref):
    half = W_LRU // 2
    pre = []
    for gi in range(2):
        cols = [_dot(ub_ref[:, hf * half:(hf + 1) * half], wg_ref[gi, hf]) for hf in range(2)]
        pre.append(jnp.concatenate(cols, axis=1))
    return pre


def _gates(pre, hu_ref, bg_ref, coef, first, g_first, a_ref, bx_ref):
    t_r = jnp.tanh(pre[0] + bg_ref[0])
    t_i = jnp.tanh(pre[1] + bg_ref[1])
    a = jnp.exp2(coef + coef * t_r)
    one_m_a2 = (1.0 - a) * (1.0 + a)
    mult = one_m_a2 * lax.rsqrt(jnp.maximum(one_m_a2, TINY))
    hu = hu_ref[...]
    iu = hu + hu * t_i
    bx = mult * iu
    a_ref[...] = a
    bx_ref[...] = bx
    fix = _slab_rows(g_first)
    bx_ref[fix, :] = jnp.where(first, iu[fix], bx[fix])


def _chunk_scan(a_ref, bx_ref, hl_ref, pl_ref, g_range):
    h = jnp.zeros((N_CHUNK, W_LRU), _f32)
    p = jnp.ones((N_CHUNK, W_LRU), _f32)
    for g in g_range:
        rows = _slab_rows(g)
        ag = a_ref[rows, :]
        h = ag * h + bx_ref[rows, :]
        p = ag * p
        hl_ref[rows, :] = h
        pl_ref[rows, :] = p
    return h, p


def _front_kernel(x_ref, xn_ref, g_ref, b_ref, w_ref, cw_ref, cb_ref, wg_ref, bg_ref, lam_ref,
                  *refs, apply_ln, blocks_per_seq):
    n_out = 9 if apply_ln else 8
    out_refs = refs[:n_out]
    (uprev_ref, carry_ref, xb_ref, hu_ref, ub_ref, a_ref, bx_ref, hl_ref, pl_ref,
     ut_ref, ot_s, ot_p) = refs[n_out:]
    if apply_ln:
        h_ref, out_refs = out_refs[0], out_refs[1:]
    q_ref, k_ref, v_ref, za_ref, zb_ref, hs_ref, pb_ref, hb0f_ref = out_refs
    j = pl.program_id(0)
    is_first_blk = j % blocks_per_seq == 0
    is_last_blk = j % blocks_per_seq == blocks_per_seq - 1
    u_off = 4 * W_ATTN
    sub = lax.broadcasted_iota(jnp.int32, (N_CHUNK, W_LRU), 0)

    def edge(row):
        return jnp.broadcast_to(row, (N_CHUNK, W_LRU))

    def proj_chunks():
        def chunk(o_ref, off, post, c0):
            def run():
                y = _dot(xb_ref[...], w_ref[:, off + c0:off + c0 + PROJ_COLS])
                o_ref[:, c0:c0 + PROJ_COLS] = post(y, slice(c0, c0 + PROJ_COLS)).astype(o_ref.dtype)
            return run
        plan = ((q_ref, 0, lambda y, cols: y * (HEAD_DIM ** -0.5 * LOG2_E)),
                (k_ref, W_ATTN, lambda y, cols: y),
                (v_ref, 2 * W_ATTN, lambda y, cols: y),
                (za_ref, 3 * W_ATTN, lambda y, cols: _silu(y)),
                (zb_ref, u_off + W_LRU, lambda y, cols: _silu(y)))
        return [chunk(o_ref, off, post, c0) for o_ref, off, post in plan
                for c0 in range(0, o_ref.shape[-1], PROJ_COLS)]

    @pl.when(is_first_blk)
    def _():
        uprev_ref[...] = jnp.zeros_like(uprev_ref)
        carry_ref[...] = jnp.zeros_like(carry_ref)

    x = x_ref[...]
    xn = xn_ref[...]
    if apply_ln:
        x = _layer_norm(x, g_ref[...], b_ref[...])
        xn = _layer_norm(xn, g_ref[...], b_ref[...])
        h_ref[...] = x
    xb_ref[...] = x.astype(_bf16)
    u = _dot(xb_ref[...], w_ref[:, u_off:u_off + W_LRU])
    u_next = _dot(xn.astype(_bf16), w_ref[:, u_off:u_off + W_LRU])
    _stage_time_major(u, ut_ref)

    slabs = {}
    slabs[-2] = jnp.where(sub == 0, edge(uprev_ref[SUBLANES - 2:SUBLANES - 1, :]),
                          pltpu.roll(_load_slab(ut_ref, G_LEN - 2), 1, axis=0))
    slabs[-1] = jnp.where(sub == 0, edge(uprev_ref[SUBLANES - 1:SUBLANES, :]),
                          pltpu.roll(_load_slab(ut_ref, G_LEN - 1), 1, axis=0))
    nxt = jnp.where(is_last_blk, 0.0, u_next[0:1, :])
    slabs[G_LEN] = jnp.where(sub == N_CHUNK - 1, edge(nxt), pltpu.roll(_load_slab(ut_ref, 0), N_CHUNK - 1, axis=0))
    uprev_ref[...] = u[TM - SUBLANES:TM, :]

    def slab(g):
        if g not in slabs:
            slabs[g] = _load_slab(ut_ref, g)
        return slabs[g]

    cw = [cw_ref[i:i + 1, :] for i in range(CONV_W)]
    cb = cb_ref[...]
    neg_lam = -lam_ref[...]
    softplus = jnp.maximum(neg_lam, 0.0) + jnp.log(1.0 + jnp.exp(-jnp.abs(neg_lam)))
    coef = (-0.5 * C_LRU * LOG2_E) * softplus
    first = jnp.logical_and(is_first_blk, sub == 0)
    last = jnp.logical_and(is_last_blk, sub == N_CHUNK - 1)

    n_grp = G_LEN // GATE_SLABS
    grp_rows = lambda r: slice(r * GATE_SLABS * N_CHUNK, (r + 1) * GATE_SLABS * N_CHUNK)

    def conv_group(r):
        for g in range(r * GATE_SLABS, (r + 1) * GATE_SLABS):
            hu_ref[_slab_rows(g), :] = (slab(g - 2) * cw[0] + slab(g - 1) * cw[1] + slab(g) * cw[2]
                                        + slab(g + 1) * cw[3] + cb)
        ub_ref[grp_rows(r), :] = hu_ref[grp_rows(r), :].astype(_bf16)

    def gate_group(r):
        rows = grp_rows(r)
        half = W_LRU // 2
        hu = hu_ref[rows, :]
        for di in range(2):
            pre = [jnp.concatenate([_dot(ub_ref[rows, hf * half:(hf + 1) * half], wg_ref[di, gi, hf])
                                    for hf in range(2)], axis=1) for gi in range(2)]
            t_r = jnp.tanh(pre[0] + bg_ref[di, 0])
            t_i = jnp.tanh(pre[1] + bg_ref[di, 1])
            a = jnp.exp2(coef[di:di + 1] + coef[di:di + 1] * t_r)
            one_m_a2 = (1.0 - a) * (1.0 + a)
            mult = one_m_a2 * lax.rsqrt(jnp.maximum(one_m_a2, TINY))
            iu = hu + hu * t_i
            bx = mult * iu
            a_ref[di, rows, :] = a
            bx_ref[di, rows, :] = bx
            g_fix = (G_LEN - 1) if di else 0
            if r * GATE_SLABS <= g_fix < (r + 1) * GATE_SLABS:
                rel = slice((g_fix - r * GATE_SLABS) * N_CHUNK, (g_fix - r * GATE_SLABS + 1) * N_CHUNK)
                bx_ref[di, _slab_rows(g_fix), :] = jnp.where(last if di else first, iu[rel], bx[rel])

    def scan_slabs(di, state, g_range):
        h, p = state
        for g in g_range:
            rows = _slab_rows(g)
            ag = a_ref[di, rows, :]
            h = ag * h + bx_ref[di, rows, :]
            p = ag * p
            hl_ref[di, rows, :] = h
            pl_ref[di, rows, :] = p
        return h, p

    chunks = proj_chunks()

    def emit(n):
        for _ in range(min(n, len(chunks))):
            chunks.pop(0)()
    zero_state = (jnp.zeros((N_CHUNK, W_LRU), _f32), jnp.ones((N_CHUNK, W_LRU), _f32))

    emit(2)
    conv_group(0)
    state = zero_state
    for r in range(n_grp):
        emit(2)
        if r + 1 < n_grp:
            conv_group(r + 1)
        gate_group(r)
        if r >= 1:
            state = scan_slabs(0, state, range((r - 1) * GATE_SLABS, r * GATE_SLABS))
    emit(2)
    h_end, p_end = scan_slabs(0, state, range((n_grp - 1) * GATE_SLABS, G_LEN))
    c = jnp.where(sub == 0, edge(carry_ref[N_CHUNK - 1:N_CHUNK, :]), 0.0)
    for kk in range(1, N_CHUNK):
        c = jnp.where(sub == kk, pltpu.roll(p_end * c + h_end, 1, axis=0), c)
    carry_ref[...] = p_end * c + h_end

    emit(2)
    h_end, p_end = scan_slabs(1, zero_state, range(G_LEN - 1, -1, -1))
    cz = jnp.zeros((N_CHUNK, W_LRU), _f32)
    d = jnp.where(sub == N_CHUNK - 1, 1.0, 0.0)
    for kk in range(N_CHUNK - 2, -1, -1):
        cz = jnp.where(sub == kk, pltpu.roll(p_end * cz + h_end, N_CHUNK - 1, axis=0), cz)
        d = jnp.where(sub == kk, pltpu.roll(p_end * d, N_CHUNK - 1, axis=0), d)

    def h_bwd(g):
        return pl_ref[1, _slab_rows(g), :] * cz + hl_ref[1, _slab_rows(g), :]

    hb0f_ref[...] = h_bwd(0)
    emit(2)
    _unstage_slabs(lambda g: pl_ref[0, _slab_rows(g), :] * c + hl_ref[0, _slab_rows(g), :] + h_bwd(g),
                   ot_s, range(G_LEN))
    emit(2)
    _unstage_slabs(lambda g: pl_ref[1, _slab_rows(g), :] * d, ot_p, range(G_LEN))
    _copy_time_major(ot_s, hs_ref, range(N_CHUNK))
    _copy_time_major(ot_p, pb_ref, range(N_CHUNK))
    emit(len(chunks))


def _block_diag_halves(w):
    per_half = N_BLOCKS // 2
    eye = jnp.eye(per_half, dtype=w.dtype)
    w = w.reshape(w.shape[:-3] + (2, per_half, BLOCK_W, BLOCK_W))
    full = jnp.einsum('...hbde,bc->...hbdce', w, eye)
    return full.reshape(w.shape[:-3] + (per_half * BLOCK_W, per_half * BLOCK_W)).astype(_bf16)


def _front(x2, seq_len, ln_g, ln_b, w_in_bf16, conv_w, conv_b, gate_w, gate_b, lam, apply_ln):
    t = x2.shape[0]
    n = t // TM
    per = TM // SUBLANES
    row = lambda i: (i, 0)
    full = lambda a: pl.BlockSpec(a.shape, lambda i: (0,) * a.ndim)
    wg = _block_diag_halves(gate_w)
    bg = (0.5 * gate_b).reshape(2, 2, 1, W_LRU)
    conv_w = 0.5 * conv_w
    cb2 = (0.5 * conv_b).reshape(1, W_LRU)
    widths = (W_ATTN, W_ATTN, W_ATTN, W_ATTN, W_LRU, W_LRU, W_LRU)
    out_shape = [jax.ShapeDtypeStruct((t, w), _bf16) for w in widths]
    out_specs = [pl.BlockSpec((TM, w), row) for w in widths]
    out_shape.append(jax.ShapeDtypeStruct((n * N_CHUNK, W_LRU), _f32))
    out_specs.append(pl.BlockSpec((N_CHUNK, W_LRU), row))
    if apply_ln:
        out_shape = [jax.ShapeDtypeStruct((t, D_MODEL), _f32)] + out_shape
        out_specs = [pl.BlockSpec((TM, D_MODEL), row)] + out_specs
    stage = pltpu.VMEM((N_LG, N_CHUNK * PITCH, LANES), _f32)
    block_f32 = pltpu.VMEM((TM, W_LRU), _f32)
    return pl.pallas_call(
        functools.partial(_front_kernel, apply_ln=apply_ln, blocks_per_seq=seq_len // TM),
        grid=(n,),
        in_specs=[pl.BlockSpec((TM, D_MODEL), row),
                  pl.BlockSpec((SUBLANES, D_MODEL), lambda i: (jnp.minimum((i + 1) * per, n * per - 1), 0)),
                  full(ln_g), full(ln_b), full(w_in_bf16), full(conv_w), full(cb2),
                  full(wg), full(bg), full(lam)],
        out_specs=out_specs,
        out_shape=out_shape,
        scratch_shapes=[pltpu.VMEM((SUBLANES, W_LRU), _f32), pltpu.VMEM((N_CHUNK, W_LRU), _f32),
                        pltpu.VMEM((TM, D_MODEL), _bf16), block_f32, pltpu.VMEM((TM, W_LRU), _bf16),
                        pltpu.VMEM((2, TM, W_LRU), _f32), pltpu.VMEM((2, TM, W_LRU), _f32),
                        pltpu.VMEM((2, TM, W_LRU), _f32), pltpu.VMEM((2, TM, W_LRU), _f32),
                        stage, stage, stage],
        compiler_params=pltpu.CompilerParams(dimension_semantics=("arbitrary",),
                                             vmem_limit_bytes=VMEM_LIMIT),
        name="front",
    )(x2, x2, ln_g, ln_b, w_in_bf16, conv_w, cb2, wg, bg, lam)


def _attn_value_table(rpb):
    qc = np.arange(GRID_W)[:, None]
    kc = np.arange(GRID_W)[None, :]
    cs = np.clip(qc - WIN_W // 2, 0, GRID_W - WIN_W)
    ok = (kc >= cs) & (kc < cs + WIN_W)
    pad = GRID_W - WIN_W
    padded = jnp.pad(rpb.astype(_f32), ((0, 0), (0, 0), (pad, pad)))
    skew = jnp.tile(padded, (1, 1, GRID_W + 1))[:, :, :GRID_W * 2 * GRID_W]
    skew = skew.reshape(skew.shape[:2] + (GRID_W, 2 * GRID_W))
    t = skew[:, :, ::-1, :GRID_W]
    t = jnp.where(ok, t * LOG2_E, NEG)
    tiles = []
    for ct, k0 in enumerate(KC_START):
        for qr in range(SUB_ROWS):
            a0 = WIN_H // 2 - 1 - qr
            sub = t[:, a0:a0 + KEY_ROWS, ct * QC:(ct + 1) * QC, k0:k0 + KC]
            sub = sub.transpose(0, 2, 1, 3).reshape(t.shape[0], QC, KEY_ROWS * KC)
            tiles.append(jnp.pad(sub, ((0, 0), (0, 0), (0, KEYS_PAD - KEY_ROWS * KC)), constant_values=NEG))
    return jnp.stack(tiles, axis=1).reshape(t.shape[0], len(KC_START), SUB_ROWS, QC, KEYS_PAD)


def _attn_row_mask(rows):
    r_first = (0, Q_ROWS, rows - Q_ROWS)
    m = np.full((3, Q_ROWS // SUB_ROWS, SUB_ROWS, SUBLANES, KEYS_PAD), NEG, np.float32)
    m[..., KEY_ROWS * KC:] = 0.0
    for v, r0 in enumerate(r_first):
        for half in range(Q_ROWS // SUB_ROWS):
            for qr in range(SUB_ROWS):
                r = r0 + half * SUB_ROWS + qr
                rs = min(max(r - WIN_H // 2, 0), rows - WIN_H)
                for kr in range(KEY_ROWS):
                    key_r = r0 + half * SUB_ROWS - WIN_H // 2 + kr
                    if rs <= key_r < rs + WIN_H:
                        m[v, half, qr, :, kr * KC:(kr + 1) * KC] = 0.0
    return m


def _attn_kernel(q_ref, *refs):
    k_refs, v_refs = refs[:N_KV_BLKS], refs[N_KV_BLKS:2 * N_KV_BLKS]
    z_ref, val_ref, rowm_ref, o_ref, k32_ref, v32_ref = refs[2 * N_KV_BLKS:]
    m_q = SUB_ROWS * QC
    blk_tok = KV_BLK_ROWS * GRID_W
    lane = lax.broadcasted_iota(jnp.int32, (m_q, LANES), 1)
    even = lane < HEAD_DIM
    for j in range(N_KV_BLKS):
        k32_ref[j * blk_tok:(j + 1) * blk_tok, :] = k_refs[j][...].astype(_f32)
        v32_ref[j * blk_tok:(j + 1) * blk_tok, :] = v_refs[j][...].astype(_f32)
    pad = jnp.zeros((KEYS_PAD - KEY_ROWS * KC, LANES), _f32)

    def key_tile(src_ref, half, k0, cols):
        parts = [src_ref[(half * SUB_ROWS + kr) * GRID_W + k0:(half * SUB_ROWS + kr) * GRID_W + k0 + KC, cols]
                 for kr in range(KEY_ROWS)]
        return jnp.concatenate(parts + [pad], axis=0).astype(_bf16)

    def q_rows(half, ct):
        return [slice((half * SUB_ROWS + qr) * GRID_W + ct * QC, (half * SUB_ROWS + qr) * GRID_W + (ct + 1) * QC)
                for qr in range(SUB_ROWS)]

    def scores(tile):
        half, ct, pair = tile
        cols = slice(pair * LANES, (pair + 1) * LANES)
        kt = key_tile(k32_ref, half, KC_START[ct], cols)
        qp = jnp.concatenate([q_ref[r, cols] for r in q_rows(half, ct)], axis=0)
        zero = jnp.zeros_like(qp)
        q2 = jnp.concatenate([jnp.where(even, qp, zero), jnp.where(even, zero, qp)], axis=0)
        return lax.dot_general(q2, kt, (((1,), (1,)), ((), ())), preferred_element_type=_f32)

    def finish(tile, s):
        half, ct, pair = tile
        cols = slice(pair * LANES, (pair + 1) * LANES)
        blocks = []
        for parity in range(2):
            for qr in range(SUB_ROWS):
                r0 = (parity * SUB_ROWS + qr) * QC
                blk = s[r0:r0 + QC] + val_ref[2 * pair + parity, ct, qr]
                blk = blk.reshape(QC // SUBLANES, SUBLANES, -1) + rowm_ref[half, qr][None]
                blocks.append(blk.reshape(QC, -1))
        s = jnp.concatenate(blocks, axis=0)
        m = jnp.max(s, axis=-1, keepdims=True)
        e = jnp.exp2(s - m)
        l = jnp.sum(e, axis=-1, keepdims=True)
        o2 = _dot(e.astype(_bf16), key_tile(v32_ref, half, KC_START[ct], cols))
        o = jnp.where(even, o2[:m_q] / l[:m_q], o2[m_q:] / l[m_q:])
        z = jnp.concatenate([z_ref[r, cols] for r in q_rows(half, ct)], axis=0).astype(_f32)
        y = (o * z).astype(o_ref.dtype)
        for qr, r in enumerate(q_rows(half, ct)):
            o_ref[r, cols] = y[qr * QC:(qr + 1) * QC]

    tiles = [(half, ct, pair) for half in range(Q_ROWS // SUB_ROWS) for ct in range(len(KC_START))
             for pair in range(N_HEADS // 2)]
    ahead = [scores(tile) for tile in tiles[:SCORE_LOOKAHEAD]]
    for t, tile in enumerate(tiles):
        if t + SCORE_LOOKAHEAD < len(tiles):
            ahead.append(scores(tiles[t + SCORE_LOOKAHEAD]))
        finish(tile, ahead.pop(0))


def _attention(q, k, v, z_a, val, rowm):
    b, s, _ = q.shape
    rows = s // GRID_W
    n_blk = rows // Q_ROWS
    n_kv = rows // KV_BLK_ROWS
    tq = Q_ROWS * GRID_W
    tkv = KV_BLK_ROWS * GRID_W

    def kv_spec(j):
        first = -(WIN_H // 2) // KV_BLK_ROWS
        return pl.BlockSpec(
            (None, tkv, W_ATTN),
            lambda bi, i: (bi, jnp.clip(i * (Q_ROWS // KV_BLK_ROWS) + first + j, 0, n_kv - 1), 0))

    def case(i):
        return jnp.where(i == 0, 0, jnp.where(i == n_blk - 1, 2, 1))

    blk = pl.BlockSpec((None, tq, W_ATTN), lambda bi, i: (bi, i, 0))
    return pl.pallas_call(
        _attn_kernel,
        grid=(b, n_blk),
        in_specs=[blk] + [kv_spec(j) for j in range(N_KV_BLKS)] * 2 + [
            blk,
            pl.BlockSpec(val.shape, lambda bi, i: (0, 0, 0, 0, 0)),
            pl.BlockSpec((None,) + rowm.shape[1:], lambda bi, i: (case(i), 0, 0, 0, 0))],
        out_specs=blk,
        out_shape=jax.ShapeDtypeStruct((b, s, W_ATTN), _bf16),
        scratch_shapes=[pltpu.VMEM((N_KV_BLKS * tkv, W_ATTN), _f32)] * 2,
        compiler_params=pltpu.CompilerParams(dimension_semantics=("arbitrary", "arbitrary"),
                                             vmem_limit_bytes=VMEM_LIMIT),
        name="attn",
    )(q, *([k] * N_KV_BLKS), *([v] * N_KV_BLKS), z_a, val, rowm)


def _back_kernel(x_ref, ya_ref, hs_ref, pb_ref, hb0f_ref, zb_ref, wgm_ref, bm_ref, wba_ref, wbb_ref,
                 wo_ref, g_ref, b_ref, o_ref, carry_ref, *, alpha, blocks_per_seq):
    i = pl.program_id(0)

    @pl.when(i % blocks_per_seq == 0)
    def _():
        carry_ref[...] = jnp.zeros_like(carry_ref)

    carry = carry_ref[0:1, :]
    cin = [None] * (TB // TM)
    for sb in range(TB // TM - 1, -1, -1):
        cin[sb] = carry
        r0 = sb * TM
        carry = hb0f_ref[sb * N_CHUNK:sb * N_CHUNK + 1, :] + pb_ref[r0:r0 + 1, :].astype(_f32) * carry
    carry_ref[0:1, :] = carry

    def rows_of(c):
        return slice(c * BACK_ROWS, (c + 1) * BACK_ROWS)

    def branch_dots(c):
        rows = rows_of(c)
        h = hs_ref[rows, :].astype(_f32) + pb_ref[rows, :].astype(_f32) * cin[c * BACK_ROWS // TM]
        yb = (h * zb_ref[rows, :].astype(_f32)).astype(_bf16)
        return _dot(ya_ref[rows, :], wba_ref[...]), _dot(yb, wbb_ref[...])

    def gate_dots(c):
        xb = x_ref[rows_of(c), :].astype(_bf16)
        return [_sigmoid(_dot(xb, wgm_ref[:, c0:c0 + PROJ_COLS]) + bm_ref[:, c0:c0 + PROJ_COLS])
                for c0 in range(0, 2 * D_MODEL, PROJ_COLS)]

    def out_dot(c, gates, pa, pb):
        g = jnp.concatenate(gates, axis=1)
        m = g[:, :D_MODEL] * pa + g[:, D_MODEL:] * pb
        return _dot(m.astype(_bf16), wo_ref[...])

    def finish(c, out):
        rows = rows_of(c)
        o_ref[rows, :] = _layer_norm(alpha * x_ref[rows, :] + out, g_ref[...], b_ref[...])

    n_c = TB // BACK_ROWS
    ready = {0: (gate_dots(0),) + branch_dots(0)}
    outs = {}
    for c in range(n_c):
        if c + 1 < n_c:
            ready[c + 1] = (gate_dots(c + 1),) + branch_dots(c + 1)
        outs[c] = out_dot(c, *ready.pop(c))
        if c >= 1:
            finish(c - 1, outs.pop(c - 1))
    finish(n_c - 1, outs.pop(n_c - 1))


def _back(x2, seq_len, y_a, h_s, p_b, h_b0f, z_b, w_gm, b_merge, w_ba, w_bb, w_out, ln_g, ln_b, alpha):
    t = x2.shape[0]
    bm2 = b_merge.reshape(1, 2 * D_MODEL)
    n = t // TB
    rev = lambda i: (n - 1 - i, 0)
    const = lambda i: (0, 0)
    rows = lambda w: pl.BlockSpec((TB, w), rev)
    full = lambda a: pl.BlockSpec(a.shape, const)
    return pl.pallas_call(
        functools.partial(_back_kernel, alpha=alpha, blocks_per_seq=seq_len // TB),
        grid=(n,),
        in_specs=[rows(D_MODEL), rows(W_ATTN), rows(W_LRU), rows(W_LRU),
                  pl.BlockSpec((TB // TM * N_CHUNK, W_LRU), rev), rows(W_LRU),
                  full(w_gm), full(bm2), full(w_ba), full(w_bb), full(w_out), full(ln_g), full(ln_b)],
        out_specs=rows(D_MODEL),
        out_shape=jax.ShapeDtypeStruct((t, D_MODEL), _f32),
        scratch_shapes=[pltpu.VMEM((SUBLANES, W_LRU), _f32)],
        compiler_params=pltpu.CompilerParams(dimension_semantics=("arbitrary",),
                                             vmem_limit_bytes=VMEM_LIMIT),
        name="back",
    )(x2, y_a, h_s, p_b, h_b0f, z_b, w_gm, bm2, w_ba, w_bb, w_out, ln_g, ln_b)


def kernel(x, emb_ln_g, emb_ln_b, w_in, rpb, conv_w, conv_b, lru_gate_w, lru_gate_b, lru_lambda,
           w_branch_attn, w_branch_lru, b_merge, w_out, ln_g, ln_b):
    b, s, d = x.shape
    depth = w_in.shape[0]
    assert d == D_MODEL and s % (Q_ROWS * GRID_W) == 0 and s % TM == 0 and s % TB == 0 and TB % TM == 0
    alpha = (2 * depth) ** 0.25
    t = b * s
    h = x.reshape(t, d)
    row2 = lambda a: a.reshape(1, -1)
    rowm = _attn_row_mask(s // GRID_W)
    for l in range(depth):
        n_front = w_in.shape[-1] - 2 * D_MODEL
        outs = _front(h, s, row2(emb_ln_g), row2(emb_ln_b), w_in[l, :, :n_front].astype(_bf16), conv_w[l],
                      conv_b[l], lru_gate_w[l], lru_gate_b[l], lru_lambda[l], apply_ln=(l == 0))
        if l == 0:
            h, outs = outs[0], outs[1:]
        q, k, v, z_a, z_b, h_s, p_b, h_b0f = outs
        seq = lambda a: a.reshape(b, s, a.shape[-1])
        y_a = _attention(seq(q), seq(k), seq(v), seq(z_a), _attn_value_table(rpb[l]), rowm)
        h = _back(h, s, y_a.reshape(t, W_ATTN), h_s, p_b, h_b0f, z_b,
                  w_in[l, :, n_front:].astype(_bf16), b_merge[l],
                  w_branch_attn[l].astype(_bf16), w_branch_lru[l].astype(_bf16),
                  w_out[l].astype(_bf16), row2(ln_g[l]), row2(ln_b[l]), alpha)
    return h.reshape(b, s, d)
```

```python
import functools

import numpy as np
import jax
import jax.numpy as jnp
from jax import lax
from jax.experimental import pallas as pl
from jax.experimental.pallas import tpu as pltpu

D_MODEL = 1024
GRID_W = 64
N_HEADS = 8
HEAD_DIM = 64
W_ATTN = N_HEADS * HEAD_DIM
WIN_H = 8
WIN_W = 16
W_LRU = 512
N_BLOCKS = 8
BLOCK_W = 64
CONV_W = 4
C_LRU = 8.0
LN_EPS = 1e-5
NEG = -1e30
TINY = 1e-30
LOG2_E = 1.4426950408889634

LANES = 128
SUBLANES = 8
VMEM_LIMIT = 56 * 1024 * 1024

TM = 512
TB = 1024
BACK_ROWS = 256
Q_ROWS = 16
SUB_ROWS = 4
KEY_ROWS = SUB_ROWS + WIN_H
KV_BLK_ROWS = 4
N_KV_BLKS = (Q_ROWS + WIN_H) // KV_BLK_ROWS
QC = 32
KC = 40
KC_START = (0, GRID_W - KC)
KEYS_PAD = 512
SCORE_LOOKAHEAD = 1
N_CHUNK = SUBLANES
G_LEN = TM // N_CHUNK
PITCH = G_LEN + 8
N_LG = W_LRU // LANES
PROJ_COLS = 256
GATE_SLABS = 16

_f32 = jnp.float32
_bf16 = jnp.bfloat16


def _dot(a, b):
    return jnp.dot(a, b, preferred_element_type=_f32)


def _layer_norm(x, g, b):
    mu = jnp.mean(x, axis=-1, keepdims=True)
    xc = x - mu
    var = jnp.mean(xc * xc, axis=-1, keepdims=True)
    return xc * lax.rsqrt(var + LN_EPS) * g + b


def _sigmoid(x):
    return 0.5 + 0.5 * jnp.tanh(0.5 * x)


def _silu(x):
    hx = 0.5 * x
    return hx + hx * jnp.tanh(hx)


def _stage_time_major(u, ut_ref):
    for ch in range(N_CHUNK):
        for lg in range(N_LG):
            ut_ref[lg, ch * PITCH:ch * PITCH + G_LEN, :] = u[ch * G_LEN:(ch + 1) * G_LEN,
                                                            lg * LANES:(lg + 1) * LANES]


def _load_slab(ut_ref, g):
    return jnp.concatenate([ut_ref[lg, pl.ds(g, N_CHUNK, stride=PITCH), :] for lg in range(N_LG)], axis=1)


def _slab_rows(g):
    return slice(g * N_CHUNK, (g + 1) * N_CHUNK)


def _unstage_slabs(slab_fn, ot_ref, g_range):
    for g in g_range:
        slab = slab_fn(g)
        for lg in range(N_LG):
            ot_ref[lg, pl.ds(g, N_CHUNK, stride=PITCH), :] = slab[:, lg * LANES:(lg + 1) * LANES]


def _copy_time_major(ot_ref, o_ref, ch_range):
    for ch in ch_range:
        for lg in range(N_LG):
            o_ref[ch * G_LEN:(ch + 1) * G_LEN, lg * LANES:(lg + 1) * LANES] = (
                ot_ref[lg, ch * PITCH:ch * PITCH + G_LEN, :].astype(o_ref.dtype))


def _gate_dots(ub_ref, wg_ref):
    half = W_LRU // 2
    pre = []
    for gi in range(2):
        cols = [_dot(ub_ref[:, hf * half:(hf + 1) * half], wg_ref[gi, hf]) for hf in range(2)]
        pre.append(jnp.concatenate(cols, axis=1))
    return pre


def _gates(pre, hu_ref, bg_ref, coef, first, g_first, a_ref, bx_ref):
    t_r = jnp.tanh(pre[0] + bg_ref[0])
    t_i = jnp.tanh(pre[1] + bg_ref[1])
    a = jnp.exp2(coef + coef * t_r)
    one_m_a2 = (1.0 - a) * (1.0 + a)
    mult = one_m_a2 * lax.rsqrt(jnp.maximum(one_m_a2, TINY))
    hu = hu_ref[...]
    iu = hu + hu * t_i
    bx = mult * iu
    a_ref[...] = a
    bx_ref[...] = bx
    fix = _slab_rows(g_first)
    bx_ref[fix, :] = jnp.where(first, iu[fix], bx[fix])


def _chunk_scan(a_ref, bx_ref, hl_ref, pl_ref, g_range):
    h = jnp.zeros((N_CHUNK, W_LRU), _f32)
    p = jnp.ones((N_CHUNK, W_LRU), _f32)
    for g in g_range:
        rows = _slab_rows(g)
        ag = a_ref[rows, :]
        h = ag * h + bx_ref[rows, :]
        p = ag * p
        hl_ref[rows, :] = h
        pl_ref[rows, :] = p
    return h, p


def _front_kernel(x_ref, xn_ref, g_ref, b_ref, w_ref, cw_ref, cb_ref, wg_ref, bg_ref, lam_ref,
                  *refs, apply_ln, blocks_per_seq):
    n_out = 9 if apply_ln else 8
    out_refs = refs[:n_out]
    (uprev_ref, carry_ref, xb_ref, hu_ref, ub_ref, a_ref, bx_ref, hl_ref, pl_ref,
     ut_ref) = refs[n_out:]
    if apply_ln:
        h_ref, out_refs = out_refs[0], out_refs[1:]
    q_ref, k_ref, v_ref, za_ref, zb_ref, hs_ref, pb_ref, hb0f_ref = out_refs
    j = pl.program_id(0)
    is_first_blk = j % blocks_per_seq == 0
    is_last_blk = j % blocks_per_seq == blocks_per_seq - 1
    u_off = 4 * W_ATTN
    sub = lax.broadcasted_iota(jnp.int32, (N_CHUNK, W_LRU), 0)

    def edge(row):
        return jnp.broadcast_to(row, (N_CHUNK, W_LRU))

    def proj_chunks():
        def chunk(o_ref, off, post, c0):
            def run():
                y = _dot(xb_ref[...], w_ref[:, off + c0:off + c0 + PROJ_COLS])
                o_ref[:, c0:c0 + PROJ_COLS] = post(y, slice(c0, c0 + PROJ_COLS)).astype(o_ref.dtype)
            return run
        plan = ((q_ref, 0, lambda y, cols: y * (HEAD_DIM ** -0.5 * LOG2_E)),
                (k_ref, W_ATTN, lambda y, cols: y),
                (v_ref, 2 * W_ATTN, lambda y, cols: y),
                (za_ref, 3 * W_ATTN, lambda y, cols: _silu(y)),
                (zb_ref, u_off + W_LRU, lambda y, cols: _silu(y)))
        return [chunk(o_ref, off, post, c0) for o_ref, off, post in plan
                for c0 in range(0, o_ref.shape[-1], PROJ_COLS)]

    @pl.when(is_first_blk)
    def _():
        uprev_ref[...] = jnp.zeros_like(uprev_ref)
        carry_ref[...] = jnp.zeros_like(carry_ref)

    x = x_ref[...]
    xn = xn_ref[...]
    if apply_ln:
        x = _layer_norm(x, g_ref[...], b_ref[...])
        xn = _layer_norm(xn, g_ref[...], b_ref[...])
        h_ref[...] = x
    xb_ref[...] = x.astype(_bf16)
    u = _dot(xb_ref[...], w_ref[:, u_off:u_off + W_LRU])
    u_next = _dot(xn.astype(_bf16), w_ref[:, u_off:u_off + W_LRU])
    _stage_time_major(u, ut_ref)

    slabs = {}
    slabs[-2] = jnp.where(sub == 0, edge(uprev_ref[SUBLANES - 2:SUBLANES - 1, :]),
                          pltpu.roll(_load_slab(ut_ref, G_LEN - 2), 1, axis=0))
    slabs[-1] = jnp.where(sub == 0, edge(uprev_ref[SUBLANES - 1:SUBLANES, :]),
                          pltpu.roll(_load_slab(ut_ref, G_LEN - 1), 1, axis=0))
    nxt = jnp.where(is_last_blk, 0.0, u_next[0:1, :])
    slabs[G_LEN] = jnp.where(sub == N_CHUNK - 1, edge(nxt), pltpu.roll(_load_slab(ut_ref, 0), N_CHUNK - 1, axis=0))
    uprev_ref[...] = u[TM - SUBLANES:TM, :]

    def slab(g):
        if g not in slabs:
            slabs[g] = _load_slab(ut_ref, g)
        return slabs[g]

    cw = [cw_ref[i:i + 1, :] for i in range(CONV_W)]
    cb = cb_ref[...]
    neg_lam = -lam_ref[...]
    softplus = jnp.maximum(neg_lam, 0.0) + jnp.log(1.0 + jnp.exp(-jnp.abs(neg_lam)))
    coef = (-0.5 * C_LRU * LOG2_E) * softplus
    first = jnp.logical_and(is_first_blk, sub == 0)
    last = jnp.logical_and(is_last_blk, sub == N_CHUNK - 1)

    n_grp = G_LEN // GATE_SLABS
    grp_rows = lambda r: slice(r * GATE_SLABS * N_CHUNK, (r + 1) * GATE_SLABS * N_CHUNK)

    def conv_group(r):
        for g in range(r * GATE_SLABS, (r + 1) * GATE_SLABS):
            hu_ref[_slab_rows(g), :] = (slab(g - 2) * cw[0] + slab(g - 1) * cw[1] + slab(g) * cw[2]
                                        + slab(g + 1) * cw[3] + cb)
        ub_ref[grp_rows(r), :] = hu_ref[grp_rows(r), :].astype(_bf16)

    def gate_group(r):
        rows = grp_rows(r)
        half = W_LRU // 2
        hu = hu_ref[rows, :]
        for di in range(2):
            pre = [jnp.concatenate([_dot(ub_ref[rows, hf * half:(hf + 1) * half], wg_ref[di, gi, hf])
                                    for hf in range(2)], axis=1) for gi in range(2)]
            t_r = jnp.tanh(pre[0] + bg_ref[di, 0])
            t_i = jnp.tanh(pre[1] + bg_ref[di, 1])
            a = jnp.exp2(coef[di:di + 1] + coef[di:di + 1] * t_r)
            one_m_a2 = (1.0 - a) * (1.0 + a)
            mult = one_m_a2 * lax.rsqrt(jnp.maximum(one_m_a2, TINY))
            iu = hu + hu * t_i
            bx = mult * iu
            a_ref[di, rows, :] = a
            bx_ref[di, rows, :] = bx
            g_fix = (G_LEN - 1) if di else 0
            if r * GATE_SLABS <= g_fix < (r + 1) * GATE_SLABS:
                rel = slice((g_fix - r * GATE_SLABS) * N_CHUNK, (g_fix - r * GATE_SLABS + 1) * N_CHUNK)
                bx_ref[di, _slab_rows(g_fix), :] = jnp.where(last if di else first, iu[rel], bx[rel])

    def scan_slabs(di, state, g_range):
        h, p = state
        for g in g_range:
            rows = _slab_rows(g)
            ag = a_ref[di, rows, :]
            h = ag * h + bx_ref[di, rows, :]
            p = ag * p
            hl_ref[di, rows, :] = h
            pl_ref[di, rows, :] = p
        return h, p

    chunks = proj_chunks()

    def emit(n):
        for _ in range(min(n, len(chunks))):
            chunks.pop(0)()
    zero_state = (jnp.zeros((N_CHUNK, W_LRU), _f32), jnp.ones((N_CHUNK, W_LRU), _f32))

    emit(2)
    conv_group(0)
    state = zero_state
    for r in range(n_grp):
        emit(2)
        if r + 1 < n_grp:
            conv_group(r + 1)
        gate_group(r)
        if r >= 1:
            state = scan_slabs(0, state, range((r - 1) * GATE_SLABS, r * GATE_SLABS))
    emit(2)
    h_end, p_end = scan_slabs(0, state, range((n_grp - 1) * GATE_SLABS, G_LEN))
    c = jnp.where(sub == 0, edge(carry_ref[N_CHUNK - 1:N_CHUNK, :]), 0.0)
    for kk in range(1, N_CHUNK):
        c = jnp.where(sub == kk, pltpu.roll(p_end * c + h_end, 1, axis=0), c)
    carry_ref[...] = p_end * c + h_end

    emit(2)
    h_end, p_end = scan_slabs(1, zero_state, range(G_LEN - 1, -1, -1))
    cz = jnp.zeros((N_CHUNK, W_LRU), _f32)
    d = jnp.where(sub == N_CHUNK - 1, 1.0, 0.0)
    for kk in range(N_CHUNK - 2, -1, -1):
        cz = jnp.where(sub == kk, pltpu.roll(p_end * cz + h_end, N_CHUNK - 1, axis=0), cz)
        d = jnp.where(sub == kk, pltpu.roll(p_end * d, N_CHUNK - 1, axis=0), d)

    def h_bwd(g):
        return pl_ref[1, _slab_rows(g), :] * cz + hl_ref[1, _slab_rows(g), :]

    hb0f_ref[...] = h_bwd(0)
    emit(len(chunks))
    for g in range(0, G_LEN, 2):
        rows = slice(g * N_CHUNK, (g + 2) * N_CHUNK)
        hs = [pl_ref[0, _slab_rows(gg), :] * c + hl_ref[0, _slab_rows(gg), :] + h_bwd(gg) for gg in (g, g + 1)]
        hs_ref[rows, :] = jnp.concatenate(hs, axis=0).astype(hs_ref.dtype)
        pb = [pl_ref[1, _slab_rows(gg), :] * d for gg in (g, g + 1)]
        pb_ref[rows, :] = jnp.concatenate(pb, axis=0).astype(pb_ref.dtype)


def _block_diag_halves(w):
    per_half = N_BLOCKS // 2
    eye = jnp.eye(per_half, dtype=w.dtype)
    w = w.reshape(w.shape[:-3] + (2, per_half, BLOCK_W, BLOCK_W))
    full = jnp.einsum('...hbde,bc->...hbdce', w, eye)
    return full.reshape(w.shape[:-3] + (per_half * BLOCK_W, per_half * BLOCK_W)).astype(_bf16)


def _front(x2, seq_len, ln_g, ln_b, w_in_bf16, conv_w, conv_b, gate_w, gate_b, lam, apply_ln):
    t = x2.shape[0]
    n = t // TM
    per = TM // SUBLANES
    row = lambda i: (i, 0)
    full = lambda a: pl.BlockSpec(a.shape, lambda i: (0,) * a.ndim)
    wg = _block_diag_halves(gate_w)
    bg = (0.5 * gate_b).reshape(2, 2, 1, W_LRU)
    conv_w = 0.5 * conv_w
    cb2 = (0.5 * conv_b).reshape(1, W_LRU)
    widths = (W_ATTN, W_ATTN, W_ATTN, W_ATTN, W_LRU, W_LRU, W_LRU)
    out_shape = [jax.ShapeDtypeStruct((t, w), _bf16) for w in widths]
    out_specs = [pl.BlockSpec((TM, w), row) for w in widths]
    out_shape.append(jax.ShapeDtypeStruct((n * N_CHUNK, W_LRU), _f32))
    out_specs.append(pl.BlockSpec((N_CHUNK, W_LRU), row))
    if apply_ln:
        out_shape = [jax.ShapeDtypeStruct((t, D_MODEL), _f32)] + out_shape
        out_specs = [pl.BlockSpec((TM, D_MODEL), row)] + out_specs
    stage = pltpu.VMEM((N_LG, N_CHUNK * PITCH, LANES), _f32)
    block_f32 = pltpu.VMEM((TM, W_LRU), _f32)
    return pl.pallas_call(
        functools.partial(_front_kernel, apply_ln=apply_ln, blocks_per_seq=seq_len // TM),
        grid=(n,),
        in_specs=[pl.BlockSpec((TM, D_MODEL), row),
                  pl.BlockSpec((SUBLANES, D_MODEL), lambda i: (jnp.minimum((i + 1) * per, n * per - 1), 0)),
                  full(ln_g), full(ln_b), full(w_in_bf16), full(conv_w), full(cb2),
                  full(wg), full(bg), full(lam)],
        out_specs=out_specs,
        out_shape=out_shape,
        scratch_shapes=[pltpu.VMEM((SUBLANES, W_LRU), _f32), pltpu.VMEM((N_CHUNK, W_LRU), _f32),
                        pltpu.VMEM((TM, D_MODEL), _bf16), block_f32, pltpu.VMEM((TM, W_LRU), _bf16),
                        pltpu.VMEM((2, TM, W_LRU), _f32), pltpu.VMEM((2, TM, W_LRU), _f32),
                        pltpu.VMEM((2, TM, W_LRU), _f32), pltpu.VMEM((2, TM, W_LRU), _f32), stage],
        compiler_params=pltpu.CompilerParams(dimension_semantics=("arbitrary",),
                                             vmem_limit_bytes=VMEM_LIMIT),
        name="front",
    )(x2, x2, ln_g, ln_b, w_in_bf16, conv_w, cb2, wg, bg, lam)


def _attn_value_table(rpb):
    qc = np.arange(GRID_W)[:, None]
    kc = np.arange(GRID_W)[None, :]
    cs = np.clip(qc - WIN_W // 2, 0, GRID_W - WIN_W)
    ok = (kc >= cs) & (kc < cs + WIN_W)
    pad = GRID_W - WIN_W
    padded = jnp.pad(rpb.astype(_f32), ((0, 0), (0, 0), (pad, pad)))
    skew = jnp.tile(padded, (1, 1, GRID_W + 1))[:, :, :GRID_W * 2 * GRID_W]
    skew = skew.reshape(skew.shape[:2] + (GRID_W, 2 * GRID_W))
    t = skew[:, :, ::-1, :GRID_W]
    t = jnp.where(ok, t * LOG2_E, NEG)
    tiles = []
    for ct, k0 in enumerate(KC_START):
        for qr in range(SUB_ROWS):
            a0 = WIN_H // 2 - 1 - qr
            sub = t[:, a0:a0 + KEY_ROWS, ct * QC:(ct + 1) * QC, k0:k0 + KC]
            sub = sub.transpose(0, 2, 1, 3).reshape(t.shape[0], QC, KEY_ROWS * KC)
            tiles.append(jnp.pad(sub, ((0, 0), (0, 0), (0, KEYS_PAD - KEY_ROWS * KC)), constant_values=NEG))
    return jnp.stack(tiles, axis=1).reshape(t.shape[0], len(KC_START), SUB_ROWS, QC, KEYS_PAD)


def _attn_row_mask(rows):
    r_first = (0, Q_ROWS, rows - Q_ROWS)
    m = np.full((3, Q_ROWS // SUB_ROWS, SUB_ROWS, SUBLANES, KEYS_PAD), NEG, np.float32)
    m[..., KEY_ROWS * KC:] = 0.0
    for v, r0 in enumerate(r_first):
        for half in range(Q_ROWS // SUB_ROWS):
            for qr in range(SUB_ROWS):
                r = r0 + half * SUB_ROWS + qr
                rs = min(max(r - WIN_H // 2, 0), rows - WIN_H)
                for kr in range(KEY_ROWS):
                    key_r = r0 + half * SUB_ROWS - WIN_H // 2 + kr
                    if rs <= key_r < rs + WIN_H:
                        m[v, half, qr, :, kr * KC:(kr + 1) * KC] = 0.0
    return m


def _attn_kernel(q_ref, *refs):
    k_refs, v_refs = refs[:N_KV_BLKS], refs[N_KV_BLKS:2 * N_KV_BLKS]
    z_ref, val_ref, rowm_ref, o_ref, k32_ref, v32_ref = refs[2 * N_KV_BLKS:]
    m_q = SUB_ROWS * QC
    blk_tok = KV_BLK_ROWS * GRID_W
    lane = lax.broadcasted_iota(jnp.int32, (m_q, LANES), 1)
    even = lane < HEAD_DIM
    for j in range(N_KV_BLKS):
        k32_ref[j * blk_tok:(j + 1) * blk_tok, :] = k_refs[j][...].astype(_f32)
        v32_ref[j * blk_tok:(j + 1) * blk_tok, :] = v_refs[j][...].astype(_f32)
    pad = jnp.zeros((KEYS_PAD - KEY_ROWS * KC, LANES), _f32)

    def key_tile(src_ref, half, k0, cols):
        parts = [src_ref[(half * SUB_ROWS + kr) * GRID_W + k0:(half * SUB_ROWS + kr) * GRID_W + k0 + KC, cols]
                 for kr in range(KEY_ROWS)]
        return jnp.concatenate(parts + [pad], axis=0).astype(_bf16)

    def q_rows(half, ct):
        return [slice((half * SUB_ROWS + qr) * GRID_W + ct * QC, (half * SUB_ROWS + qr) * GRID_W + (ct + 1) * QC)
                for qr in range(SUB_ROWS)]

    def scores(tile):
        half, ct, pair = tile
        cols = slice(pair * LANES, (pair + 1) * LANES)
        kt = key_tile(k32_ref, half, KC_START[ct], cols)
        qp = jnp.concatenate([q_ref[r, cols] for r in q_rows(half, ct)], axis=0)
        zero = jnp.zeros_like(qp)
        q2 = jnp.concatenate([jnp.where(even, qp, zero), jnp.where(even, zero, qp)], axis=0)
        return lax.dot_general(q2, kt, (((1,), (1,)), ((), ())), preferred_element_type=_f32)

    def finish(tile, s):
        half, ct, pair = tile
        cols = slice(pair * LANES, (pair + 1) * LANES)
        blocks = []
        for parity in range(2):
            for qr in range(SUB_ROWS):
                r0 = (parity * SUB_ROWS + qr) * QC
                blk = s[r0:r0 + QC] + val_ref[2 * pair + parity, ct, qr]
                blk = blk.reshape(QC // SUBLANES, SUBLANES, -1) + rowm_ref[half, qr][None]
                blocks.append(blk.reshape(QC, -1))
        s = jnp.concatenate(blocks, axis=0)
        m = jnp.max(s, axis=-1, keepdims=True)
        e = jnp.exp2(s - m)
        l = jnp.sum(e, axis=-1, keepdims=True)
        o2 = _dot(e.astype(_bf16), key_tile(v32_ref, half, KC_START[ct], cols))
        o = jnp.where(even, o2[:m_q] / l[:m_q], o2[m_q:] / l[m_q:])
        z = jnp.concatenate([z_ref[r, cols] for r in q_rows(half, ct)], axis=0).astype(_f32)
        y = (o * z).astype(o_ref.dtype)
        for qr, r in enumerate(q_rows(half, ct)):
            o_ref[r, cols] = y[qr * QC:(qr + 1) * QC]

    tiles = [(half, ct, pair) for half in range(Q_ROWS // SUB_ROWS) for ct in range(len(KC_START))
             for pair in range(N_HEADS // 2)]
    ahead = [scores(tile) for tile in tiles[:SCORE_LOOKAHEAD]]
    for t, tile in enumerate(tiles):
        if t + SCORE_LOOKAHEAD < len(tiles):
            ahead.append(scores(tiles[t + SCORE_LOOKAHEAD]))
        finish(tile, ahead.pop(0))


def _attention(q, k, v, z_a, val, rowm):
    b, s, _ = q.shape
    rows = s // GRID_W
    n_blk = rows // Q_ROWS
    n_kv = rows // KV_BLK_ROWS
    tq = Q_ROWS * GRID_W
    tkv = KV_BLK_ROWS * GRID_W

    def kv_spec(j):
        first = -(WIN_H // 2) // KV_BLK_ROWS
        return pl.BlockSpec(
            (None, tkv, W_ATTN),
            lambda bi, i: (bi, jnp.clip(i * (Q_ROWS // KV_BLK_ROWS) + first + j, 0, n_kv - 1), 0))

    def case(i):
        return jnp.where(i == 0, 0, jnp.where(i == n_blk - 1, 2, 1))

    blk = pl.BlockSpec((None, tq, W_ATTN), lambda bi, i: (bi, i, 0))
    return pl.pallas_call(
        _attn_kernel,
        grid=(b, n_blk),
        in_specs=[blk] + [kv_spec(j) for j in range(N_KV_BLKS)] * 2 + [
            blk,
            pl.BlockSpec(val.shape, lambda bi, i: (0, 0, 0, 0, 0)),
            pl.BlockSpec((None,) + rowm.shape[1:], lambda bi, i: (case(i), 0, 0, 0, 0))],
        out_specs=blk,
        out_shape=jax.ShapeDtypeStruct((b, s, W_ATTN), _bf16),
        scratch_shapes=[pltpu.VMEM((N_KV_BLKS * tkv, W_ATTN), _f32)] * 2,
        compiler_params=pltpu.CompilerParams(dimension_semantics=("arbitrary", "arbitrary"),
                                             vmem_limit_bytes=VMEM_LIMIT),
        name="attn",
    )(q, *([k] * N_KV_BLKS), *([v] * N_KV_BLKS), z_a, val, rowm)


def _back_kernel(x_ref, ya_ref, hs_ref, pb_ref, hb0f_ref, zb_ref, wgm_ref, bm_ref, wba_ref, wbb_ref,
                 wo_ref, g_ref, b_ref, o_ref, carry_ref, st_ref, *, alpha, blocks_per_seq):
    i = pl.program_id(0)

    @pl.when(i % blocks_per_seq == 0)
    def _():
        carry_ref[...] = jnp.zeros_like(carry_ref)

    carry = carry_ref[0:1, :]
    cin = [None] * (TB // TM)
    for sb in range(TB // TM - 1, -1, -1):
        cin[sb] = carry
        r0 = sb * TM
        carry = hb0f_ref[sb * N_CHUNK:sb * N_CHUNK + 1, :] + pb_ref[r0:r0 + 1, :].astype(_f32) * carry
    carry_ref[0:1, :] = carry

    def rows_of(c):
        return slice(c * BACK_ROWS, (c + 1) * BACK_ROWS)

    for sb in range(TB // TM):
        for g in range(G_LEN):
            rows = slice(sb * TM + g * N_CHUNK, sb * TM + (g + 1) * N_CHUNK)
            slab = hs_ref[rows, :].astype(_f32) + pb_ref[rows, :].astype(_f32) * cin[sb]
            for lg in range(N_LG):
                st_ref[sb * N_LG + lg, pl.ds(g, N_CHUNK, stride=PITCH), :] = slab[:, lg * LANES:(lg + 1) * LANES]

    def h_time_major(rows):
        sb, r0 = divmod(rows.start, TM)
        parts = []
        for ch in range(r0 // G_LEN, (r0 + BACK_ROWS) // G_LEN):
            parts.append(jnp.concatenate([st_ref[sb * N_LG + lg, ch * PITCH:ch * PITCH + G_LEN, :]
                                          for lg in range(N_LG)], axis=1))
        return jnp.concatenate(parts, axis=0)

    def branch_dots(c):
        rows = rows_of(c)
        yb = (h_time_major(rows) * zb_ref[rows, :].astype(_f32)).astype(_bf16)
        return _dot(ya_ref[rows, :], wba_ref[...]), _dot(yb, wbb_ref[...])

    def gate_dots(c):
        xb = x_ref[rows_of(c), :].astype(_bf16)
        return [_sigmoid(_dot(xb, wgm_ref[:, c0:c0 + PROJ_COLS]) + bm_ref[:, c0:c0 + PROJ_COLS])
                for c0 in range(0, 2 * D_MODEL, PROJ_COLS)]

    def out_dot(c, gates, pa, pb):
        g = jnp.concatenate(gates, axis=1)
        m = g[:, :D_MODEL] * pa + g[:, D_MODEL:] * pb
        return _dot(m.astype(_bf16), wo_ref[...])

    def finish(c, out):
        rows = rows_of(c)
        o_ref[rows, :] = _layer_norm(alpha * x_ref[rows, :] + out, g_ref[...], b_ref[...])

    n_c = TB // BACK_ROWS
    ready = {0: (gate_dots(0),) + branch_dots(0)}
    outs = {}
    for c in range(n_c):
        if c + 1 < n_c:
            ready[c + 1] = (gate_dots(c + 1),) + branch_dots(c + 1)
        outs[c] = out_dot(c, *ready.pop(c))
        if c >= 1:
            finish(c - 1, outs.pop(c - 1))
    finish(n_c - 1, outs.pop(n_c - 1))


def _back(x2, seq_len, y_a, h_s, p_b, h_b0f, z_b, w_gm, b_merge, w_ba, w_bb, w_out, ln_g, ln_b, alpha):
    t = x2.shape[0]
    bm2 = b_merge.reshape(1, 2 * D_MODEL)
    n = t // TB
    rev = lambda i: (n - 1 - i, 0)
    const = lambda i: (0, 0)
    rows = lambda w: pl.BlockSpec((TB, w), rev)
    full = lambda a: pl.BlockSpec(a.shape, const)
    return pl.pallas_call(
        functools.partial(_back_kernel, alpha=alpha, blocks_per_seq=seq_len // TB),
        grid=(n,),
        in_specs=[rows(D_MODEL), rows(W_ATTN), rows(W_LRU), rows(W_LRU),
                  pl.BlockSpec((TB // TM * N_CHUNK, W_LRU), rev), rows(W_LRU),
                  full(w_gm), full(bm2), full(w_ba), full(w_bb), full(w_out), full(ln_g), full(ln_b)],
        out_specs=rows(D_MODEL),
        out_shape=jax.ShapeDtypeStruct((t, D_MODEL), _f32),
        scratch_shapes=[pltpu.VMEM((SUBLANES, W_LRU), _f32),
                        pltpu.VMEM((TB // TM * N_LG, N_CHUNK * PITCH, LANES), _f32)],
        compiler_params=pltpu.CompilerParams(dimension_semantics=("arbitrary",),
                                             vmem_limit_bytes=VMEM_LIMIT),
        name="back",
    )(x2, y_a, h_s, p_b, h_b0f, z_b, w_gm, bm2, w_ba, w_bb, w_out, ln_g, ln_b)


def kernel(x, emb_ln_g, emb_ln_b, w_in, rpb, conv_w, conv_b, lru_gate_w, lru_gate_b, lru_lambda,
           w_branch_attn, w_branch_lru, b_merge, w_out, ln_g, ln_b):
    b, s, d = x.shape
    depth = w_in.shape[0]
    assert d == D_MODEL and s % (Q_ROWS * GRID_W) == 0 and s % TM == 0 and s % TB == 0 and TB % TM == 0
    alpha = (2 * depth) ** 0.25
    t = b * s
    h = x.reshape(t, d)
    row2 = lambda a: a.reshape(1, -1)
    rowm = _attn_row_mask(s // GRID_W)
    for l in range(depth):
        n_front = w_in.shape[-1] - 2 * D_MODEL
        outs = _front(h, s, row2(emb_ln_g), row2(emb_ln_b), w_in[l, :, :n_front].astype(_bf16), conv_w[l],
                      conv_b[l], lru_gate_w[l], lru_gate_b[l], lru_lambda[l], apply_ln=(l == 0))
        if l == 0:
            h, outs = outs[0], outs[1:]
        q, k, v, z_a, z_b, h_s, p_b, h_b0f = outs
        seq = lambda a: a.reshape(b, s, a.shape[-1])
        y_a = _attention(seq(q), seq(k), seq(v), seq(z_a), _attn_value_table(rpb[l]), rowm)
        h = _back(h, s, y_a.reshape(t, W_ATTN), h_s, p_b, h_b0f, z_b,
                  w_in[l, :, n_front:].astype(_bf16), b_merge[l],
                  w_branch_attn[l].astype(_bf16), w_branch_lru[l].astype(_bf16),
                  w_out[l].astype(_bf16), row2(ln_g[l]), row2(ln_b[l]), alpha)
    return h.reshape(b, s, d)
```

```python
import functools

import numpy as np
import jax
import jax.numpy as jnp
from jax import lax
from jax.experimental import pallas as pl
from jax.experimental.pallas import tpu as pltpu

D_MODEL = 1024
GRID_W = 64
N_HEADS = 8
HEAD_DIM = 64
W_ATTN = N_HEADS * HEAD_DIM
WIN_H = 8
WIN_W = 16
W_LRU = 512
N_BLOCKS = 8
BLOCK_W = 64
CONV_W = 4
C_LRU = 8.0
LN_EPS = 1e-5
NEG = -1e30
TINY = 1e-30
LOG2_E = 1.4426950408889634

LANES = 128
SUBLANES = 8
VMEM_LIMIT = 56 * 1024 * 1024

TM = 512
TB = 1024
BACK_ROWS = 256
Q_ROWS = 16
SUB_ROWS = 4
KEY_ROWS = SUB_ROWS + WIN_H
KV_BLK_ROWS = 4
N_KV_BLKS = (Q_ROWS + WIN_H) // KV_BLK_ROWS
QC = 32
KC = 40
KC_START = (0, GRID_W - KC)
KEYS_PAD = 512
SCORE_LOOKAHEAD = 1
N_CHUNK = SUBLANES
G_LEN = TM // N_CHUNK
PITCH = G_LEN + 8
N_LG = W_LRU // LANES
PROJ_COLS = 256
GATE_SLABS = 16

_f32 = jnp.float32
_bf16 = jnp.bfloat16


def _dot(a, b):
    return jnp.dot(a, b, preferred_element_type=_f32)


def _layer_norm(x, g, b):
    mu = jnp.mean(x, axis=-1, keepdims=True)
    xc = x - mu
    var = jnp.mean(xc * xc, axis=-1, keepdims=True)
    return xc * lax.rsqrt(var + LN_EPS) * g + b


def _sigmoid(x):
    return 0.5 + 0.5 * jnp.tanh(0.5 * x)


def _silu(x):
    hx = 0.5 * x
    return hx + hx * jnp.tanh(hx)


def _stage_time_major(u, ut_ref):
    for ch in range(N_CHUNK):
        for lg in range(N_LG):
            ut_ref[lg, ch * PITCH:ch * PITCH + G_LEN, :] = u[ch * G_LEN:(ch + 1) * G_LEN,
                                                            lg * LANES:(lg + 1) * LANES]


def _load_slab(ut_ref, g):
    return jnp.concatenate([ut_ref[lg, pl.ds(g, N_CHUNK, stride=PITCH), :] for lg in range(N_LG)], axis=1)


def _slab_rows(g):
    return slice(g * N_CHUNK, (g + 1) * N_CHUNK)


def _unstage_slabs(slab_fn, ot_ref, g_range):
    for g in g_range:
        slab = slab_fn(g)
        for lg in range(N_LG):
            ot_ref[lg, pl.ds(g, N_CHUNK, stride=PITCH), :] = slab[:, lg * LANES:(lg + 1) * LANES]


def _copy_time_major(ot_ref, o_ref, ch_range):
    for ch in ch_range:
        for lg in range(N_LG):
            o_ref[ch * G_LEN:(ch + 1) * G_LEN, lg * LANES:(lg + 1) * LANES] = (
                ot_ref[lg, ch * PITCH:ch * PITCH + G_LEN, :].astype(o_ref.dtype))


def _gate_dots(ub_ref, wg_ref):
    half = W_LRU // 2
    pre = []
    for gi in range(2):
        cols = [_dot(ub_ref[:, hf * half:(hf + 1) * half], wg_ref[gi, hf]) for hf in range(2)]
        pre.append(jnp.concatenate(cols, axis=1))
    return pre


def _gates(pre, hu_ref, bg_ref, coef, first, g_first, a_ref, bx_ref):
    t_r = jnp.tanh(pre[0] + bg_ref[0])
    t_i = jnp.tanh(pre[1] + bg_ref[1])
    a = jnp.exp2(coef + coef * t_r)
    one_m_a2 = (1.0 - a) * (1.0 + a)
    mult = one_m_a2 * lax.rsqrt(jnp.maximum(one_m_a2, TINY))
    hu = hu_ref[...].astype(_f32)
    iu = hu + hu * t_i
    bx = mult * iu
    a_ref[...] = a
    bx_ref[...] = bx
    if first is not None:
        fix = _slab_rows(g_first)
        bx_ref[fix, :] = jnp.where(first, iu[fix], bx[fix])


def _chunk_scan(a_ref, bx_ref, hl_ref, pl_ref, g_range):
    h = jnp.zeros((N_CHUNK, W_LRU), _f32)
    p = jnp.ones((N_CHUNK, W_LRU), _f32)
    for g in g_range:
        rows = _slab_rows(g)
        ag = a_ref[rows, :]
        h = ag * h + bx_ref[rows, :]
        p = ag * p
        hl_ref[rows, :] = h
        pl_ref[rows, :] = p
    return h, p


def _front_kernel(x_ref, xn_ref, g_ref, b_ref, w_ref, cw_ref, cb_ref, wg_ref, bg_ref, lam_ref,
                  *refs, apply_ln, blocks_per_seq):
    n_out = 8 if apply_ln else 7
    out_refs = refs[:n_out]
    (uprev_ref, carry_ref, xb_ref, hu_ref, ub_ref, a_ref, bx_ref, hl_ref, pl_ref,
     ut_ref) = refs[n_out:]
    if apply_ln:
        h_ref, out_refs = out_refs[0], out_refs[1:]
    q_ref, k_ref, v_ref, za_ref, zb_ref, hf_ref, hu_out_ref = out_refs
    j = pl.program_id(0)
    is_first_blk = j % blocks_per_seq == 0
    is_last_blk = j % blocks_per_seq == blocks_per_seq - 1
    u_off = 4 * W_ATTN
    sub = lax.broadcasted_iota(jnp.int32, (N_CHUNK, W_LRU), 0)

    def edge(row):
        return jnp.broadcast_to(row, (N_CHUNK, W_LRU))

    def proj_chunks():
        def chunk(o_ref, off, post, c0):
            def run():
                y = _dot(xb_ref[...], w_ref[:, off + c0:off + c0 + PROJ_COLS])
                o_ref[:, c0:c0 + PROJ_COLS] = post(y, slice(c0, c0 + PROJ_COLS)).astype(o_ref.dtype)
            return run
        plan = ((q_ref, 0, lambda y, cols: y * (HEAD_DIM ** -0.5 * LOG2_E)),
                (k_ref, W_ATTN, lambda y, cols: y),
                (v_ref, 2 * W_ATTN, lambda y, cols: y),
                (za_ref, 3 * W_ATTN, lambda y, cols: _silu(y)),
                (zb_ref, u_off + W_LRU, lambda y, cols: _silu(y)))
        return [chunk(o_ref, off, post, c0) for o_ref, off, post in plan
                for c0 in range(0, o_ref.shape[-1], PROJ_COLS)]

    @pl.when(is_first_blk)
    def _():
        uprev_ref[...] = jnp.zeros_like(uprev_ref)
        carry_ref[...] = jnp.zeros_like(carry_ref)

    x = x_ref[...]
    xn = xn_ref[...]
    if apply_ln:
        x = _layer_norm(x, g_ref[...], b_ref[...])
        xn = _layer_norm(xn, g_ref[...], b_ref[...])
        h_ref[...] = x
    xb_ref[...] = x.astype(_bf16)
    u = _dot(xb_ref[...], w_ref[:, u_off:u_off + W_LRU])
    u_next = _dot(xn.astype(_bf16), w_ref[:, u_off:u_off + W_LRU])
    _stage_time_major(u, ut_ref)

    slabs = {}
    slabs[-2] = jnp.where(sub == 0, edge(uprev_ref[SUBLANES - 2:SUBLANES - 1, :]),
                          pltpu.roll(_load_slab(ut_ref, G_LEN - 2), 1, axis=0))
    slabs[-1] = jnp.where(sub == 0, edge(uprev_ref[SUBLANES - 1:SUBLANES, :]),
                          pltpu.roll(_load_slab(ut_ref, G_LEN - 1), 1, axis=0))
    nxt = jnp.where(is_last_blk, 0.0, u_next[0:1, :])
    slabs[G_LEN] = jnp.where(sub == N_CHUNK - 1, edge(nxt), pltpu.roll(_load_slab(ut_ref, 0), N_CHUNK - 1, axis=0))
    uprev_ref[...] = u[TM - SUBLANES:TM, :]

    def slab(g):
        if g not in slabs:
            slabs[g] = _load_slab(ut_ref, g)
        return slabs[g]

    cw = [cw_ref[i:i + 1, :] for i in range(CONV_W)]
    cb = cb_ref[...]
    neg_lam = -lam_ref[...]
    softplus = jnp.maximum(neg_lam, 0.0) + jnp.log(1.0 + jnp.exp(-jnp.abs(neg_lam)))
    coef = (-0.5 * C_LRU * LOG2_E) * softplus
    first = jnp.logical_and(is_first_blk, sub == 0)

    n_grp = G_LEN // GATE_SLABS
    grp_rows = lambda r: slice(r * GATE_SLABS * N_CHUNK, (r + 1) * GATE_SLABS * N_CHUNK)

    def conv_group(r):
        for g in range(r * GATE_SLABS, (r + 1) * GATE_SLABS):
            hu_ref[_slab_rows(g), :] = (slab(g - 2) * cw[0] + slab(g - 1) * cw[1] + slab(g) * cw[2]
                                        + slab(g + 1) * cw[3] + cb)
        ub_ref[grp_rows(r), :] = hu_ref[grp_rows(r), :].astype(_bf16)

    def gate_group(r):
        rows = grp_rows(r)
        half = W_LRU // 2
        hu = hu_ref[rows, :]
        for di in (0,):
            pre = [jnp.concatenate([_dot(ub_ref[rows, hf * half:(hf + 1) * half], wg_ref[di, gi, hf])
                                    for hf in range(2)], axis=1) for gi in range(2)]
            t_r = jnp.tanh(pre[0] + bg_ref[di, 0])
            t_i = jnp.tanh(pre[1] + bg_ref[di, 1])
            a = jnp.exp2(coef[di:di + 1] + coef[di:di + 1] * t_r)
            one_m_a2 = (1.0 - a) * (1.0 + a)
            mult = one_m_a2 * lax.rsqrt(jnp.maximum(one_m_a2, TINY))
            iu = hu + hu * t_i
            bx = mult * iu
            a_ref[di, rows, :] = a
            bx_ref[di, rows, :] = bx
            if r == 0:
                bx_ref[di, _slab_rows(0), :] = jnp.where(first, iu[_slab_rows(0)], bx[_slab_rows(0)])

    def scan_slabs(di, state, g_range):
        h, p = state
        for g in g_range:
            rows = _slab_rows(g)
            ag = a_ref[di, rows, :]
            h = ag * h + bx_ref[di, rows, :]
            p = ag * p
            hl_ref[di, rows, :] = h
            pl_ref[di, rows, :] = p
        return h, p

    chunks = proj_chunks()

    def emit(n):
        for _ in range(min(n, len(chunks))):
            chunks.pop(0)()
    zero_state = (jnp.zeros((N_CHUNK, W_LRU), _f32), jnp.ones((N_CHUNK, W_LRU), _f32))

    emit(2)
    conv_group(0)
    state = zero_state
    for r in range(n_grp):
        emit(2)
        if r + 1 < n_grp:
            conv_group(r + 1)
        gate_group(r)
        if r >= 1:
            state = scan_slabs(0, state, range((r - 1) * GATE_SLABS, r * GATE_SLABS))
    emit(2)
    h_end, p_end = scan_slabs(0, state, range((n_grp - 1) * GATE_SLABS, G_LEN))
    c = jnp.where(sub == 0, edge(carry_ref[N_CHUNK - 1:N_CHUNK, :]), 0.0)
    for kk in range(1, N_CHUNK):
        c = jnp.where(sub == kk, pltpu.roll(p_end * c + h_end, 1, axis=0), c)
    carry_ref[...] = p_end * c + h_end

    emit(len(chunks))
    for g in range(0, G_LEN, 2):
        rows = slice(g * N_CHUNK, (g + 2) * N_CHUNK)
        hf = [pl_ref[0, _slab_rows(gg), :] * c + hl_ref[0, _slab_rows(gg), :] for gg in (g, g + 1)]
        hf_ref[rows, :] = jnp.concatenate(hf, axis=0).astype(hf_ref.dtype)
    hu_out_ref[...] = ub_ref[...]


def _block_diag_halves(w):
    per_half = N_BLOCKS // 2
    eye = jnp.eye(per_half, dtype=w.dtype)
    w = w.reshape(w.shape[:-3] + (2, per_half, BLOCK_W, BLOCK_W))
    full = jnp.einsum('...hbde,bc->...hbdce', w, eye)
    return full.reshape(w.shape[:-3] + (per_half * BLOCK_W, per_half * BLOCK_W)).astype(_bf16)


def _front(x2, seq_len, ln_g, ln_b, w_in_bf16, conv_w, conv_b, wg, bg, lam, apply_ln):
    t = x2.shape[0]
    n = t // TM
    per = TM // SUBLANES
    row = lambda i: (i, 0)
    full = lambda a: pl.BlockSpec(a.shape, lambda i: (0,) * a.ndim)
    conv_w = 0.5 * conv_w
    cb2 = (0.5 * conv_b).reshape(1, W_LRU)
    widths = (W_ATTN, W_ATTN, W_ATTN, W_ATTN, W_LRU, W_LRU, W_LRU)
    out_shape = [jax.ShapeDtypeStruct((t, w), _bf16) for w in widths]
    out_specs = [pl.BlockSpec((TM, w), row) for w in widths]
    if apply_ln:
        out_shape = [jax.ShapeDtypeStruct((t, D_MODEL), _f32)] + out_shape
        out_specs = [pl.BlockSpec((TM, D_MODEL), row)] + out_specs
    stage = pltpu.VMEM((N_LG, N_CHUNK * PITCH, LANES), _f32)
    block_f32 = pltpu.VMEM((TM, W_LRU), _f32)
    return pl.pallas_call(
        functools.partial(_front_kernel, apply_ln=apply_ln, blocks_per_seq=seq_len // TM),
        grid=(n,),
        in_specs=[pl.BlockSpec((TM, D_MODEL), row),
                  pl.BlockSpec((SUBLANES, D_MODEL), lambda i: (jnp.minimum((i + 1) * per, n * per - 1), 0)),
                  full(ln_g), full(ln_b), full(w_in_bf16), full(conv_w), full(cb2),
                  full(wg), full(bg), full(lam)],
        out_specs=out_specs,
        out_shape=out_shape,
        scratch_shapes=[pltpu.VMEM((SUBLANES, W_LRU), _f32), pltpu.VMEM((N_CHUNK, W_LRU), _f32),
                        pltpu.VMEM((TM, D_MODEL), _bf16), block_f32, pltpu.VMEM((TM, W_LRU), _bf16),
                        pltpu.VMEM((1, TM, W_LRU), _f32), pltpu.VMEM((1, TM, W_LRU), _f32),
                        pltpu.VMEM((1, TM, W_LRU), _f32), pltpu.VMEM((1, TM, W_LRU), _f32), stage],
        compiler_params=pltpu.CompilerParams(dimension_semantics=("arbitrary",),
                                             vmem_limit_bytes=VMEM_LIMIT),
        name="front",
    )(x2, x2, ln_g, ln_b, w_in_bf16, conv_w, cb2, wg, bg, lam)


def _attn_value_table(rpb):
    qc = np.arange(GRID_W)[:, None]
    kc = np.arange(GRID_W)[None, :]
    cs = np.clip(qc - WIN_W // 2, 0, GRID_W - WIN_W)
    ok = (kc >= cs) & (kc < cs + WIN_W)
    pad = GRID_W - WIN_W
    padded = jnp.pad(rpb.astype(_f32), ((0, 0), (0, 0), (pad, pad)))
    skew = jnp.tile(padded, (1, 1, GRID_W + 1))[:, :, :GRID_W * 2 * GRID_W]
    skew = skew.reshape(skew.shape[:2] + (GRID_W, 2 * GRID_W))
    t = skew[:, :, ::-1, :GRID_W]
    t = jnp.where(ok, t * LOG2_E, NEG)
    tiles = []
    for ct, k0 in enumerate(KC_START):
        for qr in range(SUB_ROWS):
            a0 = WIN_H // 2 - 1 - qr
            sub = t[:, a0:a0 + KEY_ROWS, ct * QC:(ct + 1) * QC, k0:k0 + KC]
            sub = sub.transpose(0, 2, 1, 3).reshape(t.shape[0], QC, KEY_ROWS * KC)
            tiles.append(jnp.pad(sub, ((0, 0), (0, 0), (0, KEYS_PAD - KEY_ROWS * KC)), constant_values=NEG))
    return jnp.stack(tiles, axis=1).reshape(t.shape[0], len(KC_START), SUB_ROWS, QC, KEYS_PAD)


def _attn_row_mask(rows):
    r_first = (0, Q_ROWS, rows - Q_ROWS)
    m = np.full((3, Q_ROWS // SUB_ROWS, SUB_ROWS, SUBLANES, KEYS_PAD), NEG, np.float32)
    m[..., KEY_ROWS * KC:] = 0.0
    for v, r0 in enumerate(r_first):
        for half in range(Q_ROWS // SUB_ROWS):
            for qr in range(SUB_ROWS):
                r = r0 + half * SUB_ROWS + qr
                rs = min(max(r - WIN_H // 2, 0), rows - WIN_H)
                for kr in range(KEY_ROWS):
                    key_r = r0 + half * SUB_ROWS - WIN_H // 2 + kr
                    if rs <= key_r < rs + WIN_H:
                        m[v, half, qr, :, kr * KC:(kr + 1) * KC] = 0.0
    return m


def _attn_kernel(q_ref, *refs):
    k_refs, v_refs = refs[:N_KV_BLKS], refs[N_KV_BLKS:2 * N_KV_BLKS]
    z_ref, val_ref, rowm_ref, o_ref, k32_ref, v32_ref = refs[2 * N_KV_BLKS:]
    m_q = SUB_ROWS * QC
    blk_tok = KV_BLK_ROWS * GRID_W
    lane = lax.broadcasted_iota(jnp.int32, (m_q, LANES), 1)
    even = lane < HEAD_DIM
    for j in range(N_KV_BLKS):
        k32_ref[j * blk_tok:(j + 1) * blk_tok, :] = k_refs[j][...].astype(_f32)
        v32_ref[j * blk_tok:(j + 1) * blk_tok, :] = v_refs[j][...].astype(_f32)
    pad = jnp.zeros((KEYS_PAD - KEY_ROWS * KC, LANES), _f32)

    def key_tile(src_ref, half, k0, cols):
        parts = [src_ref[(half * SUB_ROWS + kr) * GRID_W + k0:(half * SUB_ROWS + kr) * GRID_W + k0 + KC, cols]
                 for kr in range(KEY_ROWS)]
        return jnp.concatenate(parts + [pad], axis=0).astype(_bf16)

    def q_rows(half, ct):
        return [slice((half * SUB_ROWS + qr) * GRID_W + ct * QC, (half * SUB_ROWS + qr) * GRID_W + (ct + 1) * QC)
                for qr in range(SUB_ROWS)]

    def scores(tile):
        half, ct, pair = tile
        cols = slice(pair * LANES, (pair + 1) * LANES)
        kt = key_tile(k32_ref, half, KC_START[ct], cols)
        qp = jnp.concatenate([q_ref[r, cols] for r in q_rows(half, ct)], axis=0)
        zero = jnp.zeros_like(qp)
        q2 = jnp.concatenate([jnp.where(even, qp, zero), jnp.where(even, zero, qp)], axis=0)
        return lax.dot_general(q2, kt, (((1,), (1,)), ((), ())), preferred_element_type=_f32)

    def finish(tile, s):
        half, ct, pair = tile
        cols = slice(pair * LANES, (pair + 1) * LANES)
        blocks = []
        for parity in range(2):
            for qr in range(SUB_ROWS):
                r0 = (parity * SUB_ROWS + qr) * QC
                blk = s[r0:r0 + QC] + val_ref[2 * pair + parity, ct, qr]
                blk = blk.reshape(QC // SUBLANES, SUBLANES, -1) + rowm_ref[half, qr][None]
                blocks.append(blk.reshape(QC, -1))
        s = jnp.concatenate(blocks, axis=0)
        m = jnp.max(s, axis=-1, keepdims=True)
        e = jnp.exp2(s - m)
        l = jnp.sum(e, axis=-1, keepdims=True)
        o2 = _dot(e.astype(_bf16), key_tile(v32_ref, half, KC_START[ct], cols))
        o = jnp.where(even, o2[:m_q] / l[:m_q], o2[m_q:] / l[m_q:])
        z = jnp.concatenate([z_ref[r, cols] for r in q_rows(half, ct)], axis=0).astype(_f32)
        y = (o * z).astype(o_ref.dtype)
        for qr, r in enumerate(q_rows(half, ct)):
            o_ref[r, cols] = y[qr * QC:(qr + 1) * QC]

    tiles = [(half, ct, pair) for half in range(Q_ROWS // SUB_ROWS) for ct in range(len(KC_START))
             for pair in range(N_HEADS // 2)]
    ahead = [scores(tile) for tile in tiles[:SCORE_LOOKAHEAD]]
    for t, tile in enumerate(tiles):
        if t + SCORE_LOOKAHEAD < len(tiles):
            ahead.append(scores(tiles[t + SCORE_LOOKAHEAD]))
        finish(tile, ahead.pop(0))


def _attention(q, k, v, z_a, val, rowm):
    b, s, _ = q.shape
    rows = s // GRID_W
    n_blk = rows // Q_ROWS
    n_kv = rows // KV_BLK_ROWS
    tq = Q_ROWS * GRID_W
    tkv = KV_BLK_ROWS * GRID_W

    def kv_spec(j):
        first = -(WIN_H // 2) // KV_BLK_ROWS
        return pl.BlockSpec(
            (None, tkv, W_ATTN),
            lambda bi, i: (bi, jnp.clip(i * (Q_ROWS // KV_BLK_ROWS) + first + j, 0, n_kv - 1), 0))

    def case(i):
        return jnp.where(i == 0, 0, jnp.where(i == n_blk - 1, 2, 1))

    blk = pl.BlockSpec((None, tq, W_ATTN), lambda bi, i: (bi, i, 0))
    return pl.pallas_call(
        _attn_kernel,
        grid=(b, n_blk),
        in_specs=[blk] + [kv_spec(j) for j in range(N_KV_BLKS)] * 2 + [
            blk,
            pl.BlockSpec(val.shape, lambda bi, i: (0, 0, 0, 0, 0)),
            pl.BlockSpec((None,) + rowm.shape[1:], lambda bi, i: (case(i), 0, 0, 0, 0))],
        out_specs=blk,
        out_shape=jax.ShapeDtypeStruct((b, s, W_ATTN), _bf16),
        scratch_shapes=[pltpu.VMEM((N_KV_BLKS * tkv, W_ATTN), _f32)] * 2,
        compiler_params=pltpu.CompilerParams(dimension_semantics=("arbitrary", "arbitrary"),
                                             vmem_limit_bytes=VMEM_LIMIT),
        name="attn",
    )(q, *([k] * N_KV_BLKS), *([v] * N_KV_BLKS), z_a, val, rowm)


def _back_kernel(x_ref, ya_ref, hf_ref, hu_ref, zb_ref, wg_ref, bg_ref, lam_ref, wgm_ref, bm_ref, wba_ref,
                 wbb_ref, wo_ref, g_ref, b_ref, o_ref, carry_ref, st_ref, a_ref, bx_ref, hl_ref, pl_ref,
                 *, alpha, blocks_per_seq):
    i = pl.program_id(0)
    is_last_blk = i % blocks_per_seq == 0

    @pl.when(is_last_blk)
    def _():
        carry_ref[...] = jnp.zeros_like(carry_ref)

    def rows_of(c):
        return slice(c * BACK_ROWS, (c + 1) * BACK_ROWS)

    sub = lax.broadcasted_iota(jnp.int32, (N_CHUNK, W_LRU), 0)
    neg_lam = -lam_ref[...]
    softplus = jnp.maximum(neg_lam, 0.0) + jnp.log(1.0 + jnp.exp(-jnp.abs(neg_lam)))
    coef = (-0.5 * C_LRU * LOG2_E) * softplus
    n_sb = TB // TM

    def reverse_gate_dots(sb):
        return _gate_dots(hu_ref.at[sb * TM:(sb + 1) * TM, :], wg_ref)

    def reverse_scan(sb, pre):
        last = jnp.logical_and(is_last_blk, sub == N_CHUNK - 1) if sb == n_sb - 1 else None
        _gates(pre, hu_ref.at[sb * TM:(sb + 1) * TM, :], bg_ref, coef, last, G_LEN - 1, a_ref, bx_ref)
        h_end, p_end = _chunk_scan(a_ref, bx_ref, hl_ref, pl_ref, range(G_LEN - 1, -1, -1))
        cz = jnp.where(sub == N_CHUNK - 1, jnp.broadcast_to(carry_ref[0:1, :], (N_CHUNK, W_LRU)), 0.0)
        for kk in range(N_CHUNK - 2, -1, -1):
            cz = jnp.where(sub == kk, pltpu.roll(p_end * cz + h_end, N_CHUNK - 1, axis=0), cz)
        carry_ref[...] = p_end * cz + h_end
        for g in range(G_LEN):
            rows = slice(sb * TM + g * N_CHUNK, sb * TM + (g + 1) * N_CHUNK)
            slab = hf_ref[rows, :].astype(_f32) + pl_ref[_slab_rows(g), :] * cz + hl_ref[_slab_rows(g), :]
            for lg in range(N_LG):
                st_ref[sb * N_LG + lg, pl.ds(g, N_CHUNK, stride=PITCH), :] = slab[:, lg * LANES:(lg + 1) * LANES]

    def h_time_major(rows):
        sb, r0 = divmod(rows.start, TM)
        parts = []
        for ch in range(r0 // G_LEN, (r0 + BACK_ROWS) // G_LEN):
            parts.append(jnp.concatenate([st_ref[sb * N_LG + lg, ch * PITCH:ch * PITCH + G_LEN, :]
                                          for lg in range(N_LG)], axis=1))
        return jnp.concatenate(parts, axis=0)

    def branch_dots(c):
        rows = rows_of(c)
        yb = (h_time_major(rows) * zb_ref[rows, :].astype(_f32)).astype(_bf16)
        return _dot(ya_ref[rows, :], wba_ref[...]), _dot(yb, wbb_ref[...])

    def gate_dots(c):
        xb = x_ref[rows_of(c), :].astype(_bf16)
        return [_sigmoid(_dot(xb, wgm_ref[:, c0:c0 + PROJ_COLS]) + bm_ref[:, c0:c0 + PROJ_COLS])
                for c0 in range(0, 2 * D_MODEL, PROJ_COLS)]

    def out_dot(c, gates, pa, pb):
        g = jnp.concatenate(gates, axis=1)
        m = g[:, :D_MODEL] * pa + g[:, D_MODEL:] * pb
        return _dot(m.astype(_bf16), wo_ref[...])

    def finish(c, out):
        rows = rows_of(c)
        o_ref[rows, :] = _layer_norm(alpha * x_ref[rows, :] + out, g_ref[...], b_ref[...])

    def chunks_of(sb):
        per = TM // BACK_ROWS
        return range((sb + 1) * per - 1, sb * per - 1, -1)

    gates, outs, prev = {}, {}, None
    pre = reverse_gate_dots(n_sb - 1)
    for c in chunks_of(n_sb - 1):
        gates[c] = gate_dots(c)
    for sb in range(n_sb - 1, -1, -1):
        reverse_scan(sb, pre)
        if sb > 0:
            pre = reverse_gate_dots(sb - 1)
        branch = {c: branch_dots(c) for c in chunks_of(sb)}
        ahead = list(chunks_of(sb - 1)) if sb > 0 else []
        for c in chunks_of(sb):
            if ahead:
                nxt = ahead.pop(0)
                gates[nxt] = gate_dots(nxt)
            outs[c] = out_dot(c, gates.pop(c), *branch.pop(c))
            if prev is not None:
                finish(prev, outs.pop(prev))
            prev = c
    finish(prev, outs.pop(prev))


def _back(x2, seq_len, y_a, h_f, hu, z_b, wg, bg, lam, w_gm, b_merge, w_ba, w_bb, w_out, ln_g, ln_b, alpha):
    t = x2.shape[0]
    bm2 = b_merge.reshape(1, 2 * D_MODEL)
    n = t // TB
    rev = lambda i: (n - 1 - i, 0)
    rows = lambda w: pl.BlockSpec((TB, w), rev)
    full = lambda a: pl.BlockSpec(a.shape, lambda i: (0,) * a.ndim)
    block_f32 = pltpu.VMEM((TM, W_LRU), _f32)
    return pl.pallas_call(
        functools.partial(_back_kernel, alpha=alpha, blocks_per_seq=seq_len // TB),
        grid=(n,),
        in_specs=[rows(D_MODEL), rows(W_ATTN), rows(W_LRU), rows(W_LRU), rows(W_LRU),
                  full(wg), full(bg), full(lam),
                  full(w_gm), full(bm2), full(w_ba), full(w_bb), full(w_out), full(ln_g), full(ln_b)],
        out_specs=rows(D_MODEL),
        out_shape=jax.ShapeDtypeStruct((t, D_MODEL), _f32),
        scratch_shapes=[pltpu.VMEM((SUBLANES, W_LRU), _f32),
                        pltpu.VMEM((TB // TM * N_LG, N_CHUNK * PITCH, LANES), _f32),
                        block_f32, block_f32, block_f32, block_f32],
        compiler_params=pltpu.CompilerParams(dimension_semantics=("arbitrary",),
                                             vmem_limit_bytes=VMEM_LIMIT),
        name="back",
    )(x2, y_a, h_f, hu, z_b, wg, bg, lam, w_gm, bm2, w_ba, w_bb, w_out, ln_g, ln_b)


def kernel(x, emb_ln_g, emb_ln_b, w_in, rpb, conv_w, conv_b, lru_gate_w, lru_gate_b, lru_lambda,
           w_branch_attn, w_branch_lru, b_merge, w_out, ln_g, ln_b):
    b, s, d = x.shape
    depth = w_in.shape[0]
    assert d == D_MODEL and s % (Q_ROWS * GRID_W) == 0 and s % TM == 0 and s % TB == 0 and TB % TM == 0
    alpha = (2 * depth) ** 0.25
    t = b * s
    h = x.reshape(t, d)
    row2 = lambda a: a.reshape(1, -1)
    rowm = _attn_row_mask(s // GRID_W)
    for l in range(depth):
        n_front = w_in.shape[-1] - 2 * D_MODEL
        wg = _block_diag_halves(lru_gate_w[l])
        bg = (0.5 * lru_gate_b[l]).reshape(2, 2, 1, W_LRU)
        lam = lru_lambda[l]
        outs = _front(h, s, row2(emb_ln_g), row2(emb_ln_b), w_in[l, :, :n_front].astype(_bf16), conv_w[l],
                      conv_b[l], wg[:1], bg[:1], lam[:1], apply_ln=(l == 0))
        if l == 0:
            h, outs = outs[0], outs[1:]
        q, k, v, z_a, z_b, h_f, hu = outs
        seq = lambda a: a.reshape(b, s, a.shape[-1])
        y_a = _attention(seq(q), seq(k), seq(v), seq(z_a), _attn_value_table(rpb[l]), rowm)
        h = _back(h, s, y_a.reshape(t, W_ATTN), h_f, hu, z_b, wg[1], bg[1], lam[1:],
                  w_in[l, :, n_front:].astype(_bf16), b_merge[l],
                  w_branch_attn[l].astype(_bf16), w_branch_lru[l].astype(_bf16),
                  w_out[l].astype(_bf16), row2(ln_g[l]), row2(ln_b[l]), alpha)
    return h.reshape(b, s, d)
```

```python
import functools

import numpy as np
import jax
import jax.numpy as jnp
from jax import lax
from jax.experimental import pallas as pl
from jax.experimental.pallas import tpu as pltpu

D_MODEL = 1024
GRID_W = 64
N_HEADS = 8
HEAD_DIM = 64
W_ATTN = N_HEADS * HEAD_DIM
WIN_H = 8
WIN_W = 16
W_LRU = 512
N_BLOCKS = 8
BLOCK_W = 64
CONV_W = 4
C_LRU = 8.0
LN_EPS = 1e-5
NEG = -1e30
TINY = 1e-30
LOG2_E = 1.4426950408889634

LANES = 128
SUBLANES = 8
VMEM_LIMIT = 56 * 1024 * 1024

TM = 512
TB = 1024
BACK_ROWS = 256
Q_ROWS = 16
SUB_ROWS = 4
KEY_ROWS = SUB_ROWS + WIN_H
KV_BLK_ROWS = 4
N_KV_BLKS = (Q_ROWS + WIN_H) // KV_BLK_ROWS
QC = 32
KC = 40
KC_START = (0, GRID_W - KC)
KEYS_PAD = 512
SCORE_LOOKAHEAD = 1
N_CHUNK = SUBLANES
G_LEN = TM // N_CHUNK
PITCH = G_LEN + 8
N_LG = W_LRU // LANES
PROJ_COLS = 256
GATE_SLABS = 16

_f32 = jnp.float32
_bf16 = jnp.bfloat16


def _dot(a, b):
    return jnp.dot(a, b, preferred_element_type=_f32)


def _layer_norm(x, g, b):
    mu = jnp.mean(x, axis=-1, keepdims=True)
    xc = x - mu
    var = jnp.mean(xc * xc, axis=-1, keepdims=True)
    return xc * lax.rsqrt(var + LN_EPS) * g + b


def _sigmoid(x):
    return 0.5 + 0.5 * jnp.tanh(0.5 * x)


def _silu(x):
    hx = 0.5 * x
    return hx + hx * jnp.tanh(hx)


def _stage_time_major(u, ut_ref):
    for ch in range(N_CHUNK):
        for lg in range(N_LG):
            ut_ref[lg, ch * PITCH:ch * PITCH + G_LEN, :] = u[ch * G_LEN:(ch + 1) * G_LEN,
                                                            lg * LANES:(lg + 1) * LANES]


def _load_slab(ut_ref, g):
    return jnp.concatenate([ut_ref[lg, pl.ds(g, N_CHUNK, stride=PITCH), :] for lg in range(N_LG)], axis=1)


def _slab_rows(g):
    return slice(g * N_CHUNK, (g + 1) * N_CHUNK)


def _front_kernel(x_ref, xn_ref, g_ref, b_ref, w_ref, cw_ref, cb_ref, wg_ref, bg_ref, lam_ref,
                  *refs, apply_ln, blocks_per_seq):
    n_out = 9 if apply_ln else 8
    out_refs = refs[:n_out]
    (uprev_ref, carry_ref, xb_ref, hu_ref, ub_ref, a_ref, bx_ref, hl_ref, pl_ref,
     ut_ref) = refs[n_out:]
    if apply_ln:
        h_ref, out_refs = out_refs[0], out_refs[1:]
    q_ref, k_ref, v_ref, za_ref, zb_ref, hs_ref, pb_ref, hb0f_ref = out_refs
    j = pl.program_id(0)
    is_first_blk = j % blocks_per_seq == 0
    is_last_blk = j % blocks_per_seq == blocks_per_seq - 1
    u_off = 4 * W_ATTN
    sub = lax.broadcasted_iota(jnp.int32, (N_CHUNK, W_LRU), 0)

    def edge(row):
        return jnp.broadcast_to(row, (N_CHUNK, W_LRU))

    def proj_chunks():
        def chunk(o_ref, off, post, c0):
            def run():
                y = _dot(xb_ref[...], w_ref[:, off + c0:off + c0 + PROJ_COLS])
                o_ref[:, c0:c0 + PROJ_COLS] = post(y, slice(c0, c0 + PROJ_COLS)).astype(o_ref.dtype)
            return run
        plan = ((q_ref, 0, lambda y, cols: y * (HEAD_DIM ** -0.5 * LOG2_E)),
                (k_ref, W_ATTN, lambda y, cols: y),
                (v_ref, 2 * W_ATTN, lambda y, cols: y),
                (za_ref, 3 * W_ATTN, lambda y, cols: _silu(y)),
                (zb_ref, u_off + W_LRU, lambda y, cols: _silu(y)))
        return [chunk(o_ref, off, post, c0) for o_ref, off, post in plan
                for c0 in range(0, o_ref.shape[-1], PROJ_COLS)]

    @pl.when(is_first_blk)
    def _():
        uprev_ref[...] = jnp.zeros_like(uprev_ref)
        carry_ref[...] = jnp.zeros_like(carry_ref)

    x = x_ref[...]
    xn = xn_ref[...]
    if apply_ln:
        x = _layer_norm(x, g_ref[...], b_ref[...])
        xn = _layer_norm(xn, g_ref[...], b_ref[...])
        h_ref[...] = x
    xb_ref[...] = x.astype(_bf16)
    u = _dot(xb_ref[...], w_ref[:, u_off:u_off + W_LRU])
    u_next = _dot(xn.astype(_bf16), w_ref[:, u_off:u_off + W_LRU])
    _stage_time_major(u, ut_ref)

    slabs = {}
    slabs[-2] = jnp.where(sub == 0, edge(uprev_ref[SUBLANES - 2:SUBLANES - 1, :]),
                          pltpu.roll(_load_slab(ut_ref, G_LEN - 2), 1, axis=0))
    slabs[-1] = jnp.where(sub == 0, edge(uprev_ref[SUBLANES - 1:SUBLANES, :]),
                          pltpu.roll(_load_slab(ut_ref, G_LEN - 1), 1, axis=0))
    nxt = jnp.where(is_last_blk, 0.0, u_next[0:1, :])
    slabs[G_LEN] = jnp.where(sub == N_CHUNK - 1, edge(nxt), pltpu.roll(_load_slab(ut_ref, 0), N_CHUNK - 1, axis=0))
    uprev_ref[...] = u[TM - SUBLANES:TM, :]

    def slab(g):
        if g not in slabs:
            slabs[g] = _load_slab(ut_ref, g)
        return slabs[g]

    cw = [cw_ref[i:i + 1, :] for i in range(CONV_W)]
    cb = cb_ref[...]
    neg_lam = -lam_ref[...]
    softplus = jnp.maximum(neg_lam, 0.0) + jnp.log(1.0 + jnp.exp(-jnp.abs(neg_lam)))
    coef = (-0.5 * C_LRU * LOG2_E) * softplus
    first = jnp.logical_and(is_first_blk, sub == 0)
    last = jnp.logical_and(is_last_blk, sub == N_CHUNK - 1)

    n_grp = G_LEN // GATE_SLABS
    grp_rows = lambda r: slice(r * GATE_SLABS * N_CHUNK, (r + 1) * GATE_SLABS * N_CHUNK)

    def conv_group(r):
        for g in range(r * GATE_SLABS, (r + 1) * GATE_SLABS):
            hu_ref[_slab_rows(g), :] = (slab(g - 2) * cw[0] + slab(g - 1) * cw[1] + slab(g) * cw[2]
                                        + slab(g + 1) * cw[3] + cb)
        ub_ref[grp_rows(r), :] = hu_ref[grp_rows(r), :].astype(_bf16)

    def gate_group(r):
        rows = grp_rows(r)
        half = W_LRU // 2
        hu = hu_ref[rows, :]
        for di in range(2):
            pre = [jnp.concatenate([_dot(ub_ref[rows, hf * half:(hf + 1) * half], wg_ref[di, gi, hf])
                                    for hf in range(2)], axis=1) for gi in range(2)]
            t_r = jnp.tanh(pre[0] + bg_ref[di, 0])
            t_i = jnp.tanh(pre[1] + bg_ref[di, 1])
            a = jnp.exp2(coef[di:di + 1] + coef[di:di + 1] * t_r)
            one_m_a2 = (1.0 - a) * (1.0 + a)
            mult = one_m_a2 * lax.rsqrt(jnp.maximum(one_m_a2, TINY))
            iu = hu + hu * t_i
            bx = mult * iu
            a_ref[di, rows, :] = a
            bx_ref[di, rows, :] = bx
            g_fix = (G_LEN - 1) if di else 0
            if r * GATE_SLABS <= g_fix < (r + 1) * GATE_SLABS:
                rel = slice((g_fix - r * GATE_SLABS) * N_CHUNK, (g_fix - r * GATE_SLABS + 1) * N_CHUNK)
                bx_ref[di, _slab_rows(g_fix), :] = jnp.where(last if di else first, iu[rel], bx[rel])

    def scan_slabs(di, state, g_range):
        h, p = state
        for g in g_range:
            rows = _slab_rows(g)
            ag = a_ref[di, rows, :]
            h = ag * h + bx_ref[di, rows, :]
            p = ag * p
            hl_ref[di, rows, :] = h
            pl_ref[di, rows, :] = p
        return h, p

    chunks = proj_chunks()

    def emit(n):
        for _ in range(min(n, len(chunks))):
            chunks.pop(0)()
    zero_state = (jnp.zeros((N_CHUNK, W_LRU), _f32), jnp.ones((N_CHUNK, W_LRU), _f32))

    emit(2)
    conv_group(0)
    state = zero_state
    for r in range(n_grp):
        emit(2)
        if r + 1 < n_grp:
            conv_group(r + 1)
        gate_group(r)
        if r >= 1:
            state = scan_slabs(0, state, range((r - 1) * GATE_SLABS, r * GATE_SLABS))
    emit(2)
    h_end, p_end = scan_slabs(0, state, range((n_grp - 1) * GATE_SLABS, G_LEN))
    c = jnp.where(sub == 0, edge(carry_ref[N_CHUNK - 1:N_CHUNK, :]), 0.0)
    for kk in range(1, N_CHUNK):
        c = jnp.where(sub == kk, pltpu.roll(p_end * c + h_end, 1, axis=0), c)
    carry_ref[...] = p_end * c + h_end

    emit(2)
    h_end, p_end = scan_slabs(1, zero_state, range(G_LEN - 1, -1, -1))
    cz = jnp.zeros((N_CHUNK, W_LRU), _f32)
    d = jnp.where(sub == N_CHUNK - 1, 1.0, 0.0)
    for kk in range(N_CHUNK - 2, -1, -1):
        cz = jnp.where(sub == kk, pltpu.roll(p_end * cz + h_end, N_CHUNK - 1, axis=0), cz)
        d = jnp.where(sub == kk, pltpu.roll(p_end * d, N_CHUNK - 1, axis=0), d)

    def h_bwd(g):
        return pl_ref[1, _slab_rows(g), :] * cz + hl_ref[1, _slab_rows(g), :]

    hb0f_ref[...] = h_bwd(0)
    emit(len(chunks))
    for g in range(0, G_LEN, 2):
        rows = slice(g * N_CHUNK, (g + 2) * N_CHUNK)
        hs = [pl_ref[0, _slab_rows(gg), :] * c + hl_ref[0, _slab_rows(gg), :] + h_bwd(gg) for gg in (g, g + 1)]
        hs_ref[rows, :] = jnp.concatenate(hs, axis=0).astype(hs_ref.dtype)
        pb = [pl_ref[1, _slab_rows(gg), :] * d for gg in (g, g + 1)]
        pb_ref[rows, :] = jnp.concatenate(pb, axis=0).astype(pb_ref.dtype)


def _block_diag_halves(w):
    per_half = N_BLOCKS // 2
    eye = jnp.eye(per_half, dtype=w.dtype)
    w = w.reshape(w.shape[:-3] + (2, per_half, BLOCK_W, BLOCK_W))
    full = jnp.einsum('...hbde,bc->...hbdce', w, eye)
    return full.reshape(w.shape[:-3] + (per_half * BLOCK_W, per_half * BLOCK_W)).astype(_bf16)


def _front(x2, seq_len, ln_g, ln_b, w_in_bf16, layer, n_front, conv_w, conv_b, gate_w, gate_b, lam, apply_ln):
    t = x2.shape[0]
    n = t // TM
    per = TM // SUBLANES
    row = lambda i: (i, 0)
    full = lambda a: pl.BlockSpec(a.shape, lambda i: (0,) * a.ndim)
    wg = _block_diag_halves(gate_w)
    bg = (0.5 * gate_b).reshape(2, 2, 1, W_LRU)
    conv_w = 0.5 * conv_w
    cb2 = (0.5 * conv_b).reshape(1, W_LRU)
    widths = (W_ATTN, W_ATTN, W_ATTN, W_ATTN, W_LRU, W_LRU, W_LRU)
    out_shape = [jax.ShapeDtypeStruct((t, w), _bf16) for w in widths]
    out_specs = [pl.BlockSpec((TM, w), row) for w in widths]
    out_shape.append(jax.ShapeDtypeStruct((n * N_CHUNK, W_LRU), _f32))
    out_specs.append(pl.BlockSpec((N_CHUNK, W_LRU), row))
    if apply_ln:
        out_shape = [jax.ShapeDtypeStruct((t, D_MODEL), _f32)] + out_shape
        out_specs = [pl.BlockSpec((TM, D_MODEL), row)] + out_specs
    stage = pltpu.VMEM((N_LG, N_CHUNK * PITCH, LANES), _f32)
    block_f32 = pltpu.VMEM((TM, W_LRU), _f32)
    return pl.pallas_call(
        functools.partial(_front_kernel, apply_ln=apply_ln, blocks_per_seq=seq_len // TM),
        grid=(n,),
        in_specs=[pl.BlockSpec((TM, D_MODEL), row),
                  pl.BlockSpec((SUBLANES, D_MODEL), lambda i: (jnp.minimum((i + 1) * per, n * per - 1), 0)),
                  full(ln_g), full(ln_b),
                  pl.BlockSpec((None, D_MODEL, n_front), lambda i: (layer, 0, 0)), full(conv_w), full(cb2),
                  full(wg), full(bg), full(lam)],
        out_specs=out_specs,
        out_shape=out_shape,
        scratch_shapes=[pltpu.VMEM((SUBLANES, W_LRU), _f32), pltpu.VMEM((N_CHUNK, W_LRU), _f32),
                        pltpu.VMEM((TM, D_MODEL), _bf16), block_f32, pltpu.VMEM((TM, W_LRU), _bf16),
                        pltpu.VMEM((2, TM, W_LRU), _f32), pltpu.VMEM((2, TM, W_LRU), _f32),
                        pltpu.VMEM((2, TM, W_LRU), _f32), pltpu.VMEM((2, TM, W_LRU), _f32), stage],
        compiler_params=pltpu.CompilerParams(dimension_semantics=("arbitrary",),
                                             vmem_limit_bytes=VMEM_LIMIT),
        name="front",
    )(x2, x2, ln_g, ln_b, w_in_bf16, conv_w, cb2, wg, bg, lam)


def _attn_value_table(rpb):
    n = rpb.shape[0]
    qc = np.arange(GRID_W)[:, None]
    kc = np.arange(GRID_W)[None, :]
    cs = np.clip(qc - WIN_W // 2, 0, GRID_W - WIN_W)
    ok = (kc >= cs) & (kc < cs + WIN_W)
    pad = GRID_W - WIN_W
    padded = jnp.pad(rpb.astype(_f32), ((0, 0), (0, 0), (pad, pad)))
    skew = jnp.tile(padded, (1, 1, GRID_W + 1))[:, :, :GRID_W * 2 * GRID_W]
    skew = skew.reshape(skew.shape[:2] + (GRID_W, 2 * GRID_W))
    t = skew[:, :, ::-1, :GRID_W]
    t = jnp.where(ok, t * LOG2_E, NEG).transpose(0, 2, 1, 3)
    tiles = []
    for ct, k0 in enumerate(KC_START):
        for qr in range(SUB_ROWS):
            a0 = WIN_H // 2 - 1 - qr
            sub = t[:, ct * QC:(ct + 1) * QC, a0:a0 + KEY_ROWS, k0:k0 + KC]
            sub = sub.reshape(n, QC, KEY_ROWS * KC)
            tiles.append(jnp.pad(sub, ((0, 0), (0, 0), (0, KEYS_PAD - KEY_ROWS * KC)), constant_values=NEG))
    return jnp.stack(tiles, axis=1).reshape(n, len(KC_START), SUB_ROWS, QC, KEYS_PAD)


def _attn_row_mask(rows):
    r_first = (0, Q_ROWS, rows - Q_ROWS)
    m = np.full((3, Q_ROWS // SUB_ROWS, SUB_ROWS, SUBLANES, KEYS_PAD), NEG, np.float32)
    m[..., KEY_ROWS * KC:] = 0.0
    for v, r0 in enumerate(r_first):
        for half in range(Q_ROWS // SUB_ROWS):
            for qr in range(SUB_ROWS):
                r = r0 + half * SUB_ROWS + qr
                rs = min(max(r - WIN_H // 2, 0), rows - WIN_H)
                for kr in range(KEY_ROWS):
                    key_r = r0 + half * SUB_ROWS - WIN_H // 2 + kr
                    if rs <= key_r < rs + WIN_H:
                        m[v, half, qr, :, kr * KC:(kr + 1) * KC] = 0.0
    return m


def _attn_kernel(q_ref, *refs):
    k_refs, v_refs = refs[:N_KV_BLKS], refs[N_KV_BLKS:2 * N_KV_BLKS]
    z_ref, val_ref, rowm_ref, o_ref, k32_ref, v32_ref = refs[2 * N_KV_BLKS:]
    m_q = SUB_ROWS * QC
    blk_tok = KV_BLK_ROWS * GRID_W
    lane = lax.broadcasted_iota(jnp.int32, (m_q, LANES), 1)
    even = lane < HEAD_DIM
    for j in range(N_KV_BLKS):
        k32_ref[j * blk_tok:(j + 1) * blk_tok, :] = k_refs[j][...].astype(_f32)
        v32_ref[j * blk_tok:(j + 1) * blk_tok, :] = v_refs[j][...].astype(_f32)
    pad = jnp.zeros((KEYS_PAD - KEY_ROWS * KC, LANES), _f32)

    def key_tile(src_ref, half, k0, cols):
        parts = [src_ref[(half * SUB_ROWS + kr) * GRID_W + k0:(half * SUB_ROWS + kr) * GRID_W + k0 + KC, cols]
                 for kr in range(KEY_ROWS)]
        return jnp.concatenate(parts + [pad], axis=0).astype(_bf16)

    def q_rows(half, ct):
        return [slice((half * SUB_ROWS + qr) * GRID_W + ct * QC, (half * SUB_ROWS + qr) * GRID_W + (ct + 1) * QC)
                for qr in range(SUB_ROWS)]

    def scores(tile):
        half, ct, pair = tile
        cols = slice(pair * LANES, (pair + 1) * LANES)
        kt = key_tile(k32_ref, half, KC_START[ct], cols)
        qp = jnp.concatenate([q_ref[r, cols] for r in q_rows(half, ct)], axis=0)
        zero = jnp.zeros_like(qp)
        q2 = jnp.concatenate([jnp.where(even, qp, zero), jnp.where(even, zero, qp)], axis=0)
        return lax.dot_general(q2, kt, (((1,), (1,)), ((), ())), preferred_element_type=_f32)

    def finish(tile, s):
        half, ct, pair = tile
        cols = slice(pair * LANES, (pair + 1) * LANES)
        blocks = []
        for parity in range(2):
            for qr in range(SUB_ROWS):
                r0 = (parity * SUB_ROWS + qr) * QC
                blk = s[r0:r0 + QC] + val_ref[2 * pair + parity, ct, qr]
                blk = blk.reshape(QC // SUBLANES, SUBLANES, -1) + rowm_ref[half, qr][None]
                blocks.append(blk.reshape(QC, -1))
        s = jnp.concatenate(blocks, axis=0)
        m = jnp.max(s, axis=-1, keepdims=True)
        e = jnp.exp2(s - m)
        l = jnp.sum(e, axis=-1, keepdims=True)
        o2 = _dot(e.astype(_bf16), key_tile(v32_ref, half, KC_START[ct], cols))
        o = jnp.where(even, o2[:m_q] / l[:m_q], o2[m_q:] / l[m_q:])
        z = jnp.concatenate([z_ref[r, cols] for r in q_rows(half, ct)], axis=0).astype(_f32)
        y = (o * z).astype(o_ref.dtype)
        for qr, r in enumerate(q_rows(half, ct)):
            o_ref[r, cols] = y[qr * QC:(qr + 1) * QC]

    tiles = [(half, ct, pair) for half in range(Q_ROWS // SUB_ROWS) for ct in range(len(KC_START))
             for pair in range(N_HEADS // 2)]
    ahead = [scores(tile) for tile in tiles[:SCORE_LOOKAHEAD]]
    for t, tile in enumerate(tiles):
        if t + SCORE_LOOKAHEAD < len(tiles):
            ahead.append(scores(tiles[t + SCORE_LOOKAHEAD]))
        finish(tile, ahead.pop(0))


def _attention(q, k, v, z_a, val, layer, rowm):
    b, s, _ = q.shape
    rows = s // GRID_W
    n_blk = rows // Q_ROWS
    n_kv = rows // KV_BLK_ROWS
    tq = Q_ROWS * GRID_W
    tkv = KV_BLK_ROWS * GRID_W

    def kv_spec(j):
        first = -(WIN_H // 2) // KV_BLK_ROWS
        return pl.BlockSpec(
            (None, tkv, W_ATTN),
            lambda bi, i: (bi, jnp.clip(i * (Q_ROWS // KV_BLK_ROWS) + first + j, 0, n_kv - 1), 0))

    def case(i):
        return jnp.where(i == 0, 0, jnp.where(i == n_blk - 1, 2, 1))

    blk = pl.BlockSpec((None, tq, W_ATTN), lambda bi, i: (bi, i, 0))
    return pl.pallas_call(
        _attn_kernel,
        grid=(b, n_blk),
        in_specs=[blk] + [kv_spec(j) for j in range(N_KV_BLKS)] * 2 + [
            blk,
            pl.BlockSpec((N_HEADS,) + val.shape[1:], lambda bi, i: (layer, 0, 0, 0, 0)),
            pl.BlockSpec((None,) + rowm.shape[1:], lambda bi, i: (case(i), 0, 0, 0, 0))],
        out_specs=blk,
        out_shape=jax.ShapeDtypeStruct((b, s, W_ATTN), _bf16),
        scratch_shapes=[pltpu.VMEM((N_KV_BLKS * tkv, W_ATTN), _f32)] * 2,
        compiler_params=pltpu.CompilerParams(dimension_semantics=("arbitrary", "arbitrary"),
                                             vmem_limit_bytes=VMEM_LIMIT),
        name="attn",
    )(q, *([k] * N_KV_BLKS), *([v] * N_KV_BLKS), z_a, val, rowm)


def _back_kernel(x_ref, ya_ref, hs_ref, pb_ref, hb0f_ref, zb_ref, wga_ref, wgl_ref, bm_ref, wba_ref, wbb_ref,
                 wo_ref, g_ref, b_ref, o_ref, carry_ref, st_ref, *, alpha, blocks_per_seq):
    i = pl.program_id(0)

    @pl.when(i % blocks_per_seq == 0)
    def _():
        carry_ref[...] = jnp.zeros_like(carry_ref)

    carry = carry_ref[0:1, :]
    cin = [None] * (TB // TM)
    for sb in range(TB // TM - 1, -1, -1):
        cin[sb] = carry
        r0 = sb * TM
        carry = hb0f_ref[sb * N_CHUNK:sb * N_CHUNK + 1, :] + pb_ref[r0:r0 + 1, :].astype(_f32) * carry
    carry_ref[0:1, :] = carry

    def rows_of(c):
        return slice(c * BACK_ROWS, (c + 1) * BACK_ROWS)

    for sb in range(TB // TM):
        for g in range(G_LEN):
            rows = slice(sb * TM + g * N_CHUNK, sb * TM + (g + 1) * N_CHUNK)
            slab = hs_ref[rows, :].astype(_f32) + pb_ref[rows, :].astype(_f32) * cin[sb]
            for lg in range(N_LG):
                st_ref[sb * N_LG + lg, pl.ds(g, N_CHUNK, stride=PITCH), :] = slab[:, lg * LANES:(lg + 1) * LANES]

    def h_time_major(rows):
        sb, r0 = divmod(rows.start, TM)
        parts = []
        for ch in range(r0 // G_LEN, (r0 + BACK_ROWS) // G_LEN):
            parts.append(jnp.concatenate([st_ref[sb * N_LG + lg, ch * PITCH:ch * PITCH + G_LEN, :]
                                          for lg in range(N_LG)], axis=1))
        return jnp.concatenate(parts, axis=0)

    def branch_dots(c):
        rows = rows_of(c)
        yb = (h_time_major(rows) * zb_ref[rows, :].astype(_f32)).astype(_bf16)
        return _dot(ya_ref[rows, :], wba_ref[...]), _dot(yb, wbb_ref[...])

    def gate_dots(c):
        xb = x_ref[rows_of(c), :].astype(_bf16)
        return [_sigmoid(_dot(xb, w_ref[:, c0:c0 + PROJ_COLS]) + bm_ref[:, b0 + c0:b0 + c0 + PROJ_COLS])
                for w_ref, b0 in ((wga_ref, 0), (wgl_ref, D_MODEL)) for c0 in range(0, D_MODEL, PROJ_COLS)]

    def out_dot(c, gates, pa, pb):
        g = jnp.concatenate(gates, axis=1)
        m = g[:, :D_MODEL] * pa + g[:, D_MODEL:] * pb
        return _dot(m.astype(_bf16), wo_ref[...])

    def finish(c, out):
        rows = rows_of(c)
        o_ref[rows, :] = _layer_norm(alpha * x_ref[rows, :] + out, g_ref[...], b_ref[...])

    n_c = TB // BACK_ROWS
    ready = {0: (gate_dots(0),) + branch_dots(0)}
    outs = {}
    for c in range(n_c):
        if c + 1 < n_c:
            ready[c + 1] = (gate_dots(c + 1),) + branch_dots(c + 1)
        outs[c] = out_dot(c, *ready.pop(c))
        if c >= 1:
            finish(c - 1, outs.pop(c - 1))
    finish(n_c - 1, outs.pop(n_c - 1))


def _back(x2, seq_len, y_a, h_s, p_b, h_b0f, z_b, w_in_bf16, layer, n_front, b_merge, w_ba, w_bb, w_out,
          ln_g, ln_b, alpha):
    t = x2.shape[0]
    bm2 = b_merge.reshape(1, 2 * D_MODEL)
    n = t // TB
    rev = lambda i: (n - 1 - i, 0)
    rows = lambda w: pl.BlockSpec((TB, w), rev)
    full = lambda a: pl.BlockSpec(a.shape, lambda i: (0, 0))
    of_layer = lambda a: pl.BlockSpec((None,) + a.shape[1:], lambda i: (layer, 0, 0))
    gate_cols = lambda j: pl.BlockSpec((None, D_MODEL, D_MODEL), lambda i: (layer, 0, n_front // D_MODEL + j))
    return pl.pallas_call(
        functools.partial(_back_kernel, alpha=alpha, blocks_per_seq=seq_len // TB),
        grid=(n,),
        in_specs=[rows(D_MODEL), rows(W_ATTN), rows(W_LRU), rows(W_LRU),
                  pl.BlockSpec((TB // TM * N_CHUNK, W_LRU), rev), rows(W_LRU),
                  gate_cols(0), gate_cols(1), full(bm2), of_layer(w_ba), of_layer(w_bb), of_layer(w_out),
                  full(ln_g), full(ln_b)],
        out_specs=rows(D_MODEL),
        out_shape=jax.ShapeDtypeStruct((t, D_MODEL), _f32),
        scratch_shapes=[pltpu.VMEM((SUBLANES, W_LRU), _f32),
                        pltpu.VMEM((TB // TM * N_LG, N_CHUNK * PITCH, LANES), _f32)],
        compiler_params=pltpu.CompilerParams(dimension_semantics=("arbitrary",),
                                             vmem_limit_bytes=VMEM_LIMIT),
        name="back",
    )(x2, y_a, h_s, p_b, h_b0f, z_b, w_in_bf16, w_in_bf16, bm2, w_ba, w_bb, w_out, ln_g, ln_b)


def kernel(x, emb_ln_g, emb_ln_b, w_in, rpb, conv_w, conv_b, lru_gate_w, lru_gate_b, lru_lambda,
           w_branch_attn, w_branch_lru, b_merge, w_out, ln_g, ln_b):
    b, s, d = x.shape
    depth = w_in.shape[0]
    assert d == D_MODEL and s % (Q_ROWS * GRID_W) == 0 and s % TM == 0 and s % TB == 0 and TB % TM == 0
    n_front = w_in.shape[-1] - 2 * D_MODEL
    assert n_front % D_MODEL == 0
    alpha = (2 * depth) ** 0.25
    t = b * s
    h = x.reshape(t, d)
    row2 = lambda a: a.reshape(1, -1)
    rowm = _attn_row_mask(s // GRID_W)
    val = _attn_value_table(rpb.reshape((depth * N_HEADS,) + rpb.shape[2:]))
    w_in_bf16, w_ba, w_bb, w_o = (a.astype(_bf16) for a in (w_in, w_branch_attn, w_branch_lru, w_out))
    for l in range(depth):
        outs = _front(h, s, row2(emb_ln_g), row2(emb_ln_b), w_in_bf16, l, n_front, conv_w[l],
                      conv_b[l], lru_gate_w[l], lru_gate_b[l], lru_lambda[l], apply_ln=(l == 0))
        if l == 0:
            h, outs = outs[0], outs[1:]
        q, k, v, z_a, z_b, h_s, p_b, h_b0f = outs
        seq = lambda a: a.reshape(b, s, a.shape[-1])
        y_a = _attention(seq(q), seq(k), seq(v), seq(z_a), val, l, rowm)
        h = _back(h, s, y_a.reshape(t, W_ATTN), h_s, p_b, h_b0f, z_b, w_in_bf16, l, n_front, b_merge[l],
                  w_ba, w_bb, w_o, row2(ln_g[l]), row2(ln_b[l]), alpha)
    return h.reshape(b, s, d)
```

```python
import functools

import numpy as np
import jax
import jax.numpy as jnp
from jax import lax
from jax.experimental import pallas as pl
from jax.experimental.pallas import tpu as pltpu

D_MODEL = 1024
GRID_W = 64
N_HEADS = 8
HEAD_DIM = 64
W_ATTN = N_HEADS * HEAD_DIM
WIN_H = 8
WIN_W = 16
W_LRU = 512
N_BLOCKS = 8
BLOCK_W = 64
CONV_W = 4
C_LRU = 8.0
LN_EPS = 1e-5
NEG = -1e30
TINY = 1e-30
LOG2_E = 1.4426950408889634

LANES = 128
SUBLANES = 8
VMEM_LIMIT = 56 * 1024 * 1024

TM = 512
TB = 1024
BACK_ROWS = 256
Q_ROWS = 16
SUB_ROWS = 4
KEY_ROWS = SUB_ROWS + WIN_H
KV_BLK_ROWS = 4
N_KV_BLKS = (Q_ROWS + WIN_H) // KV_BLK_ROWS
QC = 32
KC = 40
KC_START = (0, GRID_W - KC)
KEYS_PAD = 512
TABLE_LANES = -(-((WIN_H // 2 - 1) * KC + KEYS_PAD) // LANES) * LANES
SCORE_LOOKAHEAD = 1
N_CHUNK = SUBLANES
G_LEN = TM // N_CHUNK
PITCH = G_LEN + 8
N_LG = W_LRU // LANES
PROJ_COLS = 256
GATE_SLABS = 16

_f32 = jnp.float32
_bf16 = jnp.bfloat16


def _dot(a, b):
    return jnp.dot(a, b, preferred_element_type=_f32)


def _layer_norm(x, g, b):
    mu = jnp.mean(x, axis=-1, keepdims=True)
    xc = x - mu
    var = jnp.mean(xc * xc, axis=-1, keepdims=True)
    return xc * lax.rsqrt(var + LN_EPS) * g + b


def _sigmoid(x):
    return 0.5 + 0.5 * jnp.tanh(0.5 * x)


def _silu(x):
    hx = 0.5 * x
    return hx + hx * jnp.tanh(hx)


def _stage_time_major(u, ut_ref):
    for ch in range(N_CHUNK):
        for lg in range(N_LG):
            ut_ref[lg, ch * PITCH:ch * PITCH + G_LEN, :] = u[ch * G_LEN:(ch + 1) * G_LEN,
                                                            lg * LANES:(lg + 1) * LANES]


def _load_slab(ut_ref, g):
    return jnp.concatenate([ut_ref[lg, pl.ds(g, N_CHUNK, stride=PITCH), :] for lg in range(N_LG)], axis=1)


def _slab_rows(g):
    return slice(g * N_CHUNK, (g + 1) * N_CHUNK)


def _front_kernel(x_ref, xn_ref, g_ref, b_ref, w_ref, cw_ref, cb_ref, wg_ref, bg_ref, lam_ref,
                  *refs, apply_ln, blocks_per_seq):
    n_out = 9 if apply_ln else 8
    out_refs = refs[:n_out]
    (uprev_ref, carry_ref, xb_ref, hu_ref, ub_ref, a_ref, bx_ref, hl_ref, pl_ref,
     ut_ref) = refs[n_out:]
    if apply_ln:
        h_ref, out_refs = out_refs[0], out_refs[1:]
    q_ref, k_ref, v_ref, za_ref, zb_ref, hs_ref, pb_ref, hb0f_ref = out_refs
    j = pl.program_id(0)
    is_first_blk = j % blocks_per_seq == 0
    is_last_blk = j % blocks_per_seq == blocks_per_seq - 1
    u_off = 4 * W_ATTN
    sub = lax.broadcasted_iota(jnp.int32, (N_CHUNK, W_LRU), 0)

    def edge(row):
        return jnp.broadcast_to(row, (N_CHUNK, W_LRU))

    def proj_chunks():
        def chunk(o_ref, off, post, c0):
            def run():
                y = _dot(xb_ref[...], w_ref[:, off + c0:off + c0 + PROJ_COLS])
                o_ref[:, c0:c0 + PROJ_COLS] = post(y, slice(c0, c0 + PROJ_COLS)).astype(o_ref.dtype)
            return run
        plan = ((q_ref, 0, lambda y, cols: y * (HEAD_DIM ** -0.5 * LOG2_E)),
                (k_ref, W_ATTN, lambda y, cols: y),
                (v_ref, 2 * W_ATTN, lambda y, cols: y),
                (za_ref, 3 * W_ATTN, lambda y, cols: _silu(y)),
                (zb_ref, u_off + W_LRU, lambda y, cols: _silu(y)))
        return [chunk(o_ref, off, post, c0) for o_ref, off, post in plan
                for c0 in range(0, o_ref.shape[-1], PROJ_COLS)]

    @pl.when(is_first_blk)
    def _():
        uprev_ref[...] = jnp.zeros_like(uprev_ref)
        carry_ref[...] = jnp.zeros_like(carry_ref)

    x = x_ref[...]
    xn = xn_ref[...]
    if apply_ln:
        x = _layer_norm(x, g_ref[...], b_ref[...])
        xn = _layer_norm(xn, g_ref[...], b_ref[...])
        h_ref[...] = x
    xb_ref[...] = x.astype(_bf16)
    u = _dot(xb_ref[...], w_ref[:, u_off:u_off + W_LRU])
    u_next = _dot(xn.astype(_bf16), w_ref[:, u_off:u_off + W_LRU])
    _stage_time_major(u, ut_ref)

    slabs = {}
    slabs[-2] = jnp.where(sub == 0, edge(uprev_ref[SUBLANES - 2:SUBLANES - 1, :]),
                          pltpu.roll(_load_slab(ut_ref, G_LEN - 2), 1, axis=0))
    slabs[-1] = jnp.where(sub == 0, edge(uprev_ref[SUBLANES - 1:SUBLANES, :]),
                          pltpu.roll(_load_slab(ut_ref, G_LEN - 1), 1, axis=0))
    nxt = jnp.where(is_last_blk, 0.0, u_next[0:1, :])
    slabs[G_LEN] = jnp.where(sub == N_CHUNK - 1, edge(nxt), pltpu.roll(_load_slab(ut_ref, 0), N_CHUNK - 1, axis=0))
    uprev_ref[...] = u[TM - SUBLANES:TM, :]

    def slab(g):
        if g not in slabs:
            slabs[g] = _load_slab(ut_ref, g)
        return slabs[g]

    cw = [cw_ref[i:i + 1, :] for i in range(CONV_W)]
    cb = cb_ref[...]
    neg_lam = -lam_ref[...]
    softplus = jnp.maximum(neg_lam, 0.0) + jnp.log(1.0 + jnp.exp(-jnp.abs(neg_lam)))
    coef = (-0.5 * C_LRU * LOG2_E) * softplus
    first = jnp.logical_and(is_first_blk, sub == 0)
    last = jnp.logical_and(is_last_blk, sub == N_CHUNK - 1)

    n_grp = G_LEN // GATE_SLABS
    grp_rows = lambda r: slice(r * GATE_SLABS * N_CHUNK, (r + 1) * GATE_SLABS * N_CHUNK)

    def conv_group(r):
        for g in range(r * GATE_SLABS, (r + 1) * GATE_SLABS):
            hu_ref[_slab_rows(g), :] = (slab(g - 2) * cw[0] + slab(g - 1) * cw[1] + slab(g) * cw[2]
                                        + slab(g + 1) * cw[3] + cb)
        ub_ref[grp_rows(r), :] = hu_ref[grp_rows(r), :].astype(_bf16)

    def gate_group(r):
        rows = grp_rows(r)
        half = W_LRU // 2
        hu = hu_ref[rows, :]
        for di in range(2):
            pre = [jnp.concatenate([_dot(ub_ref[rows, hf * half:(hf + 1) * half], wg_ref[di, gi, hf])
                                    for hf in range(2)], axis=1) for gi in range(2)]
            t_r = jnp.tanh(pre[0] + bg_ref[di, 0])
            t_i = jnp.tanh(pre[1] + bg_ref[di, 1])
            a = jnp.exp2(coef[di:di + 1] + coef[di:di + 1] * t_r)
            one_m_a2 = (1.0 - a) * (1.0 + a)
            mult = one_m_a2 * lax.rsqrt(jnp.maximum(one_m_a2, TINY))
            iu = hu + hu * t_i
            bx = mult * iu
            a_ref[di, rows, :] = a
            bx_ref[di, rows, :] = bx
            g_fix = (G_LEN - 1) if di else 0
            if r * GATE_SLABS <= g_fix < (r + 1) * GATE_SLABS:
                rel = slice((g_fix - r * GATE_SLABS) * N_CHUNK, (g_fix - r * GATE_SLABS + 1) * N_CHUNK)
                bx_ref[di, _slab_rows(g_fix), :] = jnp.where(last if di else first, iu[rel], bx[rel])

    def scan_slabs(di, state, g_range):
        h, p = state
        for g in g_range:
            rows = _slab_rows(g)
            ag = a_ref[di, rows, :]
            h = ag * h + bx_ref[di, rows, :]
            p = ag * p
            hl_ref[di, rows, :] = h
            pl_ref[di, rows, :] = p
        return h, p

    chunks = proj_chunks()

    def emit(n):
        for _ in range(min(n, len(chunks))):
            chunks.pop(0)()
    zero_state = (jnp.zeros((N_CHUNK, W_LRU), _f32), jnp.ones((N_CHUNK, W_LRU), _f32))

    emit(2)
    conv_group(0)
    state = zero_state
    for r in range(n_grp):
        emit(2)
        if r + 1 < n_grp:
            conv_group(r + 1)
        gate_group(r)
        if r >= 1:
            state = scan_slabs(0, state, range((r - 1) * GATE_SLABS, r * GATE_SLABS))
    emit(2)
    h_end, p_end = scan_slabs(0, state, range((n_grp - 1) * GATE_SLABS, G_LEN))
    c = jnp.where(sub == 0, edge(carry_ref[N_CHUNK - 1:N_CHUNK, :]), 0.0)
    for kk in range(1, N_CHUNK):
        c = jnp.where(sub == kk, pltpu.roll(p_end * c + h_end, 1, axis=0), c)
    carry_ref[...] = p_end * c + h_end

    emit(2)
    h_end, p_end = scan_slabs(1, zero_state, range(G_LEN - 1, -1, -1))
    cz = jnp.zeros((N_CHUNK, W_LRU), _f32)
    d = jnp.where(sub == N_CHUNK - 1, 1.0, 0.0)
    for kk in range(N_CHUNK - 2, -1, -1):
        cz = jnp.where(sub == kk, pltpu.roll(p_end * cz + h_end, N_CHUNK - 1, axis=0), cz)
        d = jnp.where(sub == kk, pltpu.roll(p_end * d, N_CHUNK - 1, axis=0), d)

    def h_bwd(g):
        return pl_ref[1, _slab_rows(g), :] * cz + hl_ref[1, _slab_rows(g), :]

    hb0f_ref[...] = h_bwd(0)
    emit(len(chunks))
    for g in range(0, G_LEN, 2):
        rows = slice(g * N_CHUNK, (g + 2) * N_CHUNK)
        hs = [pl_ref[0, _slab_rows(gg), :] * c + hl_ref[0, _slab_rows(gg), :] + h_bwd(gg) for gg in (g, g + 1)]
        hs_ref[rows, :] = jnp.concatenate(hs, axis=0).astype(hs_ref.dtype)
        pb = [pl_ref[1, _slab_rows(gg), :] * d for gg in (g, g + 1)]
        pb_ref[rows, :] = jnp.concatenate(pb, axis=0).astype(pb_ref.dtype)


def _block_diag_halves(w):
    per_half = N_BLOCKS // 2
    eye = jnp.eye(per_half, dtype=w.dtype)
    w = w.reshape(w.shape[:-3] + (2, per_half, BLOCK_W, BLOCK_W))
    full = jnp.einsum('...hbde,bc->...hbdce', w, eye)
    return full.reshape(w.shape[:-3] + (per_half * BLOCK_W, per_half * BLOCK_W)).astype(_bf16)


def _front(x2, seq_len, ln_g, ln_b, w_in_bf16, layer, n_front, conv_w, conv_b, gate_w, gate_b, lam, apply_ln):
    t = x2.shape[0]
    n = t // TM
    per = TM // SUBLANES
    row = lambda i: (i, 0)
    full = lambda a: pl.BlockSpec(a.shape, lambda i: (0,) * a.ndim)
    wg = _block_diag_halves(gate_w)
    bg = (0.5 * gate_b).reshape(2, 2, 1, W_LRU)
    conv_w = 0.5 * conv_w
    cb2 = (0.5 * conv_b).reshape(1, W_LRU)
    widths = (W_ATTN, W_ATTN, W_ATTN, W_ATTN, W_LRU, W_LRU, W_LRU)
    out_shape = [jax.ShapeDtypeStruct((t, w), _bf16) for w in widths]
    out_specs = [pl.BlockSpec((TM, w), row) for w in widths]
    out_shape.append(jax.ShapeDtypeStruct((n * N_CHUNK, W_LRU), _f32))
    out_specs.append(pl.BlockSpec((N_CHUNK, W_LRU), row))
    if apply_ln:
        out_shape = [jax.ShapeDtypeStruct((t, D_MODEL), _f32)] + out_shape
        out_specs = [pl.BlockSpec((TM, D_MODEL), row)] + out_specs
    stage = pltpu.VMEM((N_LG, N_CHUNK * PITCH, LANES), _f32)
    block_f32 = pltpu.VMEM((TM, W_LRU), _f32)
    return pl.pallas_call(
        functools.partial(_front_kernel, apply_ln=apply_ln, blocks_per_seq=seq_len // TM),
        grid=(n,),
        in_specs=[pl.BlockSpec((TM, D_MODEL), row),
                  pl.BlockSpec((SUBLANES, D_MODEL), lambda i: (jnp.minimum((i + 1) * per, n * per - 1), 0)),
                  full(ln_g), full(ln_b),
                  pl.BlockSpec((None, D_MODEL, n_front), lambda i: (layer, 0, 0)), full(conv_w), full(cb2),
                  full(wg), full(bg), full(lam)],
        out_specs=out_specs,
        out_shape=out_shape,
        scratch_shapes=[pltpu.VMEM((SUBLANES, W_LRU), _f32), pltpu.VMEM((N_CHUNK, W_LRU), _f32),
                        pltpu.VMEM((TM, D_MODEL), _bf16), block_f32, pltpu.VMEM((TM, W_LRU), _bf16),
                        pltpu.VMEM((2, TM, W_LRU), _f32), pltpu.VMEM((2, TM, W_LRU), _f32),
                        pltpu.VMEM((2, TM, W_LRU), _f32), pltpu.VMEM((2, TM, W_LRU), _f32), stage],
        compiler_params=pltpu.CompilerParams(dimension_semantics=("arbitrary",),
                                             vmem_limit_bytes=VMEM_LIMIT),
        name="front",
    )(x2, x2, ln_g, ln_b, w_in_bf16, conv_w, cb2, wg, bg, lam)


def _attn_value_table(rpb):
    n_a = 2 * WIN_H - 1
    q_abs = (np.arange(len(KC_START)) * QC)[:, None, None] + np.arange(QC)[None, :, None]
    k_abs = np.asarray(KC_START)[:, None, None] + np.arange(KC)[None, None, :]
    cs = np.clip(q_abs - WIN_W // 2, 0, GRID_W - WIN_W)
    ok = (k_abs >= cs) & (k_abs < cs + WIN_W)
    rel = k_abs - q_abs + WIN_W - 1
    pick = ((rel[None] == np.arange(2 * WIN_W - 1)[:, None, None, None]) & ok[None]).astype(np.float32)
    t = jnp.einsum('nar,rcqk->ncqak', rpb.astype(_f32) * LOG2_E, pick, precision=lax.Precision.HIGHEST)
    t = jnp.where(ok[None, :, :, None, :], t, NEG).reshape(rpb.shape[0], len(KC_START), QC, n_a * KC)
    return jnp.pad(t, ((0, 0), (0, 0), (0, 0), (0, TABLE_LANES - n_a * KC)), constant_values=NEG)


def _attn_row_mask(rows):
    r_first = (0, Q_ROWS, rows - Q_ROWS)
    m = np.full((3, Q_ROWS // SUB_ROWS, SUB_ROWS, SUBLANES, KEYS_PAD), NEG, np.float32)
    m[..., KEY_ROWS * KC:] = 0.0
    for v, r0 in enumerate(r_first):
        for half in range(Q_ROWS // SUB_ROWS):
            for qr in range(SUB_ROWS):
                r = r0 + half * SUB_ROWS + qr
                rs = min(max(r - WIN_H // 2, 0), rows - WIN_H)
                for kr in range(KEY_ROWS):
                    key_r = r0 + half * SUB_ROWS - WIN_H // 2 + kr
                    if rs <= key_r < rs + WIN_H:
                        m[v, half, qr, :, kr * KC:(kr + 1) * KC] = 0.0
    return m


def _attn_kernel(q_ref, *refs):
    k_refs, v_refs = refs[:N_KV_BLKS], refs[N_KV_BLKS:2 * N_KV_BLKS]
    z_ref, val_ref, rowm_ref, o_ref, k32_ref, v32_ref, tab_ref = refs[2 * N_KV_BLKS:]
    m_q = SUB_ROWS * QC
    blk_tok = KV_BLK_ROWS * GRID_W
    lane = lax.broadcasted_iota(jnp.int32, (m_q, LANES), 1)
    even = lane < HEAD_DIM

    @pl.when(jnp.logical_and(pl.program_id(0) == 0, pl.program_id(1) == 0))
    def _():
        in_tile = lax.broadcasted_iota(jnp.int32, (QC, KEYS_PAD), 1) < KEY_ROWS * KC
        for h in range(N_HEADS):
            for ct in range(len(KC_START)):
                v = val_ref[h, ct]
                for qr in range(SUB_ROWS):
                    off = (WIN_H // 2 - 1 - qr) * KC
                    tab_ref[h, ct, qr] = jnp.where(in_tile, v[:, off:off + KEYS_PAD], NEG)
    for j in range(N_KV_BLKS):
        k32_ref[j * blk_tok:(j + 1) * blk_tok, :] = k_refs[j][...].astype(_f32)
        v32_ref[j * blk_tok:(j + 1) * blk_tok, :] = v_refs[j][...].astype(_f32)
    pad = jnp.zeros((KEYS_PAD - KEY_ROWS * KC, LANES), _f32)

    def key_tile(src_ref, half, k0, cols):
        parts = [src_ref[(half * SUB_ROWS + kr) * GRID_W + k0:(half * SUB_ROWS + kr) * GRID_W + k0 + KC, cols]
                 for kr in range(KEY_ROWS)]
        return jnp.concatenate(parts + [pad], axis=0).astype(_bf16)

    def q_rows(half, ct):
        return [slice((half * SUB_ROWS + qr) * GRID_W + ct * QC, (half * SUB_ROWS + qr) * GRID_W + (ct + 1) * QC)
                for qr in range(SUB_ROWS)]

    def scores(tile):
        half, ct, pair = tile
        cols = slice(pair * LANES, (pair + 1) * LANES)
        kt = key_tile(k32_ref, half, KC_START[ct], cols)
        qp = jnp.concatenate([q_ref[r, cols] for r in q_rows(half, ct)], axis=0)
        zero = jnp.zeros_like(qp)
        q2 = jnp.concatenate([jnp.where(even, qp, zero), jnp.where(even, zero, qp)], axis=0)
        return lax.dot_general(q2, kt, (((1,), (1,)), ((), ())), preferred_element_type=_f32)

    def finish(tile, s):
        half, ct, pair = tile
        cols = slice(pair * LANES, (pair + 1) * LANES)
        blocks = []
        for parity in range(2):
            for qr in range(SUB_ROWS):
                r0 = (parity * SUB_ROWS + qr) * QC
                blk = s[r0:r0 + QC] + tab_ref[2 * pair + parity, ct, qr]
                blk = blk.reshape(QC // SUBLANES, SUBLANES, -1) + rowm_ref[half, qr][None]
                blocks.append(blk.reshape(QC, -1))
        s = jnp.concatenate(blocks, axis=0)
        m = jnp.max(s, axis=-1, keepdims=True)
        e = jnp.exp2(s - m)
        l = jnp.sum(e, axis=-1, keepdims=True)
        o2 = _dot(e.astype(_bf16), key_tile(v32_ref, half, KC_START[ct], cols))
        o = jnp.where(even, o2[:m_q] / l[:m_q], o2[m_q:] / l[m_q:])
        z = jnp.concatenate([z_ref[r, cols] for r in q_rows(half, ct)], axis=0).astype(_f32)
        y = (o * z).astype(o_ref.dtype)
        for qr, r in enumerate(q_rows(half, ct)):
            o_ref[r, cols] = y[qr * QC:(qr + 1) * QC]

    tiles = [(half, ct, pair) for half in range(Q_ROWS // SUB_ROWS) for ct in range(len(KC_START))
             for pair in range(N_HEADS // 2)]
    ahead = [scores(tile) for tile in tiles[:SCORE_LOOKAHEAD]]
    for t, tile in enumerate(tiles):
        if t + SCORE_LOOKAHEAD < len(tiles):
            ahead.append(scores(tiles[t + SCORE_LOOKAHEAD]))
        finish(tile, ahead.pop(0))


def _attention(q, k, v, z_a, val, layer, rowm):
    b, s, _ = q.shape
    rows = s // GRID_W
    n_blk = rows // Q_ROWS
    n_kv = rows // KV_BLK_ROWS
    tq = Q_ROWS * GRID_W
    tkv = KV_BLK_ROWS * GRID_W

    def kv_spec(j):
        first = -(WIN_H // 2) // KV_BLK_ROWS
        return pl.BlockSpec(
            (None, tkv, W_ATTN),
            lambda bi, i: (bi, jnp.clip(i * (Q_ROWS // KV_BLK_ROWS) + first + j, 0, n_kv - 1), 0))

    def case(i):
        return jnp.where(i == 0, 0, jnp.where(i == n_blk - 1, 2, 1))

    blk = pl.BlockSpec((None, tq, W_ATTN), lambda bi, i: (bi, i, 0))
    return pl.pallas_call(
        _attn_kernel,
        grid=(b, n_blk),
        in_specs=[blk] + [kv_spec(j) for j in range(N_KV_BLKS)] * 2 + [
            blk,
            pl.BlockSpec((N_HEADS,) + val.shape[1:], lambda bi, i: (layer, 0, 0, 0)),
            pl.BlockSpec((None,) + rowm.shape[1:], lambda bi, i: (case(i), 0, 0, 0, 0))],
        out_specs=blk,
        out_shape=jax.ShapeDtypeStruct((b, s, W_ATTN), _bf16),
        scratch_shapes=[pltpu.VMEM((N_KV_BLKS * tkv, W_ATTN), _f32)] * 2 + [
            pltpu.VMEM((N_HEADS, len(KC_START), SUB_ROWS, QC, KEYS_PAD), _f32)],
        compiler_params=pltpu.CompilerParams(dimension_semantics=("arbitrary", "arbitrary"),
                                             vmem_limit_bytes=VMEM_LIMIT),
        name="attn",
    )(q, *([k] * N_KV_BLKS), *([v] * N_KV_BLKS), z_a, val, rowm)


def _back_kernel(x_ref, ya_ref, hs_ref, pb_ref, hb0f_ref, zb_ref, wga_ref, wgl_ref, bm_ref, wba_ref, wbb_ref,
                 wo_ref, g_ref, b_ref, o_ref, carry_ref, st_ref, *, alpha, blocks_per_seq):
    i = pl.program_id(0)

    @pl.when(i % blocks_per_seq == 0)
    def _():
        carry_ref[...] = jnp.zeros_like(carry_ref)

    carry = carry_ref[0:1, :]
    cin = [None] * (TB // TM)
    for sb in range(TB // TM - 1, -1, -1):
        cin[sb] = carry
        r0 = sb * TM
        carry = hb0f_ref[sb * N_CHUNK:sb * N_CHUNK + 1, :] + pb_ref[r0:r0 + 1, :].astype(_f32) * carry
    carry_ref[0:1, :] = carry

    def rows_of(c):
        return slice(c * BACK_ROWS, (c + 1) * BACK_ROWS)

    for sb in range(TB // TM):
        for g in range(G_LEN):
            rows = slice(sb * TM + g * N_CHUNK, sb * TM + (g + 1) * N_CHUNK)
            slab = hs_ref[rows, :].astype(_f32) + pb_ref[rows, :].astype(_f32) * cin[sb]
            for lg in range(N_LG):
                st_ref[sb * N_LG + lg, pl.ds(g, N_CHUNK, stride=PITCH), :] = slab[:, lg * LANES:(lg + 1) * LANES]

    def h_time_major(rows):
        sb, r0 = divmod(rows.start, TM)
        parts = []
        for ch in range(r0 // G_LEN, (r0 + BACK_ROWS) // G_LEN):
            parts.append(jnp.concatenate([st_ref[sb * N_LG + lg, ch * PITCH:ch * PITCH + G_LEN, :]
                                          for lg in range(N_LG)], axis=1))
        return jnp.concatenate(parts, axis=0)

    def branch_dots(c):
        rows = rows_of(c)
        yb = (h_time_major(rows) * zb_ref[rows, :].astype(_f32)).astype(_bf16)
        return _dot(ya_ref[rows, :], wba_ref[...]), _dot(yb, wbb_ref[...])

    def gate_dots(c):
        xb = x_ref[rows_of(c), :].astype(_bf16)
        return [_sigmoid(_dot(xb, w_ref[:, c0:c0 + PROJ_COLS]) + bm_ref[:, b0 + c0:b0 + c0 + PROJ_COLS])
                for w_ref, b0 in ((wga_ref, 0), (wgl_ref, D_MODEL)) for c0 in range(0, D_MODEL, PROJ_COLS)]

    def out_dot(c, gates, pa, pb):
        g = jnp.concatenate(gates, axis=1)
        m = g[:, :D_MODEL] * pa + g[:, D_MODEL:] * pb
        return _dot(m.astype(_bf16), wo_ref[...])

    def finish(c, out):
        rows = rows_of(c)
        o_ref[rows, :] = _layer_norm(alpha * x_ref[rows, :] + out, g_ref[...], b_ref[...])

    n_c = TB // BACK_ROWS
    ready = {0: (gate_dots(0),) + branch_dots(0)}
    outs = {}
    for c in range(n_c):
        if c + 1 < n_c:
            ready[c + 1] = (gate_dots(c + 1),) + branch_dots(c + 1)
        outs[c] = out_dot(c, *ready.pop(c))
        if c >= 1:
            finish(c - 1, outs.pop(c - 1))
    finish(n_c - 1, outs.pop(n_c - 1))


def _back(x2, seq_len, y_a, h_s, p_b, h_b0f, z_b, w_in_bf16, layer, n_front, b_merge, w_ba, w_bb, w_out,
          ln_g, ln_b, alpha):
    t = x2.shape[0]
    bm2 = b_merge.reshape(1, 2 * D_MODEL)
    n = t // TB
    rev = lambda i: (n - 1 - i, 0)
    rows = lambda w: pl.BlockSpec((TB, w), rev)
    full = lambda a: pl.BlockSpec(a.shape, lambda i: (0, 0))
    of_layer = lambda a: pl.BlockSpec((None,) + a.shape[1:], lambda i: (layer, 0, 0))
    gate_cols = lambda j: pl.BlockSpec((None, D_MODEL, D_MODEL), lambda i: (layer, 0, n_front // D_MODEL + j))
    return pl.pallas_call(
        functools.partial(_back_kernel, alpha=alpha, blocks_per_seq=seq_len // TB),
        grid=(n,),
        in_specs=[rows(D_MODEL), rows(W_ATTN), rows(W_LRU), rows(W_LRU),
                  pl.BlockSpec((TB // TM * N_CHUNK, W_LRU), rev), rows(W_LRU),
                  gate_cols(0), gate_cols(1), full(bm2), of_layer(w_ba), of_layer(w_bb), of_layer(w_out),
                  full(ln_g), full(ln_b)],
        out_specs=rows(D_MODEL),
        out_shape=jax.ShapeDtypeStruct((t, D_MODEL), _f32),
        scratch_shapes=[pltpu.VMEM((SUBLANES, W_LRU), _f32),
                        pltpu.VMEM((TB // TM * N_LG, N_CHUNK * PITCH, LANES), _f32)],
        compiler_params=pltpu.CompilerParams(dimension_semantics=("arbitrary",),
                                             vmem_limit_bytes=VMEM_LIMIT),
        name="back",
    )(x2, y_a, h_s, p_b, h_b0f, z_b, w_in_bf16, w_in_bf16, bm2, w_ba, w_bb, w_out, ln_g, ln_b)


def kernel(x, emb_ln_g, emb_ln_b, w_in, rpb, conv_w, conv_b, lru_gate_w, lru_gate_b, lru_lambda,
           w_branch_attn, w_branch_lru, b_merge, w_out, ln_g, ln_b):
    b, s, d = x.shape
    depth = w_in.shape[0]
    assert d == D_MODEL and s % (Q_ROWS * GRID_W) == 0 and s % TM == 0 and s % TB == 0 and TB % TM == 0
    n_front = w_in.shape[-1] - 2 * D_MODEL
    assert n_front % D_MODEL == 0
    alpha = (2 * depth) ** 0.25
    t = b * s
    h = x.reshape(t, d)
    row2 = lambda a: a.reshape(1, -1)
    rowm = _attn_row_mask(s // GRID_W)
    val = _attn_value_table(rpb.reshape((depth * N_HEADS,) + rpb.shape[2:]))
    w_in_bf16, w_ba, w_bb, w_o = (a.astype(_bf16) for a in (w_in, w_branch_attn, w_branch_lru, w_out))
    for l in range(depth):
        outs = _front(h, s, row2(emb_ln_g), row2(emb_ln_b), w_in_bf16, l, n_front, conv_w[l],
                      conv_b[l], lru_gate_w[l], lru_gate_b[l], lru_lambda[l], apply_ln=(l == 0))
        if l == 0:
            h, outs = outs[0], outs[1:]
        q, k, v, z_a, z_b, h_s, p_b, h_b0f = outs
        seq = lambda a: a.reshape(b, s, a.shape[-1])
        y_a = _attention(seq(q), seq(k), seq(v), seq(z_a), val, l, rowm)
        h = _back(h, s, y_a.reshape(t, W_ATTN), h_s, p_b, h_b0f, z_b, w_in_bf16, l, n_front, b_merge[l],
                  w_ba, w_bb, w_o, row2(ln_g[l]), row2(ln_b[l]), alpha)
    return h.reshape(b, s, d)
```

```python
import functools

import numpy as np
import jax
import jax.numpy as jnp
from jax import lax
from jax.experimental import pallas as pl
from jax.experimental.pallas import tpu as pltpu

D_MODEL = 1024
GRID_W = 64
N_HEADS = 8
HEAD_DIM = 64
W_ATTN = N_HEADS * HEAD_DIM
WIN_H = 8
WIN_W = 16
W_LRU = 512
N_BLOCKS = 8
BLOCK_W = 64
CONV_W = 4
C_LRU = 8.0
LN_EPS = 1e-5
NEG = -1e30
TINY = 1e-30
LOG2_E = 1.4426950408889634

LANES = 128
SUBLANES = 8
VMEM_LIMIT = 56 * 1024 * 1024

TM = 512
TB = 1024
BACK_ROWS = 256
Q_ROWS = 16
SUB_ROWS = 4
KEY_ROWS = SUB_ROWS + WIN_H
KV_BLK_ROWS = 4
N_KV_BLKS = (Q_ROWS + WIN_H) // KV_BLK_ROWS
QC = 32
KC = 40
KC_START = (0, GRID_W - KC)
KEYS_PAD = 512
SCORE_LOOKAHEAD = 1
N_CHUNK = SUBLANES
G_LEN = TM // N_CHUNK
PITCH = G_LEN + 8
N_LG = W_LRU // LANES
PROJ_COLS = 256
GATE_SLABS = 16

_f32 = jnp.float32
_bf16 = jnp.bfloat16


def _dot(a, b):
    return jnp.dot(a, b, preferred_element_type=_f32)


def _layer_norm(x, g, b):
    mu = jnp.mean(x, axis=-1, keepdims=True)
    xc = x - mu
    var = jnp.mean(xc * xc, axis=-1, keepdims=True)
    return xc * lax.rsqrt(var + LN_EPS) * g + b


def _sigmoid(x):
    return 0.5 + 0.5 * jnp.tanh(0.5 * x)


def _silu(x):
    hx = 0.5 * x
    return hx + hx * jnp.tanh(hx)


def _stage_time_major(u, ut_ref):
    for ch in range(N_CHUNK):
        for lg in range(N_LG):
            ut_ref[lg, ch * PITCH:ch * PITCH + G_LEN, :] = u[ch * G_LEN:(ch + 1) * G_LEN,
                                                            lg * LANES:(lg + 1) * LANES]


def _load_slab(ut_ref, g):
    return jnp.concatenate([ut_ref[lg, pl.ds(g, N_CHUNK, stride=PITCH), :] for lg in range(N_LG)], axis=1)


def _slab_rows(g):
    return slice(g * N_CHUNK, (g + 1) * N_CHUNK)


def _front_kernel(x_ref, xn_ref, g_ref, b_ref, w_ref, cw_ref, cb_ref, wg_ref, bg_ref, lam_ref,
                  *refs, apply_ln, blocks_per_seq):
    n_out = 9 if apply_ln else 8
    out_refs = refs[:n_out]
    (uprev_ref, carry_ref, xb_ref, hu_ref, ub_ref, a_ref, bx_ref, hl_ref, pl_ref,
     ut_ref) = refs[n_out:]
    if apply_ln:
        h_ref, out_refs = out_refs[0], out_refs[1:]
    q_ref, k_ref, v_ref, za_ref, zb_ref, hs_ref, pb_ref, hb0f_ref = out_refs
    j = pl.program_id(0)
    is_first_blk = j % blocks_per_seq == 0
    is_last_blk = j % blocks_per_seq == blocks_per_seq - 1
    u_off = 4 * W_ATTN
    sub = lax.broadcasted_iota(jnp.int32, (N_CHUNK, W_LRU), 0)

    def edge(row):
        return jnp.broadcast_to(row, (N_CHUNK, W_LRU))

    def proj_chunks():
        def chunk(o_ref, off, post, c0):
            def run():
                y = _dot(xb_ref[...], w_ref[:, off + c0:off + c0 + PROJ_COLS])
                o_ref[:, c0:c0 + PROJ_COLS] = post(y, slice(c0, c0 + PROJ_COLS)).astype(o_ref.dtype)
            return run
        plan = ((q_ref, 0, lambda y, cols: y * (HEAD_DIM ** -0.5 * LOG2_E)),
                (k_ref, W_ATTN, lambda y, cols: y),
                (v_ref, 2 * W_ATTN, lambda y, cols: y),
                (za_ref, 3 * W_ATTN, lambda y, cols: _silu(y)),
                (zb_ref, u_off + W_LRU, lambda y, cols: _silu(y)))
        return [chunk(o_ref, off, post, c0) for o_ref, off, post in plan
                for c0 in range(0, o_ref.shape[-1], PROJ_COLS)]

    @pl.when(is_first_blk)
    def _():
        uprev_ref[...] = jnp.zeros_like(uprev_ref)
        carry_ref[...] = jnp.zeros_like(carry_ref)

    x = x_ref[...]
    xn = xn_ref[...]
    if apply_ln:
        x = _layer_norm(x, g_ref[...], b_ref[...])
        xn = _layer_norm(xn, g_ref[...], b_ref[...])
        h_ref[...] = x
    xb_ref[...] = x.astype(_bf16)
    u = _dot(xb_ref[...], w_ref[:, u_off:u_off + W_LRU])
    u_next = _dot(xn.astype(_bf16), w_ref[:, u_off:u_off + W_LRU])
    _stage_time_major(u, ut_ref)

    slabs = {}
    slabs[-2] = jnp.where(sub == 0, edge(uprev_ref[SUBLANES - 2:SUBLANES - 1, :]),
                          pltpu.roll(_load_slab(ut_ref, G_LEN - 2), 1, axis=0))
    slabs[-1] = jnp.where(sub == 0, edge(uprev_ref[SUBLANES - 1:SUBLANES, :]),
                          pltpu.roll(_load_slab(ut_ref, G_LEN - 1), 1, axis=0))
    nxt = jnp.where(is_last_blk, 0.0, u_next[0:1, :])
    slabs[G_LEN] = jnp.where(sub == N_CHUNK - 1, edge(nxt), pltpu.roll(_load_slab(ut_ref, 0), N_CHUNK - 1, axis=0))
    uprev_ref[...] = u[TM - SUBLANES:TM, :]

    def slab(g):
        if g not in slabs:
            slabs[g] = _load_slab(ut_ref, g)
        return slabs[g]

    cw = [cw_ref[i:i + 1, :] for i in range(CONV_W)]
    cb = cb_ref[...]
    neg_lam = -lam_ref[...]
    softplus = jnp.maximum(neg_lam, 0.0) + jnp.log(1.0 + jnp.exp(-jnp.abs(neg_lam)))
    coef = (-0.5 * C_LRU * LOG2_E) * softplus
    first = jnp.logical_and(is_first_blk, sub == 0)
    last = jnp.logical_and(is_last_blk, sub == N_CHUNK - 1)

    n_grp = G_LEN // GATE_SLABS
    grp_rows = lambda r: slice(r * GATE_SLABS * N_CHUNK, (r + 1) * GATE_SLABS * N_CHUNK)

    def conv_group(r):
        for g in range(r * GATE_SLABS, (r + 1) * GATE_SLABS):
            hu_ref[_slab_rows(g), :] = (slab(g - 2) * cw[0] + slab(g - 1) * cw[1] + slab(g) * cw[2]
                                        + slab(g + 1) * cw[3] + cb)
        ub_ref[grp_rows(r), :] = hu_ref[grp_rows(r), :].astype(_bf16)

    def gate_group(r):
        rows = grp_rows(r)
        half = W_LRU // 2
        hu = hu_ref[rows, :]
        for di in range(2):
            pre = [jnp.concatenate([_dot(ub_ref[rows, hf * half:(hf + 1) * half], wg_ref[di, gi, hf])
                                    for hf in range(2)], axis=1) for gi in range(2)]
            t_r = jnp.tanh(pre[0] + bg_ref[di, 0])
            t_i = jnp.tanh(pre[1] + bg_ref[di, 1])
            a = jnp.exp2(coef[di:di + 1] + coef[di:di + 1] * t_r)
            one_m_a2 = (1.0 - a) * (1.0 + a)
            mult = one_m_a2 * lax.rsqrt(jnp.maximum(one_m_a2, TINY))
            iu = hu + hu * t_i
            bx = mult * iu
            a_ref[di, rows, :] = a
            bx_ref[di, rows, :] = bx
            g_fix = (G_LEN - 1) if di else 0
            if r * GATE_SLABS <= g_fix < (r + 1) * GATE_SLABS:
                rel = slice((g_fix - r * GATE_SLABS) * N_CHUNK, (g_fix - r * GATE_SLABS + 1) * N_CHUNK)
                bx_ref[di, _slab_rows(g_fix), :] = jnp.where(last if di else first, iu[rel], bx[rel])

    def scan_slabs(di, state, g_range):
        h, p = state
        for g in g_range:
            rows = _slab_rows(g)
            ag = a_ref[di, rows, :]
            h = ag * h + bx_ref[di, rows, :]
            p = ag * p
            hl_ref[di, rows, :] = h
            pl_ref[di, rows, :] = p
        return h, p

    chunks = proj_chunks()

    def emit(n):
        for _ in range(min(n, len(chunks))):
            chunks.pop(0)()
    zero_state = (jnp.zeros((N_CHUNK, W_LRU), _f32), jnp.ones((N_CHUNK, W_LRU), _f32))

    emit(2)
    conv_group(0)
    state = zero_state
    for r in range(n_grp):
        emit(2)
        if r + 1 < n_grp:
            conv_group(r + 1)
        gate_group(r)
        if r >= 1:
            state = scan_slabs(0, state, range((r - 1) * GATE_SLABS, r * GATE_SLABS))
    emit(2)
    h_end, p_end = scan_slabs(0, state, range((n_grp - 1) * GATE_SLABS, G_LEN))
    c = jnp.where(sub == 0, edge(carry_ref[N_CHUNK - 1:N_CHUNK, :]), 0.0)
    for kk in range(1, N_CHUNK):
        c = jnp.where(sub == kk, pltpu.roll(p_end * c + h_end, 1, axis=0), c)
    carry_ref[...] = p_end * c + h_end

    emit(2)
    h_end, p_end = scan_slabs(1, zero_state, range(G_LEN - 1, -1, -1))
    cz = jnp.zeros((N_CHUNK, W_LRU), _f32)
    d = jnp.where(sub == N_CHUNK - 1, 1.0, 0.0)
    for kk in range(N_CHUNK - 2, -1, -1):
        cz = jnp.where(sub == kk, pltpu.roll(p_end * cz + h_end, N_CHUNK - 1, axis=0), cz)
        d = jnp.where(sub == kk, pltpu.roll(p_end * d, N_CHUNK - 1, axis=0), d)

    def h_bwd(g):
        return pl_ref[1, _slab_rows(g), :] * cz + hl_ref[1, _slab_rows(g), :]

    hb0f_ref[...] = h_bwd(0)
    emit(len(chunks))
    for g in range(0, G_LEN, 2):
        rows = slice(g * N_CHUNK, (g + 2) * N_CHUNK)
        hs = [pl_ref[0, _slab_rows(gg), :] * c + hl_ref[0, _slab_rows(gg), :] + h_bwd(gg) for gg in (g, g + 1)]
        hs_ref[rows, :] = jnp.concatenate(hs, axis=0).astype(hs_ref.dtype)
        pb = [pl_ref[1, _slab_rows(gg), :] * d for gg in (g, g + 1)]
        pb_ref[rows, :] = jnp.concatenate(pb, axis=0).astype(pb_ref.dtype)


def _block_diag_halves(w):
    per_half = N_BLOCKS // 2
    eye = jnp.eye(per_half, dtype=w.dtype)
    w = w.reshape(w.shape[:-3] + (2, per_half, BLOCK_W, BLOCK_W))
    full = jnp.einsum('...hbde,bc->...hbdce', w, eye)
    return full.reshape(w.shape[:-3] + (per_half * BLOCK_W, per_half * BLOCK_W)).astype(_bf16)


def _front(x2, seq_len, ln_g, ln_b, w_in_bf16, layer, n_front, conv_w, conv_b, gate_w, gate_b, lam, apply_ln):
    t = x2.shape[0]
    n = t // TM
    per = TM // SUBLANES
    row = lambda i: (i, 0)
    full = lambda a: pl.BlockSpec(a.shape, lambda i: (0,) * a.ndim)
    wg = _block_diag_halves(gate_w)
    bg = (0.5 * gate_b).reshape(2, 2, 1, W_LRU)
    conv_w = 0.5 * conv_w
    cb2 = (0.5 * conv_b).reshape(1, W_LRU)
    widths = (W_ATTN, W_ATTN, W_ATTN, W_ATTN, W_LRU, W_LRU, W_LRU)
    out_shape = [jax.ShapeDtypeStruct((t, w), _bf16) for w in widths]
    out_specs = [pl.BlockSpec((TM, w), row) for w in widths]
    out_shape.append(jax.ShapeDtypeStruct((n * N_CHUNK, W_LRU), _f32))
    out_specs.append(pl.BlockSpec((N_CHUNK, W_LRU), row))
    if apply_ln:
        out_shape = [jax.ShapeDtypeStruct((t, D_MODEL), _f32)] + out_shape
        out_specs = [pl.BlockSpec((TM, D_MODEL), row)] + out_specs
    stage = pltpu.VMEM((N_LG, N_CHUNK * PITCH, LANES), _f32)
    block_f32 = pltpu.VMEM((TM, W_LRU), _f32)
    return pl.pallas_call(
        functools.partial(_front_kernel, apply_ln=apply_ln, blocks_per_seq=seq_len // TM),
        grid=(n,),
        in_specs=[pl.BlockSpec((TM, D_MODEL), row),
                  pl.BlockSpec((SUBLANES, D_MODEL), lambda i: (jnp.minimum((i + 1) * per, n * per - 1), 0)),
                  full(ln_g), full(ln_b),
                  pl.BlockSpec((None, D_MODEL, n_front), lambda i: (layer, 0, 0)), full(conv_w), full(cb2),
                  full(wg), full(bg), full(lam)],
        out_specs=out_specs,
        out_shape=out_shape,
        scratch_shapes=[pltpu.VMEM((SUBLANES, W_LRU), _f32), pltpu.VMEM((N_CHUNK, W_LRU), _f32),
                        pltpu.VMEM((TM, D_MODEL), _bf16), block_f32, pltpu.VMEM((TM, W_LRU), _bf16),
                        pltpu.VMEM((2, TM, W_LRU), _f32), pltpu.VMEM((2, TM, W_LRU), _f32),
                        pltpu.VMEM((2, TM, W_LRU), _f32), pltpu.VMEM((2, TM, W_LRU), _f32), stage],
        compiler_params=pltpu.CompilerParams(dimension_semantics=("arbitrary",),
                                             vmem_limit_bytes=VMEM_LIMIT),
        name="front",
    )(x2, x2, ln_g, ln_b, w_in_bf16, conv_w, cb2, wg, bg, lam)


def _attn_column_window():
    q_abs = (np.arange(len(KC_START)) * QC)[:, None, None] + np.arange(QC)[None, :, None]
    k_abs = np.asarray(KC_START)[:, None, None] + np.arange(KC)[None, None, :]
    cs = np.clip(q_abs - WIN_W // 2, 0, GRID_W - WIN_W)
    return (k_abs >= cs) & (k_abs < cs + WIN_W), k_abs - q_abs + WIN_W - 1


def _attn_value_table(rpb):
    ok, rel = _attn_column_window()
    pick = (rel[None] == np.arange(2 * WIN_W - 1)[:, None, None, None]) & ok[None]
    pick = np.pad(pick.astype(np.float32), ((0, 0), (0, 0), (0, 0), (0, LANES - KC)))
    return jnp.einsum('nar,rcqk->nacqk', rpb.astype(_f32) * LOG2_E, pick, precision=lax.Precision.HIGHEST)


def _attn_column_mask():
    ok, _ = _attn_column_window()
    m = np.zeros((len(KC_START), QC, KEYS_PAD), np.float32)
    m[:, :, :KEY_ROWS * KC] = np.tile(ok.astype(np.float32), (1, 1, KEY_ROWS))
    return m


def _attn_row_mask(rows):
    r_first = (0, Q_ROWS, rows - Q_ROWS)
    m = np.full((3, Q_ROWS // SUB_ROWS, SUB_ROWS, SUBLANES, KEYS_PAD), NEG, np.float32)
    m[..., KEY_ROWS * KC:] = 0.0
    for v, r0 in enumerate(r_first):
        for half in range(Q_ROWS // SUB_ROWS):
            for qr in range(SUB_ROWS):
                r = r0 + half * SUB_ROWS + qr
                rs = min(max(r - WIN_H // 2, 0), rows - WIN_H)
                for kr in range(KEY_ROWS):
                    key_r = r0 + half * SUB_ROWS - WIN_H // 2 + kr
                    if rs <= key_r < rs + WIN_H:
                        m[v, half, qr, :, kr * KC:(kr + 1) * KC] = 0.0
    return m


def _attn_kernel(q_ref, *refs):
    k_refs, v_refs = refs[:N_KV_BLKS], refs[N_KV_BLKS:2 * N_KV_BLKS]
    z_ref, val_ref, colm_ref, rowm_ref, o_ref, k32_ref, v32_ref, tab_ref = refs[2 * N_KV_BLKS:]
    m_q = SUB_ROWS * QC
    blk_tok = KV_BLK_ROWS * GRID_W
    lane = lax.broadcasted_iota(jnp.int32, (m_q, LANES), 1)
    even = lane < HEAD_DIM

    @pl.when(jnp.logical_and(pl.program_id(0) == 0, pl.program_id(1) == 0))
    def _():
        lane_q = lax.broadcasted_iota(jnp.int32, (QC, LANES), 1)
        for h in range(N_HEADS):
            for ct in range(len(KC_START)):
                for qr in range(SUB_ROWS):
                    a0 = WIN_H // 2 - 1 - qr
                    acc = [jnp.zeros((QC, LANES), _f32)] * (KEYS_PAD // LANES)
                    for kr in range(KEY_ROWS):
                        j, o = divmod(kr * KC, LANES)
                        piece = val_ref[h, a0 + kr, ct]
                        piece = pltpu.roll(piece, o, axis=1) if o else piece
                        over = o + KC - LANES
                        if over <= 0:
                            acc[j] = acc[j] + piece
                        else:
                            acc[j] = acc[j] + jnp.where(lane_q >= o, piece, 0.0)
                            acc[j + 1] = acc[j + 1] + jnp.where(lane_q < over, piece, 0.0)
                    tab_ref[h, ct, qr] = jnp.where(colm_ref[ct] > 0.5, jnp.concatenate(acc, axis=1), NEG)
    for j in range(N_KV_BLKS):
        k32_ref[j * blk_tok:(j + 1) * blk_tok, :] = k_refs[j][...].astype(_f32)
        v32_ref[j * blk_tok:(j + 1) * blk_tok, :] = v_refs[j][...].astype(_f32)
    pad = jnp.zeros((KEYS_PAD - KEY_ROWS * KC, LANES), _f32)

    def key_tile(src_ref, half, k0, cols):
        parts = [src_ref[(half * SUB_ROWS + kr) * GRID_W + k0:(half * SUB_ROWS + kr) * GRID_W + k0 + KC, cols]
                 for kr in range(KEY_ROWS)]
        return jnp.concatenate(parts + [pad], axis=0).astype(_bf16)

    def q_rows(half, ct):
        return [slice((half * SUB_ROWS + qr) * GRID_W + ct * QC, (half * SUB_ROWS + qr) * GRID_W + (ct + 1) * QC)
                for qr in range(SUB_ROWS)]

    def scores(tile):
        half, ct, pair = tile
        cols = slice(pair * LANES, (pair + 1) * LANES)
        kt = key_tile(k32_ref, half, KC_START[ct], cols)
        qp = jnp.concatenate([q_ref[r, cols] for r in q_rows(half, ct)], axis=0)
        zero = jnp.zeros_like(qp)
        q2 = jnp.concatenate([jnp.where(even, qp, zero), jnp.where(even, zero, qp)], axis=0)
        return lax.dot_general(q2, kt, (((1,), (1,)), ((), ())), preferred_element_type=_f32)

    def finish(tile, s):
        half, ct, pair = tile
        cols = slice(pair * LANES, (pair + 1) * LANES)
        blocks = []
        for parity in range(2):
            for qr in range(SUB_ROWS):
                r0 = (parity * SUB_ROWS + qr) * QC
                blk = s[r0:r0 + QC] + tab_ref[2 * pair + parity, ct, qr]
                blk = blk.reshape(QC // SUBLANES, SUBLANES, -1) + rowm_ref[half, qr][None]
                blocks.append(blk.reshape(QC, -1))
        s = jnp.concatenate(blocks, axis=0)
        m = jnp.max(s, axis=-1, keepdims=True)
        e = jnp.exp2(s - m)
        l = jnp.sum(e, axis=-1, keepdims=True)
        o2 = _dot(e.astype(_bf16), key_tile(v32_ref, half, KC_START[ct], cols))
        o = jnp.where(even, o2[:m_q] / l[:m_q], o2[m_q:] / l[m_q:])
        z = jnp.concatenate([z_ref[r, cols] for r in q_rows(half, ct)], axis=0).astype(_f32)
        y = (o * z).astype(o_ref.dtype)
        for qr, r in enumerate(q_rows(half, ct)):
            o_ref[r, cols] = y[qr * QC:(qr + 1) * QC]

    tiles = [(half, ct, pair) for half in range(Q_ROWS // SUB_ROWS) for ct in range(len(KC_START))
             for pair in range(N_HEADS // 2)]
    ahead = [scores(tile) for tile in tiles[:SCORE_LOOKAHEAD]]
    for t, tile in enumerate(tiles):
        if t + SCORE_LOOKAHEAD < len(tiles):
            ahead.append(scores(tiles[t + SCORE_LOOKAHEAD]))
        finish(tile, ahead.pop(0))


def _attention(q, k, v, z_a, val, layer, rowm):
    b, s, _ = q.shape
    colm = _attn_column_mask()
    rows = s // GRID_W
    n_blk = rows // Q_ROWS
    n_kv = rows // KV_BLK_ROWS
    tq = Q_ROWS * GRID_W
    tkv = KV_BLK_ROWS * GRID_W

    def kv_spec(j):
        first = -(WIN_H // 2) // KV_BLK_ROWS
        return pl.BlockSpec(
            (None, tkv, W_ATTN),
            lambda bi, i: (bi, jnp.clip(i * (Q_ROWS // KV_BLK_ROWS) + first + j, 0, n_kv - 1), 0))

    def case(i):
        return jnp.where(i == 0, 0, jnp.where(i == n_blk - 1, 2, 1))

    blk = pl.BlockSpec((None, tq, W_ATTN), lambda bi, i: (bi, i, 0))
    return pl.pallas_call(
        _attn_kernel,
        grid=(b, n_blk),
        in_specs=[blk] + [kv_spec(j) for j in range(N_KV_BLKS)] * 2 + [
            blk,
            pl.BlockSpec((N_HEADS,) + val.shape[1:], lambda bi, i: (layer, 0, 0, 0, 0)),
            pl.BlockSpec(colm.shape, lambda bi, i: (0, 0, 0)),
            pl.BlockSpec((None,) + rowm.shape[1:], lambda bi, i: (case(i), 0, 0, 0, 0))],
        out_specs=blk,
        out_shape=jax.ShapeDtypeStruct((b, s, W_ATTN), _bf16),
        scratch_shapes=[pltpu.VMEM((N_KV_BLKS * tkv, W_ATTN), _f32)] * 2 + [
            pltpu.VMEM((N_HEADS, len(KC_START), SUB_ROWS, QC, KEYS_PAD), _f32)],
        compiler_params=pltpu.CompilerParams(dimension_semantics=("arbitrary", "arbitrary"),
                                             vmem_limit_bytes=VMEM_LIMIT),
        name="attn",
    )(q, *([k] * N_KV_BLKS), *([v] * N_KV_BLKS), z_a, val, colm, rowm)


def _back_kernel(x_ref, ya_ref, hs_ref, pb_ref, hb0f_ref, zb_ref, wga_ref, wgl_ref, bm_ref, wba_ref, wbb_ref,
                 wo_ref, g_ref, b_ref, o_ref, carry_ref, st_ref, *, alpha, blocks_per_seq):
    i = pl.program_id(0)

    @pl.when(i % blocks_per_seq == 0)
    def _():
        carry_ref[...] = jnp.zeros_like(carry_ref)

    carry = carry_ref[0:1, :]
    cin = [None] * (TB // TM)
    for sb in range(TB // TM - 1, -1, -1):
        cin[sb] = carry
        r0 = sb * TM
        carry = hb0f_ref[sb * N_CHUNK:sb * N_CHUNK + 1, :] + pb_ref[r0:r0 + 1, :].astype(_f32) * carry
    carry_ref[0:1, :] = carry

    def rows_of(c):
        return slice(c * BACK_ROWS, (c + 1) * BACK_ROWS)

    for sb in range(TB // TM):
        for g in range(G_LEN):
            rows = slice(sb * TM + g * N_CHUNK, sb * TM + (g + 1) * N_CHUNK)
            slab = hs_ref[rows, :].astype(_f32) + pb_ref[rows, :].astype(_f32) * cin[sb]
            for lg in range(N_LG):
                st_ref[sb * N_LG + lg, pl.ds(g, N_CHUNK, stride=PITCH), :] = slab[:, lg * LANES:(lg + 1) * LANES]

    def h_time_major(rows):
        sb, r0 = divmod(rows.start, TM)
        parts = []
        for ch in range(r0 // G_LEN, (r0 + BACK_ROWS) // G_LEN):
            parts.append(jnp.concatenate([st_ref[sb * N_LG + lg, ch * PITCH:ch * PITCH + G_LEN, :]
                                          for lg in range(N_LG)], axis=1))
        return jnp.concatenate(parts, axis=0)

    def branch_dots(c):
        rows = rows_of(c)
        yb = (h_time_major(rows) * zb_ref[rows, :].astype(_f32)).astype(_bf16)
        return _dot(ya_ref[rows, :], wba_ref[...]), _dot(yb, wbb_ref[...])

    def gate_dots(c):
        xb = x_ref[rows_of(c), :].astype(_bf16)
        return [_sigmoid(_dot(xb, w_ref[:, c0:c0 + PROJ_COLS]) + bm_ref[:, b0 + c0:b0 + c0 + PROJ_COLS])
                for w_ref, b0 in ((wga_ref, 0), (wgl_ref, D_MODEL)) for c0 in range(0, D_MODEL, PROJ_COLS)]

    def out_dot(c, gates, pa, pb):
        g = jnp.concatenate(gates, axis=1)
        m = g[:, :D_MODEL] * pa + g[:, D_MODEL:] * pb
        return _dot(m.astype(_bf16), wo_ref[...])

    def finish(c, out):
        rows = rows_of(c)
        o_ref[rows, :] = _layer_norm(alpha * x_ref[rows, :] + out, g_ref[...], b_ref[...])

    n_c = TB // BACK_ROWS
    ready = {0: (gate_dots(0),) + branch_dots(0)}
    outs = {}
    for c in range(n_c):
        if c + 1 < n_c:
            ready[c + 1] = (gate_dots(c + 1),) + branch_dots(c + 1)
        outs[c] = out_dot(c, *ready.pop(c))
        if c >= 1:
            finish(c - 1, outs.pop(c - 1))
    finish(n_c - 1, outs.pop(n_c - 1))


def _back(x2, seq_len, y_a, h_s, p_b, h_b0f, z_b, w_in_bf16, layer, n_front, b_merge, w_ba, w_bb, w_out,
          ln_g, ln_b, alpha):
    t = x2.shape[0]
    bm2 = b_merge.reshape(1, 2 * D_MODEL)
    n = t // TB
    rev = lambda i: (n - 1 - i, 0)
    rows = lambda w: pl.BlockSpec((TB, w), rev)
    full = lambda a: pl.BlockSpec(a.shape, lambda i: (0, 0))
    of_layer = lambda a: pl.BlockSpec((None,) + a.shape[1:], lambda i: (layer, 0, 0))
    gate_cols = lambda j: pl.BlockSpec((None, D_MODEL, D_MODEL), lambda i: (layer, 0, n_front // D_MODEL + j))
    return pl.pallas_call(
        functools.partial(_back_kernel, alpha=alpha, blocks_per_seq=seq_len // TB),
        grid=(n,),
        in_specs=[rows(D_MODEL), rows(W_ATTN), rows(W_LRU), rows(W_LRU),
                  pl.BlockSpec((TB // TM * N_CHUNK, W_LRU), rev), rows(W_LRU),
                  gate_cols(0), gate_cols(1), full(bm2), of_layer(w_ba), of_layer(w_bb), of_layer(w_out),
                  full(ln_g), full(ln_b)],
        out_specs=rows(D_MODEL),
        out_shape=jax.ShapeDtypeStruct((t, D_MODEL), _f32),
        scratch_shapes=[pltpu.VMEM((SUBLANES, W_LRU), _f32),
                        pltpu.VMEM((TB // TM * N_LG, N_CHUNK * PITCH, LANES), _f32)],
        compiler_params=pltpu.CompilerParams(dimension_semantics=("arbitrary",),
                                             vmem_limit_bytes=VMEM_LIMIT),
        name="back",
    )(x2, y_a, h_s, p_b, h_b0f, z_b, w_in_bf16, w_in_bf16, bm2, w_ba, w_bb, w_out, ln_g, ln_b)


def kernel(x, emb_ln_g, emb_ln_b, w_in, rpb, conv_w, conv_b, lru_gate_w, lru_gate_b, lru_lambda,
           w_branch_attn, w_branch_lru, b_merge, w_out, ln_g, ln_b):
    b, s, d = x.shape
    depth = w_in.shape[0]
    assert d == D_MODEL and s % (Q_ROWS * GRID_W) == 0 and s % TM == 0 and s % TB == 0 and TB % TM == 0
    n_front = w_in.shape[-1] - 2 * D_MODEL
    assert n_front % D_MODEL == 0
    alpha = (2 * depth) ** 0.25
    t = b * s
    h = x.reshape(t, d)
    row2 = lambda a: a.reshape(1, -1)
    rowm = _attn_row_mask(s // GRID_W)
    val = _attn_value_table(rpb.reshape((depth * N_HEADS,) + rpb.shape[2:]))
    w_in_bf16, w_ba, w_bb, w_o = (a.astype(_bf16) for a in (w_in, w_branch_attn, w_branch_lru, w_out))
    for l in range(depth):
        outs = _front(h, s, row2(emb_ln_g), row2(emb_ln_b), w_in_bf16, l, n_front, conv_w[l],
                      conv_b[l], lru_gate_w[l], lru_gate_b[l], lru_lambda[l], apply_ln=(l == 0))
        if l == 0:
            h, outs = outs[0], outs[1:]
        q, k, v, z_a, z_b, h_s, p_b, h_b0f = outs
        seq = lambda a: a.reshape(b, s, a.shape[-1])
        y_a = _attention(seq(q), seq(k), seq(v), seq(z_a), val, l, rowm)
        h = _back(h, s, y_a.reshape(t, W_ATTN), h_s, p_b, h_b0f, z_b, w_in_bf16, l, n_front, b_merge[l],
                  w_ba, w_bb, w_o, row2(ln_g[l]), row2(ln_b[l]), alpha)
    return h.reshape(b, s, d)
```

```python
import functools

import numpy as np
import jax
import jax.numpy as jnp
from jax import lax
from jax.experimental import pallas as pl
from jax.experimental.pallas import tpu as pltpu

D_MODEL = 1024
GRID_W = 64
N_HEADS = 8
HEAD_DIM = 64
W_ATTN = N_HEADS * HEAD_DIM
WIN_H = 8
WIN_W = 16
W_LRU = 512
N_BLOCKS = 8
BLOCK_W = 64
CONV_W = 4
C_LRU = 8.0
LN_EPS = 1e-5
NEG = -1e30
TINY = 1e-30
LOG2_E = 1.4426950408889634

LANES = 128
SUBLANES = 8
VMEM_LIMIT = 56 * 1024 * 1024

TM = 512
TB = 1024
BACK_ROWS = 256
Q_ROWS = 16
SUB_ROWS = 4
KEY_ROWS = SUB_ROWS + WIN_H
KV_BLK_ROWS = 4
N_KV_BLKS = (Q_ROWS + WIN_H) // KV_BLK_ROWS
QC = 32
KC = 40
KC_START = (0, GRID_W - KC)
KEYS_PAD = 512
TABLE_LANES = -(-((WIN_H // 2 - 1) * KC + KEYS_PAD) // LANES) * LANES
SCORE_LOOKAHEAD = 1
N_CHUNK = SUBLANES
G_LEN = TM // N_CHUNK
PITCH = G_LEN + 8
N_LG = W_LRU // LANES
PROJ_COLS = 256
GATE_SLABS = 16

_f32 = jnp.float32
_bf16 = jnp.bfloat16


def _dot(a, b):
    return jnp.dot(a, b, preferred_element_type=_f32)


def _layer_norm(x, g, b):
    mu = jnp.mean(x, axis=-1, keepdims=True)
    xc = x - mu
    var = jnp.mean(xc * xc, axis=-1, keepdims=True)
    return xc * lax.rsqrt(var + LN_EPS) * g + b


def _sigmoid(x):
    return 0.5 + 0.5 * jnp.tanh(0.5 * x)


def _silu(x):
    hx = 0.5 * x
    return hx + hx * jnp.tanh(hx)


def _stage_time_major(u, ut_ref):
    for ch in range(N_CHUNK):
        for lg in range(N_LG):
            ut_ref[lg, ch * PITCH:ch * PITCH + G_LEN, :] = u[ch * G_LEN:(ch + 1) * G_LEN,
                                                            lg * LANES:(lg + 1) * LANES]


def _load_slab(ut_ref, g):
    return jnp.concatenate([ut_ref[lg, pl.ds(g, N_CHUNK, stride=PITCH), :] for lg in range(N_LG)], axis=1)


def _slab_rows(g):
    return slice(g * N_CHUNK, (g + 1) * N_CHUNK)


def _front_kernel(x_ref, xn_ref, g_ref, b_ref, w_ref, cw_ref, cb_ref, wg_ref, bg_ref, lam_ref,
                  *refs, apply_ln, blocks_per_seq):
    n_out = 9 if apply_ln else 8
    out_refs = refs[:n_out]
    (uprev_ref, carry_ref, xb_ref, hu_ref, ub_ref, a_ref, bx_ref, hl_ref, pl_ref,
     ut_ref) = refs[n_out:]
    if apply_ln:
        h_ref, out_refs = out_refs[0], out_refs[1:]
    q_ref, k_ref, v_ref, za_ref, zb_ref, hs_ref, pb_ref, hb0f_ref = out_refs
    j = pl.program_id(0)
    is_first_blk = j % blocks_per_seq == 0
    is_last_blk = j % blocks_per_seq == blocks_per_seq - 1
    u_off = 4 * W_ATTN
    sub = lax.broadcasted_iota(jnp.int32, (N_CHUNK, W_LRU), 0)

    def edge(row):
        return jnp.broadcast_to(row, (N_CHUNK, W_LRU))

    def proj_chunks():
        def chunk(o_ref, off, post, c0):
            def run():
                y = _dot(xb_ref[...], w_ref[:, off + c0:off + c0 + PROJ_COLS])
                o_ref[:, c0:c0 + PROJ_COLS] = post(y, slice(c0, c0 + PROJ_COLS)).astype(o_ref.dtype)
            return run
        plan = ((q_ref, 0, lambda y, cols: y * (HEAD_DIM ** -0.5 * LOG2_E)),
                (k_ref, W_ATTN, lambda y, cols: y),
                (v_ref, 2 * W_ATTN, lambda y, cols: y),
                (za_ref, 3 * W_ATTN, lambda y, cols: _silu(y)),
                (zb_ref, u_off + W_LRU, lambda y, cols: _silu(y)))
        return [chunk(o_ref, off, post, c0) for o_ref, off, post in plan
                for c0 in range(0, o_ref.shape[-1], PROJ_COLS)]

    @pl.when(is_first_blk)
    def _():
        uprev_ref[...] = jnp.zeros_like(uprev_ref)
        carry_ref[...] = jnp.zeros_like(carry_ref)

    x = x_ref[...]
    xn = xn_ref[...]
    if apply_ln:
        x = _layer_norm(x, g_ref[...], b_ref[...])
        xn = _layer_norm(xn, g_ref[...], b_ref[...])
        h_ref[...] = x
    xb_ref[...] = x.astype(_bf16)
    u = _dot(xb_ref[...], w_ref[:, u_off:u_off + W_LRU])
    u_next = _dot(xn.astype(_bf16), w_ref[:, u_off:u_off + W_LRU])
    _stage_time_major(u, ut_ref)

    slabs = {}
    slabs[-2] = jnp.where(sub == 0, edge(uprev_ref[SUBLANES - 2:SUBLANES - 1, :]),
                          pltpu.roll(_load_slab(ut_ref, G_LEN - 2), 1, axis=0))
    slabs[-1] = jnp.where(sub == 0, edge(uprev_ref[SUBLANES - 1:SUBLANES, :]),
                          pltpu.roll(_load_slab(ut_ref, G_LEN - 1), 1, axis=0))
    nxt = jnp.where(is_last_blk, 0.0, u_next[0:1, :])
    slabs[G_LEN] = jnp.where(sub == N_CHUNK - 1, edge(nxt), pltpu.roll(_load_slab(ut_ref, 0), N_CHUNK - 1, axis=0))
    uprev_ref[...] = u[TM - SUBLANES:TM, :]

    def slab(g):
        if g not in slabs:
            slabs[g] = _load_slab(ut_ref, g)
        return slabs[g]

    cw = [cw_ref[i:i + 1, :] for i in range(CONV_W)]
    cb = cb_ref[...]
    neg_lam = -lam_ref[...]
    softplus = jnp.maximum(neg_lam, 0.0) + jnp.log(1.0 + jnp.exp(-jnp.abs(neg_lam)))
    coef = (-0.5 * C_LRU * LOG2_E) * softplus
    first = jnp.logical_and(is_first_blk, sub == 0)
    last = jnp.logical_and(is_last_blk, sub == N_CHUNK - 1)

    n_grp = G_LEN // GATE_SLABS
    grp_rows = lambda r: slice(r * GATE_SLABS * N_CHUNK, (r + 1) * GATE_SLABS * N_CHUNK)

    def conv_group(r):
        for g in range(r * GATE_SLABS, (r + 1) * GATE_SLABS):
            hu_ref[_slab_rows(g), :] = (slab(g - 2) * cw[0] + slab(g - 1) * cw[1] + slab(g) * cw[2]
                                        + slab(g + 1) * cw[3] + cb)
        ub_ref[grp_rows(r), :] = hu_ref[grp_rows(r), :].astype(_bf16)

    def gate_group(r):
        rows = grp_rows(r)
        half = W_LRU // 2
        hu = hu_ref[rows, :]
        for di in range(2):
            pre = [jnp.concatenate([_dot(ub_ref[rows, hf * half:(hf + 1) * half], wg_ref[di, gi, hf])
                                    for hf in range(2)], axis=1) for gi in range(2)]
            t_r = jnp.tanh(pre[0] + bg_ref[di, 0])
            t_i = jnp.tanh(pre[1] + bg_ref[di, 1])
            a = jnp.exp2(coef[di:di + 1] + coef[di:di + 1] * t_r)
            one_m_a2 = (1.0 - a) * (1.0 + a)
            mult = one_m_a2 * lax.rsqrt(jnp.maximum(one_m_a2, TINY))
            iu = hu + hu * t_i
            bx = mult * iu
            a_ref[di, rows, :] = a
            bx_ref[di, rows, :] = bx
            g_fix = (G_LEN - 1) if di else 0
            if r * GATE_SLABS <= g_fix < (r + 1) * GATE_SLABS:
                rel = slice((g_fix - r * GATE_SLABS) * N_CHUNK, (g_fix - r * GATE_SLABS + 1) * N_CHUNK)
                bx_ref[di, _slab_rows(g_fix), :] = jnp.where(last if di else first, iu[rel], bx[rel])

    def scan_slabs(di, state, g_range):
        h, p = state
        for g in g_range:
            rows = _slab_rows(g)
            ag = a_ref[di, rows, :]
            h = ag * h + bx_ref[di, rows, :]
            p = ag * p
            hl_ref[di, rows, :] = h
            pl_ref[di, rows, :] = p
        return h, p

    chunks = proj_chunks()

    def emit(n):
        for _ in range(min(n, len(chunks))):
            chunks.pop(0)()
    zero_state = (jnp.zeros((N_CHUNK, W_LRU), _f32), jnp.ones((N_CHUNK, W_LRU), _f32))

    emit(2)
    conv_group(0)
    state = zero_state
    for r in range(n_grp):
        emit(2)
        if r + 1 < n_grp:
            conv_group(r + 1)
        gate_group(r)
        if r >= 1:
            state = scan_slabs(0, state, range((r - 1) * GATE_SLABS, r * GATE_SLABS))
    emit(2)
    h_end, p_end = scan_slabs(0, state, range((n_grp - 1) * GATE_SLABS, G_LEN))
    c = jnp.where(sub == 0, edge(carry_ref[N_CHUNK - 1:N_CHUNK, :]), 0.0)
    for kk in range(1, N_CHUNK):
        c = jnp.where(sub == kk, pltpu.roll(p_end * c + h_end, 1, axis=0), c)
    carry_ref[...] = p_end * c + h_end

    emit(2)
    h_end, p_end = scan_slabs(1, zero_state, range(G_LEN - 1, -1, -1))
    cz = jnp.zeros((N_CHUNK, W_LRU), _f32)
    d = jnp.where(sub == N_CHUNK - 1, 1.0, 0.0)
    for kk in range(N_CHUNK - 2, -1, -1):
        cz = jnp.where(sub == kk, pltpu.roll(p_end * cz + h_end, N_CHUNK - 1, axis=0), cz)
        d = jnp.where(sub == kk, pltpu.roll(p_end * d, N_CHUNK - 1, axis=0), d)

    def h_bwd(g):
        return pl_ref[1, _slab_rows(g), :] * cz + hl_ref[1, _slab_rows(g), :]

    hb0f_ref[...] = h_bwd(0)
    emit(len(chunks))
    for g in range(0, G_LEN, 2):
        rows = slice(g * N_CHUNK, (g + 2) * N_CHUNK)
        hs = [pl_ref[0, _slab_rows(gg), :] * c + hl_ref[0, _slab_rows(gg), :] + h_bwd(gg) for gg in (g, g + 1)]
        hs_ref[rows, :] = jnp.concatenate(hs, axis=0).astype(hs_ref.dtype)
        pb = [pl_ref[1, _slab_rows(gg), :] * d for gg in (g, g + 1)]
        pb_ref[rows, :] = jnp.concatenate(pb, axis=0).astype(pb_ref.dtype)


def _block_diag_halves(w):
    per_half = N_BLOCKS // 2
    eye = jnp.eye(per_half, dtype=w.dtype)
    w = w.reshape(w.shape[:-3] + (2, per_half, BLOCK_W, BLOCK_W))
    full = jnp.einsum('...hbde,bc->...hbdce', w, eye)
    return full.reshape(w.shape[:-3] + (per_half * BLOCK_W, per_half * BLOCK_W)).astype(_bf16)


def _front(x2, seq_len, ln_g, ln_b, w_in_bf16, layer, n_front, conv_w, conv_b, gate_w, gate_b, lam, apply_ln):
    t = x2.shape[0]
    n = t // TM
    per = TM // SUBLANES
    row = lambda i: (i, 0)
    once = pl.Buffered(1)
    full = lambda a: pl.BlockSpec(a.shape, lambda i: (0,) * a.ndim, pipeline_mode=once)
    wg = _block_diag_halves(gate_w)
    bg = (0.5 * gate_b).reshape(2, 2, 1, W_LRU)
    conv_w = 0.5 * conv_w
    cb2 = (0.5 * conv_b).reshape(1, W_LRU)
    widths = (W_ATTN, W_ATTN, W_ATTN, W_ATTN, W_LRU, W_LRU, W_LRU)
    out_shape = [jax.ShapeDtypeStruct((t, w), _bf16) for w in widths]
    out_specs = [pl.BlockSpec((TM, w), row) for w in widths]
    out_shape.append(jax.ShapeDtypeStruct((n * N_CHUNK, W_LRU), _f32))
    out_specs.append(pl.BlockSpec((N_CHUNK, W_LRU), row))
    if apply_ln:
        out_shape = [jax.ShapeDtypeStruct((t, D_MODEL), _f32)] + out_shape
        out_specs = [pl.BlockSpec((TM, D_MODEL), row)] + out_specs
    stage = pltpu.VMEM((N_LG, N_CHUNK * PITCH, LANES), _f32)
    block_f32 = pltpu.VMEM((TM, W_LRU), _f32)
    return pl.pallas_call(
        functools.partial(_front_kernel, apply_ln=apply_ln, blocks_per_seq=seq_len // TM),
        grid=(n,),
        in_specs=[pl.BlockSpec((TM, D_MODEL), row),
                  pl.BlockSpec((SUBLANES, D_MODEL), lambda i: (jnp.minimum((i + 1) * per, n * per - 1), 0)),
                  full(ln_g), full(ln_b),
                  pl.BlockSpec((None, D_MODEL, n_front), lambda i: (layer, 0, 0), pipeline_mode=once),
                  full(conv_w), full(cb2),
                  full(wg), full(bg), full(lam)],
        out_specs=out_specs,
        out_shape=out_shape,
        scratch_shapes=[pltpu.VMEM((SUBLANES, W_LRU), _f32), pltpu.VMEM((N_CHUNK, W_LRU), _f32),
                        pltpu.VMEM((TM, D_MODEL), _bf16), block_f32, pltpu.VMEM((TM, W_LRU), _bf16),
                        pltpu.VMEM((2, TM, W_LRU), _f32), pltpu.VMEM((2, TM, W_LRU), _f32),
                        pltpu.VMEM((2, TM, W_LRU), _f32), pltpu.VMEM((2, TM, W_LRU), _f32), stage],
        compiler_params=pltpu.CompilerParams(dimension_semantics=("arbitrary",),
                                             vmem_limit_bytes=VMEM_LIMIT),
        name="front",
    )(x2, x2, ln_g, ln_b, w_in_bf16, conv_w, cb2, wg, bg, lam)


def _attn_value_table(rpb):
    n_a = 2 * WIN_H - 1
    q_abs = (np.arange(len(KC_START)) * QC)[:, None, None] + np.arange(QC)[None, :, None]
    k_abs = np.asarray(KC_START)[:, None, None] + np.arange(KC)[None, None, :]
    cs = np.clip(q_abs - WIN_W // 2, 0, GRID_W - WIN_W)
    ok = (k_abs >= cs) & (k_abs < cs + WIN_W)
    rel = k_abs - q_abs + WIN_W - 1
    pick = ((rel[None] == np.arange(2 * WIN_W - 1)[:, None, None, None]) & ok[None]).astype(np.float32)
    t = jnp.einsum('nar,rcqk->ncqak', rpb.astype(_f32) * LOG2_E, pick, precision=lax.Precision.HIGHEST)
    t = jnp.where(ok[None, :, :, None, :], t, NEG).reshape(rpb.shape[0], len(KC_START), QC, n_a * KC)
    return jnp.pad(t, ((0, 0), (0, 0), (0, 0), (0, TABLE_LANES - n_a * KC)), constant_values=NEG)


def _attn_row_mask(rows):
    r_first = (0, Q_ROWS, rows - Q_ROWS)
    m = np.full((3, Q_ROWS // SUB_ROWS, SUB_ROWS, SUBLANES, KEYS_PAD), NEG, np.float32)
    m[..., KEY_ROWS * KC:] = 0.0
    for v, r0 in enumerate(r_first):
        for half in range(Q_ROWS // SUB_ROWS):
            for qr in range(SUB_ROWS):
                r = r0 + half * SUB_ROWS + qr
                rs = min(max(r - WIN_H // 2, 0), rows - WIN_H)
                for kr in range(KEY_ROWS):
                    key_r = r0 + half * SUB_ROWS - WIN_H // 2 + kr
                    if rs <= key_r < rs + WIN_H:
                        m[v, half, qr, :, kr * KC:(kr + 1) * KC] = 0.0
    return m


def _attn_kernel(q_ref, *refs):
    k_refs, v_refs = refs[:N_KV_BLKS], refs[N_KV_BLKS:2 * N_KV_BLKS]
    z_ref, val_ref, rowm_ref, o_ref, k32_ref, v32_ref, tab_ref = refs[2 * N_KV_BLKS:]
    m_q = SUB_ROWS * QC
    blk_tok = KV_BLK_ROWS * GRID_W
    lane = lax.broadcasted_iota(jnp.int32, (m_q, LANES), 1)
    even = lane < HEAD_DIM

    @pl.when(jnp.logical_and(pl.program_id(0) == 0, pl.program_id(1) == 0))
    def _():
        in_tile = lax.broadcasted_iota(jnp.int32, (QC, KEYS_PAD), 1) < KEY_ROWS * KC
        for h in range(N_HEADS):
            for ct in range(len(KC_START)):
                v = val_ref[h, ct]
                for qr in range(SUB_ROWS):
                    off = (WIN_H // 2 - 1 - qr) * KC
                    tab_ref[h, ct, qr] = jnp.where(in_tile, v[:, off:off + KEYS_PAD], NEG)
    for j in range(N_KV_BLKS):
        k32_ref[j * blk_tok:(j + 1) * blk_tok, :] = k_refs[j][...].astype(_f32)
        v32_ref[j * blk_tok:(j + 1) * blk_tok, :] = v_refs[j][...].astype(_f32)
    pad = jnp.zeros((KEYS_PAD - KEY_ROWS * KC, LANES), _f32)

    def key_tile(src_ref, half, k0, cols):
        parts = [src_ref[(half * SUB_ROWS + kr) * GRID_W + k0:(half * SUB_ROWS + kr) * GRID_W + k0 + KC, cols]
                 for kr in range(KEY_ROWS)]
        return jnp.concatenate(parts + [pad], axis=0).astype(_bf16)

    def q_rows(half, ct):
        return [slice((half * SUB_ROWS + qr) * GRID_W + ct * QC, (half * SUB_ROWS + qr) * GRID_W + (ct + 1) * QC)
                for qr in range(SUB_ROWS)]

    def scores(tile):
        half, ct, pair = tile
        cols = slice(pair * LANES, (pair + 1) * LANES)
        kt = key_tile(k32_ref, half, KC_START[ct], cols)
        qp = jnp.concatenate([q_ref[r, cols] for r in q_rows(half, ct)], axis=0)
        zero = jnp.zeros_like(qp)
        q2 = jnp.concatenate([jnp.where(even, qp, zero), jnp.where(even, zero, qp)], axis=0)
        return lax.dot_general(q2, kt, (((1,), (1,)), ((), ())), preferred_element_type=_f32)

    def finish(tile, s):
        half, ct, pair = tile
        cols = slice(pair * LANES, (pair + 1) * LANES)
        blocks = []
        for parity in range(2):
            for qr in range(SUB_ROWS):
                r0 = (parity * SUB_ROWS + qr) * QC
                blk = s[r0:r0 + QC] + tab_ref[2 * pair + parity, ct, qr]
                blk = blk.reshape(QC // SUBLANES, SUBLANES, -1) + rowm_ref[half, qr][None]
                blocks.append(blk.reshape(QC, -1))
        s = jnp.concatenate(blocks, axis=0)
        m = jnp.max(s, axis=-1, keepdims=True)
        e = jnp.exp2(s - m)
        l = jnp.sum(e, axis=-1, keepdims=True)
        o2 = _dot(e.astype(_bf16), key_tile(v32_ref, half, KC_START[ct], cols))
        o = jnp.where(even, o2[:m_q] / l[:m_q], o2[m_q:] / l[m_q:])
        z = jnp.concatenate([z_ref[r, cols] for r in q_rows(half, ct)], axis=0).astype(_f32)
        y = (o * z).astype(o_ref.dtype)
        for qr, r in enumerate(q_rows(half, ct)):
            o_ref[r, cols] = y[qr * QC:(qr + 1) * QC]

    tiles = [(half, ct, pair) for half in range(Q_ROWS // SUB_ROWS) for ct in range(len(KC_START))
             for pair in range(N_HEADS // 2)]
    ahead = [scores(tile) for tile in tiles[:SCORE_LOOKAHEAD]]
    for t, tile in enumerate(tiles):
        if t + SCORE_LOOKAHEAD < len(tiles):
            ahead.append(scores(tiles[t + SCORE_LOOKAHEAD]))
        finish(tile, ahead.pop(0))


def _attention(q, k, v, z_a, val, layer, rowm):
    b, s, _ = q.shape
    rows = s // GRID_W
    n_blk = rows // Q_ROWS
    n_kv = rows // KV_BLK_ROWS
    tq = Q_ROWS * GRID_W
    tkv = KV_BLK_ROWS * GRID_W

    def kv_spec(j):
        first = -(WIN_H // 2) // KV_BLK_ROWS
        return pl.BlockSpec(
            (None, tkv, W_ATTN),
            lambda bi, i: (bi, jnp.clip(i * (Q_ROWS // KV_BLK_ROWS) + first + j, 0, n_kv - 1), 0))

    def case(i):
        return jnp.where(i == 0, 0, jnp.where(i == n_blk - 1, 2, 1))

    blk = pl.BlockSpec((None, tq, W_ATTN), lambda bi, i: (bi, i, 0))
    return pl.pallas_call(
        _attn_kernel,
        grid=(b, n_blk),
        in_specs=[blk] + [kv_spec(j) for j in range(N_KV_BLKS)] * 2 + [
            blk,
            pl.BlockSpec((N_HEADS,) + val.shape[1:], lambda bi, i: (layer, 0, 0, 0),
                         pipeline_mode=pl.Buffered(1)),
            pl.BlockSpec((None,) + rowm.shape[1:], lambda bi, i: (case(i), 0, 0, 0, 0))],
        out_specs=blk,
        out_shape=jax.ShapeDtypeStruct((b, s, W_ATTN), _bf16),
        scratch_shapes=[pltpu.VMEM((N_KV_BLKS * tkv, W_ATTN), _f32)] * 2 + [
            pltpu.VMEM((N_HEADS, len(KC_START), SUB_ROWS, QC, KEYS_PAD), _f32)],
        compiler_params=pltpu.CompilerParams(dimension_semantics=("arbitrary", "arbitrary"),
                                             vmem_limit_bytes=VMEM_LIMIT),
        name="attn",
    )(q, *([k] * N_KV_BLKS), *([v] * N_KV_BLKS), z_a, val, rowm)


def _back_kernel(x_ref, ya_ref, hs_ref, pb_ref, hb0f_ref, zb_ref, wga_ref, wgl_ref, bm_ref, wba_ref, wbb_ref,
                 wo_ref, g_ref, b_ref, o_ref, carry_ref, st_ref, *, alpha, blocks_per_seq):
    i = pl.program_id(0)

    @pl.when(i % blocks_per_seq == 0)
    def _():
        carry_ref[...] = jnp.zeros_like(carry_ref)

    carry = carry_ref[0:1, :]
    cin = [None] * (TB // TM)
    for sb in range(TB // TM - 1, -1, -1):
        cin[sb] = carry
        r0 = sb * TM
        carry = hb0f_ref[sb * N_CHUNK:sb * N_CHUNK + 1, :] + pb_ref[r0:r0 + 1, :].astype(_f32) * carry
    carry_ref[0:1, :] = carry

    def rows_of(c):
        return slice(c * BACK_ROWS, (c + 1) * BACK_ROWS)

    for sb in range(TB // TM):
        for g in range(G_LEN):
            rows = slice(sb * TM + g * N_CHUNK, sb * TM + (g + 1) * N_CHUNK)
            slab = hs_ref[rows, :].astype(_f32) + pb_ref[rows, :].astype(_f32) * cin[sb]
            for lg in range(N_LG):
                st_ref[sb * N_LG + lg, pl.ds(g, N_CHUNK, stride=PITCH), :] = slab[:, lg * LANES:(lg + 1) * LANES]

    def h_time_major(rows):
        sb, r0 = divmod(rows.start, TM)
        parts = []
        for ch in range(r0 // G_LEN, (r0 + BACK_ROWS) // G_LEN):
            parts.append(jnp.concatenate([st_ref[sb * N_LG + lg, ch * PITCH:ch * PITCH + G_LEN, :]
                                          for lg in range(N_LG)], axis=1))
        return jnp.concatenate(parts, axis=0)

    def branch_dots(c):
        rows = rows_of(c)
        yb = (h_time_major(rows) * zb_ref[rows, :].astype(_f32)).astype(_bf16)
        return _dot(ya_ref[rows, :], wba_ref[...]), _dot(yb, wbb_ref[...])

    def gate_dots(c):
        xb = x_ref[rows_of(c), :].astype(_bf16)
        return [_sigmoid(_dot(xb, w_ref[:, c0:c0 + PROJ_COLS]) + bm_ref[:, b0 + c0:b0 + c0 + PROJ_COLS])
                for w_ref, b0 in ((wga_ref, 0), (wgl_ref, D_MODEL)) for c0 in range(0, D_MODEL, PROJ_COLS)]

    def out_dot(c, gates, pa, pb):
        g = jnp.concatenate(gates, axis=1)
        m = g[:, :D_MODEL] * pa + g[:, D_MODEL:] * pb
        return _dot(m.astype(_bf16), wo_ref[...])

    def finish(c, out):
        rows = rows_of(c)
        o_ref[rows, :] = _layer_norm(alpha * x_ref[rows, :] + out, g_ref[...], b_ref[...])

    n_c = TB // BACK_ROWS
    ready = {0: (gate_dots(0),) + branch_dots(0)}
    outs = {}
    for c in range(n_c):
        if c + 1 < n_c:
            ready[c + 1] = (gate_dots(c + 1),) + branch_dots(c + 1)
        outs[c] = out_dot(c, *ready.pop(c))
        if c >= 1:
            finish(c - 1, outs.pop(c - 1))
    finish(n_c - 1, outs.pop(n_c - 1))


def _back(x2, seq_len, y_a, h_s, p_b, h_b0f, z_b, w_in_bf16, layer, n_front, b_merge, w_ba, w_bb, w_out,
          ln_g, ln_b, alpha):
    t = x2.shape[0]
    bm2 = b_merge.reshape(1, 2 * D_MODEL)
    n = t // TB
    rev = lambda i: (n - 1 - i, 0)
    rows = lambda w: pl.BlockSpec((TB, w), rev)
    once = pl.Buffered(1)
    full = lambda a: pl.BlockSpec(a.shape, lambda i: (0, 0), pipeline_mode=once)
    of_layer = lambda a: pl.BlockSpec((None,) + a.shape[1:], lambda i: (layer, 0, 0), pipeline_mode=once)
    gate_cols = lambda j: pl.BlockSpec((None, D_MODEL, D_MODEL), lambda i: (layer, 0, n_front // D_MODEL + j),
                                       pipeline_mode=once)
    return pl.pallas_call(
        functools.partial(_back_kernel, alpha=alpha, blocks_per_seq=seq_len // TB),
        grid=(n,),
        in_specs=[rows(D_MODEL), rows(W_ATTN), rows(W_LRU), rows(W_LRU),
                  pl.BlockSpec((TB // TM * N_CHUNK, W_LRU), rev), rows(W_LRU),
                  gate_cols(0), gate_cols(1), full(bm2), of_layer(w_ba), of_layer(w_bb), of_layer(w_out),
                  full(ln_g), full(ln_b)],
        out_specs=rows(D_MODEL),
        out_shape=jax.ShapeDtypeStruct((t, D_MODEL), _f32),
        scratch_shapes=[pltpu.VMEM((SUBLANES, W_LRU), _f32),
                        pltpu.VMEM((TB // TM * N_LG, N_CHUNK * PITCH, LANES), _f32)],
        compiler_params=pltpu.CompilerParams(dimension_semantics=("arbitrary",),
                                             vmem_limit_bytes=VMEM_LIMIT),
        name="back",
    )(x2, y_a, h_s, p_b, h_b0f, z_b, w_in_bf16, w_in_bf16, bm2, w_ba, w_bb, w_out, ln_g, ln_b)


def kernel(x, emb_ln_g, emb_ln_b, w_in, rpb, conv_w, conv_b, lru_gate_w, lru_gate_b, lru_lambda,
           w_branch_attn, w_branch_lru, b_merge, w_out, ln_g, ln_b):
    b, s, d = x.shape
    depth = w_in.shape[0]
    assert d == D_MODEL and s % (Q_ROWS * GRID_W) == 0 and s % TM == 0 and s % TB == 0 and TB % TM == 0
    n_front = w_in.shape[-1] - 2 * D_MODEL
    assert n_front % D_MODEL == 0
    alpha = (2 * depth) ** 0.25
    t = b * s
    h = x.reshape(t, d)
    row2 = lambda a: a.reshape(1, -1)
    rowm = _attn_row_mask(s // GRID_W)
    val = _attn_value_table(rpb.reshape((depth * N_HEADS,) + rpb.shape[2:]))
    w_in_bf16, w_ba, w_bb, w_o = (a.astype(_bf16) for a in (w_in, w_branch_attn, w_branch_lru, w_out))
    for l in range(depth):
        outs = _front(h, s, row2(emb_ln_g), row2(emb_ln_b), w_in_bf16, l, n_front, conv_w[l],
                      conv_b[l], lru_gate_w[l], lru_gate_b[l], lru_lambda[l], apply_ln=(l == 0))
        if l == 0:
            h, outs = outs[0], outs[1:]
        q, k, v, z_a, z_b, h_s, p_b, h_b0f = outs
        seq = lambda a: a.reshape(b, s, a.shape[-1])
        y_a = _attention(seq(q), seq(k), seq(v), seq(z_a), val, l, rowm)
        h = _back(h, s, y_a.reshape(t, W_ATTN), h_s, p_b, h_b0f, z_b, w_in_bf16, l, n_front, b_merge[l],
                  w_ba, w_bb, w_o, row2(ln_g[l]), row2(ln_b[l]), alpha)
    return h.reshape(b, s, d)
```
